```python
import math
import jax, jax.numpy as jnp
from jax import lax
import numpy as np

D_MODEL = 1024
BATCH = 16
SEQ = 2048
DEPTH = 4
DEC_BATCH = 8
DEC_SEQ = 32
PAST_LEN = 1024

CHUNK = 64
N_A = DEPTH // 2
N_B = DEPTH - N_A
HG_HEADS = 8
HG_DK = D_MODEL // HG_HEADS
HG_DV = D_MODEL // HG_HEADS
GLA_BLOCK = 16
DA_HEADS = 8
DA_HEAD_DIM = D_MODEL // (2 * DA_HEADS)
Q_BLOCK = 128
NUM_BUCKETS = 32
MAX_DISTANCE = 256
D_FF = 2816
CONV_W = 3
PLE_DIM = 256
EPS = 1e-6

kernel_name = 'yoco_hgrn2_diffattn_streaming_step'


def rmsnorm(x, g):
    xf = x.astype(jnp.float32)
    y = xf * lax.rsqrt(jnp.mean(xf * xf, axis=-1, keepdims=True) + EPS)
    return (y * g.astype(jnp.float32)).astype(x.dtype)


def head_rmsnorm(x, g):
    return rmsnorm(x, g.reshape(x.shape[-2], x.shape[-1]))


def t5_bucket(rel):
    half = NUM_BUCKETS // 2
    ret = jnp.where(rel > 0, half, 0)
    n = jnp.abs(rel)
    max_exact = half // 2
    nf = jnp.maximum(n, 1).astype(jnp.float32)
    large = max_exact + (jnp.log(nf / max_exact) / math.log(MAX_DISTANCE / max_exact)
                         * (half - max_exact)).astype(jnp.int32)
    large = jnp.minimum(large, half - 1)
    return ret + jnp.where(n < max_exact, n, large)


def gla_blocked(q, k, v, logf, s0):
    B, T, H, dk = q.shape
    dv = v.shape[-1]
    pad = (-T) % GLA_BLOCK
    if pad:
        pw = ((0, 0), (0, pad), (0, 0), (0, 0))
        q, k, v, logf = [jnp.pad(a, pw) for a in (q, k, v, logf)]
    n = (T + pad) // GLA_BLOCK

    def blocks(a):
        return a.astype(jnp.float32).reshape(B, n, GLA_BLOCK, H, a.shape[-1]).transpose(1, 0, 2, 3, 4)

    qs, ks, vs, ls = blocks(q), blocks(k), blocks(v), blocks(logf)
    causal = jnp.tril(jnp.ones((GLA_BLOCK, GLA_BLOCK), dtype=bool))

    def step(S, inp):
        qb, kb, vb, lb = inp
        b = jnp.cumsum(lb, axis=1)
        b_last = b[:, -1:]
        qt = qb * jnp.exp(b)
        kt = kb * jnp.exp(-b)
        a = jnp.where(causal, jnp.einsum('blhk,bmhk->bhlm', qt, kt), 0.0)
        o = jnp.einsum('blhk,bhkv->blhv', qt, S) + jnp.einsum('bhlm,bmhv->blhv', a, vb)
        kd = kb * jnp.exp(b_last - b)
        S = jnp.exp(b_last[:, 0])[..., None] * S + jnp.einsum('bmhk,bmhv->bhkv', kd, vb)
        return S, o

    S, o = lax.scan(step, s0.astype(jnp.float32), (qs, ks, vs, ls))
    o = o.transpose(1, 0, 2, 3, 4).reshape(B, n * GLA_BLOCK, H, dv)[:, :T]
    return o, S


def hgrn2(h, s0, w_in, lower, g_out_norm, w_out):
    B, T, _ = h.shape
    qh, fh, ih, gh = jnp.split(h @ w_in, 4, axis=-1)
    fp = fh.astype(jnp.float32)
    logf = jnp.logaddexp(jnp.log(lower), jnp.log1p(-lower) + jax.nn.log_sigmoid(fp))
    kk = (1.0 - lower) * jax.nn.sigmoid(-fp)
    shp = (B, T, HG_HEADS, HG_DK)
    o, s = gla_blocked(jax.nn.silu(qh).reshape(shp), kk.reshape(shp),
                       ih.reshape(B, T, HG_HEADS, HG_DV), logf.reshape(shp), s0)
    o = head_rmsnorm(o, g_out_norm).astype(h.dtype) * jax.nn.silu(gh).reshape(B, T, HG_HEADS, HG_DV)
    return o.reshape(B, T, D_MODEL) @ w_out, s.astype(s0.dtype)


def diff_attn_core(q, k, v, q_pos, k_pos, rel_bias, lam):
    B, K = k.shape[0], k.shape[1]
    k = k.reshape(B, K, DA_HEADS, 2, DA_HEAD_DIM)
    s = jnp.einsum('bqhcd,bkhcd->bchqk', q, k).astype(jnp.float32) * (DA_HEAD_DIM ** -0.5)
    rel = k_pos[None, :] - q_pos[:, None]
    bias = jnp.transpose(rel_bias[t5_bucket(rel)], (2, 0, 1)).astype(jnp.float32)
    mask = (k_pos[None, :] // CHUNK) <= (q_pos[:, None] // CHUNK)
    p = jax.nn.softmax(jnp.where(mask, s + bias, -jnp.inf), axis=-1)
    w = p[:, 0] - lam * p[:, 1]
    return jnp.einsum('bhqk,bkhe->bqhe', w.astype(v.dtype), v)


def diff_attention(h, k_all, v_all, q_pos, k_pos, w_q, lam_p, g_subln, w_out, rel_bias, layer_idx, blocked):
    B, T, _ = h.shape
    q = (h @ w_q).reshape(B, T, DA_HEADS, 2, DA_HEAD_DIM)
    lam_init = 0.8 - 0.6 * math.exp(-0.3 * layer_idx)
    lp = lam_p.astype(jnp.float32)
    lam = jnp.exp(jnp.sum(lp[0] * lp[1])) - jnp.exp(jnp.sum(lp[2] * lp[3])) + lam_init
    if blocked:
        nb = T // Q_BLOCK
        qb = q.reshape(B, nb, Q_BLOCK, DA_HEADS, 2, DA_HEAD_DIM).transpose(1, 0, 2, 3, 4, 5)
        qpb = q_pos.reshape(nb, Q_BLOCK)
        o = lax.map(lambda a: diff_attn_core(a[0], k_all, v_all, a[1], k_pos, rel_bias, lam), (qb, qpb))
        o = o.transpose(1, 0, 2, 3, 4).reshape(B, T, DA_HEADS, 2 * DA_HEAD_DIM)
    else:
        o = diff_attn_core(q, k_all, v_all, q_pos, k_pos, rel_bias, lam)
    o = head_rmsnorm(o, g_subln) * (1.0 - lam_init)
    return o.reshape(B, T, D_MODEL) @ w_out


def conv_ffn(h, buf, w_in, cw, cb, w_out):
    g, u = jnp.split(h @ w_in, 2, axis=-1)
    T = g.shape[1]
    gp = jnp.concatenate([buf.astype(g.dtype), g], axis=1)
    c = cb
    for j in range(CONV_W):
        c = c + cw[j] * gp[:, j:j + T]
    return (jax.nn.silu(c) * u) @ w_out, gp[:, -(CONV_W - 1):]


def trunk(x, ple, pos, past_k, past_v, hg_s0, conv_s0, blocked,
          g_norms, w_in_a, lb_raw, g_hg, w_out_a, g_kv, w_kv, rel_bias,
          w_q_b, lam_b, g_subln, w_out_b, w_ffn_in, conv_w, conv_b, w_ffn_out,
          w_ple, w_ple_gate):
    B, T, _ = x.shape
    sm = jax.nn.softmax(lb_raw.astype(jnp.float32), axis=0)
    cs = jnp.cumsum(sm, axis=0)
    lower = cs - cs[0:1]
    hg_out, conv_out = [], []
    k_all = v_all = k_new = v_new = k_pos = None
    for i in range(DEPTH):
        h = rmsnorm(x, g_norms[i, 0])
        if i < N_A:
            m, s = hgrn2(h, hg_s0[i], w_in_a[i], lower[i], g_hg[i], w_out_a[i])
            hg_out.append(s)
        else:
            j = i - N_A
            m = diff_attention(h, k_all, v_all, pos, k_pos, w_q_b[j], lam_b[j], g_subln[j],
                               w_out_b[j], rel_bias, i, blocked)
        x = x + rmsnorm(m, g_norms[i, 1])
        h = rmsnorm(x, g_norms[i, 2])
        f, cbuf = conv_ffn(h, conv_s0[i], w_ffn_in[i], conv_w[i], conv_b[i], w_ffn_out[i])
        conv_out.append(cbuf)
        x = x + rmsnorm(f, g_norms[i, 3])
        x = x + jax.nn.sigmoid(x @ w_ple_gate[i]) * (ple[i] @ w_ple[i])
        if i == N_A - 1:
            kn, vn = jnp.split(rmsnorm(x, g_kv) @ w_kv, 2, axis=-1)
            k_new = kn.reshape(B, T, DA_HEADS, 2 * DA_HEAD_DIM)
            v_new = vn.reshape(B, T, DA_HEADS, 2 * DA_HEAD_DIM)
            if past_k is None:
                k_all, v_all, k_pos = k_new, v_new, pos
            else:
                k_all = jnp.concatenate([past_k.astype(k_new.dtype), k_new], axis=1)
                v_all = jnp.concatenate([past_v.astype(v_new.dtype), v_new], axis=1)
                k_pos = jnp.arange(past_k.shape[1] + T, dtype=jnp.int32)
    return x, k_new, v_new, jnp.stack(hg_out), jnp.stack(conv_out)


def setup_inputs(seed: int = 0) -> dict:
    key = jax.random.key(seed)
    ks = jax.random.split(key, 32)
    nrm = jax.random.normal
    f32 = jnp.float32
    d = {}
    d['x_prompt'] = nrm(ks[0], (BATCH, SEQ, D_MODEL), f32)
    d['x_sample'] = nrm(ks[1], (DEC_BATCH, DEC_SEQ, D_MODEL), f32)
    d['cache_k'] = nrm(ks[2], (DEC_BATCH, PAST_LEN, DA_HEADS, 2 * DA_HEAD_DIM), f32)
    d['cache_v'] = nrm(ks[3], (DEC_BATCH, PAST_LEN, DA_HEADS, 2 * DA_HEAD_DIM), f32)
    d['state_hgrn'] = 0.5 * nrm(ks[4], (N_A, DEC_BATCH, HG_HEADS, HG_DK, HG_DV), f32)
    d['state_conv'] = nrm(ks[5], (DEPTH, DEC_BATCH, CONV_W - 1, D_FF), f32)
    d['p_prompt'] = nrm(ks[6], (DEPTH, BATCH, SEQ, PLE_DIM), f32)
    d['p_sample'] = nrm(ks[7], (DEPTH, DEC_BATCH, DEC_SEQ, PLE_DIM), f32)
    sd = D_MODEL ** -0.5
    d['g_norms'] = 1.0 + 0.02 * nrm(ks[8], (DEPTH, 4, D_MODEL), f32)
    d['w_in_a'] = sd * nrm(ks[9], (N_A, D_MODEL, 4 * D_MODEL), f32)
    d['lb_raw'] = 0.5 * nrm(ks[10], (N_A, D_MODEL), f32)
    d['g_hg'] = 1.0 + 0.02 * nrm(ks[11], (N_A, D_MODEL), f32)
    d['w_out_a'] = sd * nrm(ks[12], (N_A, D_MODEL, D_MODEL), f32)
    d['g_kv'] = 1.0 + 0.02 * nrm(ks[13], (D_MODEL,), f32)
    d['w_kv'] = sd * nrm(ks[14], (D_MODEL, 2 * D_MODEL), f32)
    d['rel_bias'] = 0.2 * nrm(ks[15], (NUM_BUCKETS, DA_HEADS), f32)
    d['w_q_b'] = sd * nrm(ks[16], (N_B, D_MODEL, D_MODEL), f32)
    d['lam_b'] = 0.1 * nrm(ks[17], (N_B, 4, DA_HEAD_DIM), f32)
    d['g_subln'] = 1.0 + 0.02 * nrm(ks[18], (N_B, D_MODEL), f32)
    d['w_out_b'] = sd * nrm(ks[19], (N_B, D_MODEL, D_MODEL), f32)
    d['w_ffn_in'] = sd * nrm(ks[20], (DEPTH, D_MODEL, 2 * D_FF), f32)
    d['conv_w'] = (CONV_W ** -0.5) * nrm(ks[21], (DEPTH, CONV_W, D_FF), f32)
    d['conv_b'] = 0.01 * nrm(ks[22], (DEPTH, D_FF), f32)
    d['w_ffn_out'] = (D_FF ** -0.5) * nrm(ks[23], (DEPTH, D_FF, D_MODEL), f32)
    d['w_ple'] = (PLE_DIM ** -0.5) * nrm(ks[24], (DEPTH, PLE_DIM, D_MODEL), f32)
    d['w_ple_gate'] = sd * nrm(ks[25], (DEPTH, D_MODEL, D_MODEL), f32)
    return d


def reference(x_prompt, x_sample, cache_k, cache_v, state_hgrn, state_conv, p_prompt, p_sample,
              g_norms, w_in_a, lb_raw, g_hg, w_out_a, g_kv, w_kv, rel_bias,
              w_q_b, lam_b, g_subln, w_out_b, w_ffn_in, conv_w, conv_b, w_ffn_out,
              w_ple, w_ple_gate):
    weights = (g_norms, w_in_a, lb_raw, g_hg, w_out_a, g_kv, w_kv, rel_bias,
               w_q_b, lam_b, g_subln, w_out_b, w_ffn_in, conv_w, conv_b, w_ffn_out,
               w_ple, w_ple_gate)
    Bp, Tp, _ = x_prompt.shape
    Bs, Ts, _ = x_sample.shape
    pos_p = jnp.arange(Tp, dtype=jnp.int32)
    hg0 = jnp.zeros((N_A, Bp, HG_HEADS, HG_DK, HG_DV), x_prompt.dtype)
    cv0 = jnp.zeros((DEPTH, Bp, CONV_W - 1, D_FF), x_prompt.dtype)
    y_prompt, k_prompt, v_prompt, hgrn_prompt, conv_prompt = trunk(
        x_prompt, p_prompt, pos_p, None, None, hg0, cv0, True, *weights)
    pos_s = cache_k.shape[1] + jnp.arange(Ts, dtype=jnp.int32)
    y_sample, k_sample, v_sample, hgrn_sample, conv_sample = trunk(
        x_sample, p_sample, pos_s, cache_k, cache_v, state_hgrn, state_conv, False, *weights)
    return (y_prompt, y_sample, k_prompt, v_prompt, k_sample, v_sample,
            hgrn_prompt, hgrn_sample, conv_prompt, conv_sample)
```

```python
import functools
import math

import jax
import jax.numpy as jnp
from jax import lax
from jax.experimental import pallas as pl
from jax.experimental.pallas import tpu as pltpu

F32 = jnp.float32
BF16 = jnp.bfloat16

D_MODEL = 1024
N_HEADS = 8
HEAD_W = D_MODEL // N_HEADS
DA_HEAD_DIM = HEAD_W // 2
CHUNK = 64
GLA_BLOCK = 16
NUM_BUCKETS = 32
MAX_DISTANCE = 256
D_FF = 2816
CONV_W = 3
EPS = 1e-6
NEG_BIG = -1e30

V7X_VMEM_LIMIT_BYTES = 56 * 1024 * 1024

ROW_TILE = 512
FFN_ROW_TILE = 1024
FFN_COL_TILE = 256
GLA_TILE = 256
ATTN_TILE = 256


def _cparams(*sem):
    return pltpu.CompilerParams(dimension_semantics=sem,
                                vmem_limit_bytes=V7X_VMEM_LIMIT_BYTES)


def _rms(x, g):
    ms = jnp.mean(x * x, axis=-1, keepdims=True)
    return x * lax.rsqrt(ms + EPS) * g


def _sigmoid(z):
    return 1.0 / (1.0 + jnp.exp(-z))


def _row_tile(m, cap):
    t = min(m, cap)
    assert m % t == 0, (m, t)
    return t


def _norm_matmul_kernel(x_ref, g_ref, w_ref, o_ref, h_ref):
    @pl.when(pl.program_id(1) == 0)
    def _():
        h_ref[...] = _rms(x_ref[...], g_ref[...]).astype(BF16)

    o_ref[...] = jnp.dot(h_ref[...], w_ref[...],
                         preferred_element_type=F32).astype(o_ref.dtype)


def _norm_matmul(x2d, g, w, out_dtype, name):
    m, d = x2d.shape
    n = w.shape[1]
    tm = _row_tile(m, ROW_TILE)
    tn = _row_tile(n, 1024)
    return pl.pallas_call(
        _norm_matmul_kernel,
        out_shape=jax.ShapeDtypeStruct((m, n), out_dtype),
        grid=(m // tm, n // tn),
        in_specs=[pl.BlockSpec((tm, d), lambda i, j: (i, 0)),
                  pl.BlockSpec((1, d), lambda i, j: (0, 0)),
                  pl.BlockSpec((d, tn), lambda i, j: (0, j))],
        out_specs=pl.BlockSpec((tm, tn), lambda i, j: (i, j)),
        scratch_shapes=[pltpu.VMEM((tm, d), BF16)],
        compiler_params=_cparams("parallel", "arbitrary"),
        name=name,
    )(x2d, g.reshape(1, d), w)


def _norm_kv_kernel(x_ref, g_ref, w_ref, k_ref, v_ref):
    d = k_ref.shape[1]
    h = _rms(x_ref[...], g_ref[...]).astype(BF16)
    k_ref[...] = jnp.dot(h, w_ref[:, :d], preferred_element_type=F32)
    v_ref[...] = jnp.dot(h, w_ref[:, d:], preferred_element_type=F32)


def _norm_kv(x2d, g, w, name):
    m, d = x2d.shape
    tm = _row_tile(m, ROW_TILE)
    out = jax.ShapeDtypeStruct((m, d), F32)
    return pl.pallas_call(
        _norm_kv_kernel,
        out_shape=(out, out),
        grid=(m // tm,),
        in_specs=[pl.BlockSpec((tm, d), lambda i: (i, 0)),
                  pl.BlockSpec((1, d), lambda i: (0, 0)),
                  pl.BlockSpec((d, 2 * d), lambda i: (0, 0))],
        out_specs=(pl.BlockSpec((tm, d), lambda i: (i, 0)),
                   pl.BlockSpec((tm, d), lambda i: (i, 0))),
        compiler_params=_cparams("parallel"),
        name=name,
    )(x2d, g.reshape(1, d), w)


def _matmul_norm_res_kernel(a_ref, w_ref, g_ref, x_ref, o_ref):
    m = jnp.dot(a_ref[...], w_ref[...], preferred_element_type=F32)
    o_ref[...] = x_ref[...] + _rms(m, g_ref[...])


def _matmul_norm_res(a2d, w, g, x2d, name):
    m, k = a2d.shape
    d = w.shape[1]
    tm = _row_tile(m, ROW_TILE)
    return pl.pallas_call(
        _matmul_norm_res_kernel,
        out_shape=jax.ShapeDtypeStruct((m, d), F32),
        grid=(m // tm,),
        in_specs=[pl.BlockSpec((tm, k), lambda i: (i, 0)),
                  pl.BlockSpec((k, d), lambda i: (0, 0)),
                  pl.BlockSpec((1, d), lambda i: (0, 0)),
                  pl.BlockSpec((tm, d), lambda i: (i, 0))],
        out_specs=pl.BlockSpec((tm, d), lambda i: (i, 0)),
        compiler_params=_cparams("parallel"),
        name=name,
    )(a2d, w, g.reshape(1, d), x2d)


def _ple_kernel(x_ref, p_ref, wg_ref, wp_ref, o_ref):
    x = x_ref[...]
    z = jnp.dot(x.astype(BF16), wg_ref[...], preferred_element_type=F32)
    e = jnp.dot(p_ref[...].astype(BF16), wp_ref[...], preferred_element_type=F32)
    o_ref[...] = x + _sigmoid(z) * e


def _ple(x2d, p2d, wg, wp, name):
    m, d = x2d.shape
    pd = p2d.shape[1]
    tm = _row_tile(m, ROW_TILE)
    return pl.pallas_call(
        _ple_kernel,
        out_shape=jax.ShapeDtypeStruct((m, d), F32),
        grid=(m // tm,),
        in_specs=[pl.BlockSpec((tm, d), lambda i: (i, 0)),
                  pl.BlockSpec((tm, pd), lambda i: (i, 0)),
                  pl.BlockSpec((d, d), lambda i: (0, 0)),
                  pl.BlockSpec((pd, d), lambda i: (0, 0))],
        out_specs=pl.BlockSpec((tm, d), lambda i: (i, 0)),
        compiler_params=_cparams("parallel"),
        name=name,
    )(x2d, p2d, wg, wp)


def _ffn_kernel(x_ref, gin_ref, wg_ref, wu_ref, cw_ref, cb_ref, wo_ref, gout_ref,
                cs_ref, o_ref, co_ref, h_ref, acc_ref):
    t = pl.program_id(1)
    j = pl.program_id(2)
    tm = x_ref.shape[1]

    @pl.when(j == 0)
    def _():
        h_ref[...] = _rms(x_ref[0], gin_ref[...]).astype(BF16)
        acc_ref[...] = jnp.zeros_like(acc_ref)

    @pl.when(t == 0)
    def _():
        co_ref[0, j] = cs_ref[0, j]

    h = h_ref[...]
    g = jnp.dot(h, wg_ref[...], preferred_element_type=F32)
    u = jnp.dot(h, wu_ref[...], preferred_element_type=F32)
    prev = co_ref[0, j]
    co_ref[0, j] = g[tm - (CONV_W - 1):, :]
    row = lax.broadcasted_iota(jnp.int32, g.shape, 0)
    g1 = jnp.where(row == 0, prev[1:2, :], pltpu.roll(g, 1, axis=0))
    g2 = jnp.where(row == 0, prev[0:1, :],
                   jnp.where(row == 1, prev[1:2, :], pltpu.roll(g, 2, axis=0)))
    cw = cw_ref[...]
    c = cb_ref[...] + cw[0:1, :] * g2 + cw[1:2, :] * g1 + cw[2:3, :] * g
    act = (c * _sigmoid(c) * u).astype(BF16)
    acc_ref[...] += jnp.dot(act, wo_ref[...], preferred_element_type=F32)

    @pl.when(j == pl.num_programs(2) - 1)
    def _():
        o_ref[0] = x_ref[0] + _rms(acc_ref[...], gout_ref[...])


def _ffn(x, gin, w_in, cw, cb, w_out, gout, conv_state, name):
    b, t, d = x.shape
    f = w_out.shape[0]
    tm = _row_tile(t, FFN_ROW_TILE)
    tf = FFN_COL_TILE
    nf = f // tf
    assert f % tf == 0 and t >= CONV_W - 1
    cs = conv_state.reshape(b, CONV_W - 1, nf, tf).transpose(0, 2, 1, 3)
    cs_spec = pl.BlockSpec((1, nf, CONV_W - 1, tf), lambda bi, ti, j: (bi, 0, 0, 0))
    y, co = pl.pallas_call(
        _ffn_kernel,
        out_shape=(jax.ShapeDtypeStruct((b, t, d), F32),
                   jax.ShapeDtypeStruct(cs.shape, F32)),
        grid=(b, t // tm, nf),
        in_specs=[pl.BlockSpec((1, tm, d), lambda bi, ti, j: (bi, ti, 0)),
                  pl.BlockSpec((1, d), lambda bi, ti, j: (0, 0)),
                  pl.BlockSpec((d, tf), lambda bi, ti, j: (0, j)),
                  pl.BlockSpec((d, tf), lambda bi, ti, j: (0, j + nf)),
                  pl.BlockSpec((CONV_W, tf), lambda bi, ti, j: (0, j)),
                  pl.BlockSpec((1, tf), lambda bi, ti, j: (0, j)),
                  pl.BlockSpec((tf, d), lambda bi, ti, j: (j, 0)),
                  pl.BlockSpec((1, d), lambda bi, ti, j: (0, 0)),
                  cs_spec],
        out_specs=(pl.BlockSpec((1, tm, d), lambda bi, ti, j: (bi, ti, 0)), cs_spec),
        scratch_shapes=[pltpu.VMEM((tm, d), BF16),
                        pltpu.VMEM((tm, d), F32)],
        compiler_params=_cparams("parallel", "arbitrary", "arbitrary"),
        name=name,
    )(x, gin.reshape(1, d), w_in, w_in, cw, cb.reshape(1, f), w_out,
      gout.reshape(1, d), cs)
    return y, co.transpose(0, 2, 1, 3).reshape(b, CONV_W - 1, f)


def _split3(x):
    hi = x.astype(BF16)
    r1 = x - hi.astype(F32)
    mid = r1.astype(BF16)
    lo = (r1 - mid.astype(F32)).astype(BF16)
    return hi, mid, lo


def _gla_kernel(q_ref, f_ref, i_ref, g_ref, ll_ref, l1m_ref, om_ref, gn_ref, s0_ref,
                o_ref, so_ref, st_ref, qt_ref, kt_ref, kd_ref, v_ref, dec_ref, oo_ref):
    t = pl.program_id(1)
    tc = q_ref.shape[1]
    nblk = tc // GLA_BLOCK

    @pl.when(t == 0)
    def _():
        for h in range(N_HEADS):
            st_ref[h] = s0_ref[0, h].T

    qh = q_ref[0]
    fp = f_ref[0]
    q = qh * _sigmoid(qh)
    e = jnp.exp(-jnp.abs(fp))
    logsig = jnp.minimum(fp, 0.0) - jnp.log1p(e)
    a = ll_ref[...]
    c = l1m_ref[...] + logsig
    logf = jnp.maximum(a, c) + jnp.log1p(jnp.exp(-jnp.abs(a - c)))
    kk = om_ref[...] * (jnp.where(fp >= 0.0, e, 1.0) / (1.0 + e))

    row = lax.broadcasted_iota(jnp.int32, (tc, tc), 0)
    col = lax.broadcasted_iota(jnp.int32, (tc, tc), 1)
    same = (row // GLA_BLOCK) == (col // GLA_BLOCK)
    m_incl = jnp.where(same & (col <= row), 1.0, 0.0).astype(BF16)
    m_rest = jnp.where(same & (col > row), 1.0, 0.0).astype(BF16)
    parts = _split3(logf)
    b = sum(jnp.dot(m_incl, p, preferred_element_type=F32) for p in parts)
    r = sum(jnp.dot(m_rest, p, preferred_element_type=F32) for p in parts)
    brow = lax.broadcasted_iota(jnp.int32, (nblk * 8, tc), 0)
    bcol = lax.broadcasted_iota(jnp.int32, (nblk * 8, tc), 1)
    m_tot = jnp.where((brow // 8) == (bcol // GLA_BLOCK), 1.0, 0.0).astype(BF16)
    tot = sum(jnp.dot(m_tot, p, preferred_element_type=F32) for p in parts)
    dec_ref[...] = jnp.exp(tot).reshape(nblk, 8, tot.shape[-1])

    qt_ref[...] = (q * jnp.exp(b)).astype(BF16)
    kt_ref[...] = (kk * jnp.exp(-b)).astype(BF16)
    kd_ref[...] = (kk * jnp.exp(r)).astype(BF16)
    v_ref[...] = i_ref[0].astype(BF16)

    lrow = lax.broadcasted_iota(jnp.int32, (GLA_BLOCK, GLA_BLOCK), 0)
    lcol = lax.broadcasted_iota(jnp.int32, (GLA_BLOCK, GLA_BLOCK), 1)
    causal = lcol <= lrow
    nt = (((1,), (1,)), ((), ()))
    tn = (((0,), (0,)), ((), ()))

    def block(jb, carry):
        r0 = pl.multiple_of(jb * GLA_BLOCK, GLA_BLOCK)
        rows = pl.ds(r0, GLA_BLOCK)
        for h in range(N_HEADS):
            cols = slice(h * HEAD_W, (h + 1) * HEAD_W)
            qt = qt_ref[rows, cols]
            kt = kt_ref[rows, cols]
            kd = kd_ref[rows, cols]
            vb = v_ref[rows, cols]
            s_t = st_ref[h]
            o_inter = lax.dot_general(qt, s_t.astype(BF16), nt, preferred_element_type=F32)
            att = lax.dot_general(qt, kt, nt, preferred_element_type=F32)
            att = jnp.where(causal, att, 0.0).astype(BF16)
            o_intra = jnp.dot(att, vb, preferred_element_type=F32)
            oo_ref[rows, cols] = o_inter + o_intra
            upd = lax.dot_general(vb, kd, tn, preferred_element_type=F32)
            decay = dec_ref[jb, :, cols]
            st_ref[h] = s_t * decay[0:1, :] + upd
        return carry

    lax.fori_loop(0, nblk, block, 0)

    gh = g_ref[0]
    gate = gh * _sigmoid(gh)
    for h in range(N_HEADS):
        cols = slice(h * HEAD_W, (h + 1) * HEAD_W)
        oh = _rms(oo_ref[:, cols], gn_ref[:, cols])
        o_ref[0, :, cols] = (oh * gate[:, cols]).astype(o_ref.dtype)

    @pl.when(t == pl.num_programs(1) - 1)
    def _():
        for h in range(N_HEADS):
            so_ref[0, h] = st_ref[h].T


def _gla(proj, log_lower, log1m_lower, one_m_lower, g_out, s0, name):
    b, t, _ = proj.shape
    d = D_MODEL
    tc = _row_tile(t, GLA_TILE)
    assert tc % GLA_BLOCK == 0
    vec = pl.BlockSpec((1, d), lambda bi, ti: (0, 0))
    col_spec = lambda c: pl.BlockSpec((1, tc, d), lambda bi, ti: (bi, ti, c))
    st_spec = pl.BlockSpec((1, N_HEADS, HEAD_W, HEAD_W), lambda bi, ti: (bi, 0, 0, 0))
    return pl.pallas_call(
        _gla_kernel,
        out_shape=(jax.ShapeDtypeStruct((b, t, d), BF16),
                   jax.ShapeDtypeStruct((b, N_HEADS, HEAD_W, HEAD_W), F32)),
        grid=(b, t // tc),
        in_specs=[col_spec(0), col_spec(1), col_spec(2), col_spec(3),
                  vec, vec, vec, vec, st_spec],
        out_specs=(pl.BlockSpec((1, tc, d), lambda bi, ti: (bi, ti, 0)), st_spec),
        scratch_shapes=[pltpu.VMEM((N_HEADS, HEAD_W, HEAD_W), F32),
                        pltpu.VMEM((tc, d), BF16), pltpu.VMEM((tc, d), BF16),
                        pltpu.VMEM((tc, d), BF16), pltpu.VMEM((tc, d), BF16),
                        pltpu.VMEM((tc // GLA_BLOCK, 8, d), F32), pltpu.VMEM((tc, d), F32)],
        compiler_params=_cparams("parallel", "arbitrary"),
        name=name,
    )(proj, proj, proj, proj, log_lower.reshape(1, d), log1m_lower.reshape(1, d),
      one_m_lower.reshape(1, d), g_out.reshape(1, d), s0)


def _t5_bucket(rel):
    half = NUM_BUCKETS // 2
    ret = jnp.where(rel > 0, half, 0)
    n = jnp.abs(rel)
    max_exact = half // 2
    nf = jnp.maximum(n, 1).astype(F32)
    large = max_exact + (jnp.log(nf / max_exact) / math.log(MAX_DISTANCE / max_exact)
                         * (half - max_exact)).astype(jnp.int32)
    large = jnp.minimum(large, half - 1)
    return ret + jnp.where(n < max_exact, n, large)


def _bias_tiles(rel_bias, d0_list, tq, tk):
    d0 = jnp.asarray(d0_list, jnp.int32)[:, None, None]
    rel = d0 + jnp.arange(tk, dtype=jnp.int32)[None, None, :] - jnp.arange(tq, dtype=jnp.int32)[None, :, None]
    return jnp.transpose(rel_bias[_t5_bucket(rel)], (3, 0, 1, 2)).astype(F32)


def _attn_kernel(*refs, segs, q_off, tq):
    nseg = len(segs)
    q_ref, lam_ref, gn_ref = refs[0], refs[1], refs[2]
    seg_refs = [refs[3 + 3 * s: 6 + 3 * s] for s in range(nseg)]
    o_ref = refs[3 + 3 * nseg]
    m_ref, l_ref, acc_ref = refs[4 + 3 * nseg:]

    i = pl.program_id(2)
    q0 = q_off + i * tq
    q = q_ref[0]
    qs = [(q[:, c * DA_HEAD_DIM:(c + 1) * DA_HEAD_DIM] * (DA_HEAD_DIM ** -0.5)).astype(BF16)
          for c in range(2)]
    m_ref[...] = jnp.full_like(m_ref, NEG_BIG)
    l_ref[...] = jnp.zeros_like(l_ref)
    acc_ref[...] = jnp.zeros_like(acc_ref)
    q_chunk = (q0 + lax.broadcasted_iota(jnp.int32, (tq, 1), 0)) // CHUNK
    vis_end = ((q0 + tq - 1) // CHUNK + 1) * CHUNK
    nt = (((1,), (1,)), ((), ()))

    for (k_ref, v_ref, b_ref), (pos0, seg_len, tk, d0_min, d0_step) in zip(seg_refs, segs):
        n_tiles = seg_len // tk
        n_vis = jnp.clip((vis_end - pos0 + tk - 1) // tk, 0, n_tiles)

        def tile(kt, carry, k_ref=k_ref, v_ref=v_ref, b_ref=b_ref, pos0=pos0, tk=tk,
                 d0_min=d0_min, d0_step=d0_step):
            r0 = pl.multiple_of(kt * tk, tk)
            kb = k_ref[0, pl.ds(r0, tk), :].astype(BF16)
            vb = v_ref[0, pl.ds(r0, tk), :].astype(BF16)
            bias = b_ref[0, (pos0 + r0 - q0 - d0_min) // d0_step]
            k_chunk = (pos0 + r0 + lax.broadcasted_iota(jnp.int32, (1, tk), 1)) // CHUNK
            visible = k_chunk <= q_chunk
            for c in range(2):
                kc = kb[:, c * DA_HEAD_DIM:(c + 1) * DA_HEAD_DIM]
                s = lax.dot_general(qs[c], kc, nt, preferred_element_type=F32) + bias
                s = jnp.where(visible, s, NEG_BIG)
                m_prev = m_ref[c]
                m_new = jnp.maximum(m_prev, jnp.max(s, axis=-1, keepdims=True))
                p = jnp.exp(s - m_new)
                alpha = jnp.exp(m_prev - m_new)
                l_ref[c] = alpha * l_ref[c] + jnp.sum(p, axis=-1, keepdims=True)
                acc_ref[c] = alpha * acc_ref[c] + jnp.dot(p.astype(BF16), vb,
                                                          preferred_element_type=F32)
                m_ref[c] = m_new
            return carry

        lax.fori_loop(0, n_vis, tile, 0)

    lam = lam_ref[0:1, 0:1]
    o = acc_ref[0] / l_ref[0] - lam * (acc_ref[1] / l_ref[1])
    o_ref[0] = (_rms(o, gn_ref[...]) * lam_ref[0:1, 1:2]).astype(o_ref.dtype)


def _attention(q, segments, rel_bias, lam, out_scale, g_subln, q_off, name):
    b, tq_all, d = q.shape
    tq = _row_tile(tq_all, ATTN_TILE)
    nq = tq_all // tq
    segs, args, in_specs = [], [], []
    for (k, v, pos0, tk) in segments:
        seg_len = k.shape[1]
        assert seg_len % tk == 0
        d0s = sorted({pos0 + kt * tk - (q_off + i * tq)
                      for i in range(nq) for kt in range(seg_len // tk)
                      if pos0 + kt * tk < ((q_off + (i + 1) * tq - 1) // CHUNK + 1) * CHUNK})
        step = math.gcd(tq, tk)
        d0s = list(range(d0s[0], d0s[-1] + 1, step))
        tiles = _bias_tiles(rel_bias, d0s, tq, tk)
        segs.append((pos0, seg_len, tk, d0s[0], step))
        args += [k, v, tiles]
        in_specs += [pl.BlockSpec((1, seg_len, HEAD_W), lambda bi, h, i: (bi, 0, h)),
                     pl.BlockSpec((1, seg_len, HEAD_W), lambda bi, h, i: (bi, 0, h)),
                     pl.BlockSpec((1, len(d0s), tq, tk), lambda bi, h, i: (h, 0, 0, 0))]
    scal = jnp.zeros((1, HEAD_W), F32).at[0, 0].set(lam).at[0, 1].set(out_scale)
    kern = functools.partial(_attn_kernel, segs=tuple(segs), q_off=q_off, tq=tq)
    return pl.pallas_call(
        kern,
        out_shape=jax.ShapeDtypeStruct((b, tq_all, d), BF16),
        grid=(b, N_HEADS, nq),
        in_specs=[pl.BlockSpec((1, tq, HEAD_W), lambda bi, h, i: (bi, i, h)),
                  pl.BlockSpec((1, HEAD_W), lambda bi, h, i: (0, 0)),
                  pl.BlockSpec((1, HEAD_W), lambda bi, h, i: (0, h))] + in_specs,
        out_specs=pl.BlockSpec((1, tq, HEAD_W), lambda bi, h, i: (bi, i, h)),
        scratch_shapes=[pltpu.VMEM((2, tq, 1), F32), pltpu.VMEM((2, tq, 1), F32),
                        pltpu.VMEM((2, tq, HEAD_W), F32)],
        compiler_params=_cparams("parallel", "parallel", "arbitrary"),
        name=name,
    )(q, scal, g_subln.reshape(1, d), *args)


def _trunk(x, ple, q_off, past_k, past_v, hg_s0, conv_s0, wts, tag):
    (g_norms, w_in_a, lower, g_hg, w_out_a, g_kv, w_kv, rel_bias, w_q_b, lam_b,
     g_subln, w_out_b, w_ffn_in, conv_w, conv_b, w_ffn_out, w_ple, w_ple_gate) = wts
    b, t, d = x.shape
    depth = g_norms.shape[0]
    n_a = w_in_a.shape[0]
    m = b * t
    hg_out, conv_out = [], []
    k_new = v_new = None
    segments = None
    for i in range(depth):
        nm = f"{tag}{i}"
        x2 = x.reshape(m, d)
        if i < n_a:
            lw = lower[i]
            proj = _norm_matmul(x2, g_norms[i, 0], w_in_a[i], F32, nm + "_hg_in")
            o, s = _gla(proj.reshape(b, t, 4 * d), jnp.log(lw), jnp.log1p(-lw), 1.0 - lw,
                        g_hg[i], hg_s0[i], nm + "_gla")
            hg_out.append(s)
            x2 = _matmul_norm_res(o.reshape(m, d), w_out_a[i], g_norms[i, 1], x2, nm + "_hg_out")
        else:
            j = i - n_a
            q = _norm_matmul(x2, g_norms[i, 0], w_q_b[j], BF16, nm + "_q")
            lam_init = 0.8 - 0.6 * math.exp(-0.3 * i)
            lp = lam_b[j].astype(F32)
            lam = jnp.exp(jnp.sum(lp[0] * lp[1])) - jnp.exp(jnp.sum(lp[2] * lp[3])) + lam_init
            o = _attention(q.reshape(b, t, d), segments, rel_bias, lam, 1.0 - lam_init,
                           g_subln[j], q_off, nm + "_attn")
            x2 = _matmul_norm_res(o.reshape(m, d), w_out_b[j], g_norms[i, 1], x2, nm + "_attn_out")
        x3, cbuf = _ffn(x2.reshape(b, t, d), g_norms[i, 2], w_ffn_in[i], conv_w[i], conv_b[i],
                        w_ffn_out[i], g_norms[i, 3], conv_s0[i], nm + "_ffn")
        conv_out.append(cbuf)
        x2 = _ple(x3.reshape(m, d), ple[i].reshape(m, -1), w_ple_gate[i], w_ple[i], nm + "_ple")
        x = x2.reshape(b, t, d)
        if i == n_a - 1:
            k_new, v_new = _norm_kv(x2, g_kv, w_kv, nm + "_kv")
            k_new = k_new.reshape(b, t, d)
            v_new = v_new.reshape(b, t, d)
            segments = []
            if past_k is not None:
                tp = past_k.shape[1]
                segments.append((past_k.reshape(b, tp, d), past_v.reshape(b, tp, d), 0, tp))
            segments.append((k_new, v_new, q_off, min(t, ATTN_TILE)))
    hd = (b, t, N_HEADS, HEAD_W)
    return x, k_new.reshape(hd), v_new.reshape(hd), jnp.stack(hg_out), jnp.stack(conv_out)


def kernel(x_prompt, x_sample, cache_k, cache_v, state_hgrn, state_conv, p_prompt, p_sample,
           g_norms, w_in_a, lb_raw, g_hg, w_out_a, g_kv, w_kv, rel_bias,
           w_q_b, lam_b, g_subln, w_out_b, w_ffn_in, conv_w, conv_b, w_ffn_out,
           w_ple, w_ple_gate):
    sm = jax.nn.softmax(lb_raw.astype(F32), axis=0)
    cs = jnp.cumsum(sm, axis=0)
    lower = cs - cs[0:1]
    bf = lambda w: w.astype(BF16)
    wts = (g_norms, bf(w_in_a), lower, g_hg, bf(w_out_a), g_kv, bf(w_kv), rel_bias,
           bf(w_q_b), lam_b, g_subln, bf(w_out_b), bf(w_ffn_in), conv_w, conv_b,
           bf(w_ffn_out), bf(w_ple), bf(w_ple_gate))
    bp, tp, _ = x_prompt.shape
    n_a, depth = w_in_a.shape[0], g_norms.shape[0]
    hg0 = jnp.zeros((n_a, bp) + state_hgrn.shape[2:], F32)
    cv0 = jnp.zeros((depth, bp) + state_conv.shape[2:], F32)
    y_p, k_p, v_p, hg_p, cv_p = _trunk(x_prompt, p_prompt, 0, None, None, hg0, cv0, wts, "p")
    y_s, k_s, v_s, hg_s, cv_s = _trunk(x_sample, p_sample, cache_k.shape[1], cache_k, cache_v,
                                       state_hgrn, state_conv, wts, "s")
    return (y_p, y_s, k_p, v_p, k_s, v_s, hg_p, hg_s, cv_p, cv_s)
```

```python
import functools
import math

import jax
import jax.numpy as jnp
from jax import lax
from jax.experimental import pallas as pl
from jax.experimental.pallas import tpu as pltpu

F32 = jnp.float32
BF16 = jnp.bfloat16

D_MODEL = 1024
N_HEADS = 8
HEAD_W = D_MODEL // N_HEADS
DA_HEAD_DIM = HEAD_W // 2
CHUNK = 64
GLA_BLOCK = 16
NUM_BUCKETS = 32
MAX_DISTANCE = 256
D_FF = 2816
CONV_W = 3
EPS = 1e-6
NEG_BIG = -1e30

V7X_VMEM_LIMIT_BYTES = 56 * 1024 * 1024

ROW_TILE = 512
FFN_ROW_TILE = 1024
FFN_COL_TILE = 256
GLA_TILE = 256
ATTN_TILE = 256
ATTN_HEADS_PER_STEP = 2


def _cparams(*sem):
    return pltpu.CompilerParams(dimension_semantics=sem,
                                vmem_limit_bytes=V7X_VMEM_LIMIT_BYTES)


def _rms(x, g):
    ms = jnp.mean(x * x, axis=-1, keepdims=True)
    return x * lax.rsqrt(ms + EPS) * g


def _sigmoid(z):
    return 1.0 / (1.0 + jnp.exp(-z))


def _row_tile(m, cap):
    t = min(m, cap)
    assert m % t == 0, (m, t)
    return t


def _norm_matmul_kernel(x_ref, g_ref, w_ref, o_ref, h_ref):
    @pl.when(pl.program_id(1) == 0)
    def _():
        h_ref[...] = _rms(x_ref[...], g_ref[...]).astype(BF16)

    o_ref[...] = jnp.dot(h_ref[...], w_ref[...],
                         preferred_element_type=F32).astype(o_ref.dtype)


def _norm_matmul(x2d, g, w, out_dtype, name):
    m, d = x2d.shape
    n = w.shape[1]
    tm = _row_tile(m, ROW_TILE)
    tn = _row_tile(n, 1024)
    return pl.pallas_call(
        _norm_matmul_kernel,
        out_shape=jax.ShapeDtypeStruct((m, n), out_dtype),
        grid=(m // tm, n // tn),
        in_specs=[pl.BlockSpec((tm, d), lambda i, j: (i, 0)),
                  pl.BlockSpec((1, d), lambda i, j: (0, 0)),
                  pl.BlockSpec((d, tn), lambda i, j: (0, j))],
        out_specs=pl.BlockSpec((tm, tn), lambda i, j: (i, j)),
        scratch_shapes=[pltpu.VMEM((tm, d), BF16)],
        compiler_params=_cparams("parallel", "arbitrary"),
        name=name,
    )(x2d, g.reshape(1, d), w)


def _norm_kv_kernel(x_ref, g_ref, w_ref, k_ref, v_ref):
    d = k_ref.shape[1]
    h = _rms(x_ref[...], g_ref[...]).astype(BF16)
    k_ref[...] = jnp.dot(h, w_ref[:, :d], preferred_element_type=F32)
    v_ref[...] = jnp.dot(h, w_ref[:, d:], preferred_element_type=F32)


def _norm_kv(x2d, g, w, name):
    m, d = x2d.shape
    tm = _row_tile(m, ROW_TILE)
    out = jax.ShapeDtypeStruct((m, d), F32)
    return pl.pallas_call(
        _norm_kv_kernel,
        out_shape=(out, out),
        grid=(m // tm,),
        in_specs=[pl.BlockSpec((tm, d), lambda i: (i, 0)),
                  pl.BlockSpec((1, d), lambda i: (0, 0)),
                  pl.BlockSpec((d, 2 * d), lambda i: (0, 0))],
        out_specs=(pl.BlockSpec((tm, d), lambda i: (i, 0)),
                   pl.BlockSpec((tm, d), lambda i: (i, 0))),
        compiler_params=_cparams("parallel"),
        name=name,
    )(x2d, g.reshape(1, d), w)


def _matmul_norm_res_kernel(a_ref, w_ref, g_ref, x_ref, o_ref):
    m = jnp.dot(a_ref[...], w_ref[...], preferred_element_type=F32)
    o_ref[...] = x_ref[...] + _rms(m, g_ref[...])


def _matmul_norm_res(a2d, w, g, x2d, name):
    m, k = a2d.shape
    d = w.shape[1]
    tm = _row_tile(m, ROW_TILE)
    return pl.pallas_call(
        _matmul_norm_res_kernel,
        out_shape=jax.ShapeDtypeStruct((m, d), F32),
        grid=(m // tm,),
        in_specs=[pl.BlockSpec((tm, k), lambda i: (i, 0)),
                  pl.BlockSpec((k, d), lambda i: (0, 0)),
                  pl.BlockSpec((1, d), lambda i: (0, 0)),
                  pl.BlockSpec((tm, d), lambda i: (i, 0))],
        out_specs=pl.BlockSpec((tm, d), lambda i: (i, 0)),
        compiler_params=_cparams("parallel"),
        name=name,
    )(a2d, w, g.reshape(1, d), x2d)


def _ple_kernel(x_ref, p_ref, wg_ref, wp_ref, o_ref):
    x = x_ref[...]
    z = jnp.dot(x.astype(BF16), wg_ref[...], preferred_element_type=F32)
    e = jnp.dot(p_ref[...].astype(BF16), wp_ref[...], preferred_element_type=F32)
    o_ref[...] = x + _sigmoid(z) * e


def _ple(x2d, p2d, wg, wp, name):
    m, d = x2d.shape
    pd = p2d.shape[1]
    tm = _row_tile(m, ROW_TILE)
    return pl.pallas_call(
        _ple_kernel,
        out_shape=jax.ShapeDtypeStruct((m, d), F32),
        grid=(m // tm,),
        in_specs=[pl.BlockSpec((tm, d), lambda i: (i, 0)),
                  pl.BlockSpec((tm, pd), lambda i: (i, 0)),
                  pl.BlockSpec((d, d), lambda i: (0, 0)),
                  pl.BlockSpec((pd, d), lambda i: (0, 0))],
        out_specs=pl.BlockSpec((tm, d), lambda i: (i, 0)),
        compiler_params=_cparams("parallel"),
        name=name,
    )(x2d, p2d, wg, wp)


def _ffn_kernel(x_ref, gin_ref, wg_ref, wu_ref, cw_ref, cb_ref, wo_ref, gout_ref,
                cs_ref, o_ref, co_ref, h_ref, acc_ref):
    t = pl.program_id(1)
    j = pl.program_id(2)
    tm = x_ref.shape[1]

    @pl.when(j == 0)
    def _():
        h_ref[...] = _rms(x_ref[0], gin_ref[...]).astype(BF16)
        acc_ref[...] = jnp.zeros_like(acc_ref)

    @pl.when(t == 0)
    def _():
        co_ref[0, j] = cs_ref[0, j]

    h = h_ref[...]
    g = jnp.dot(h, wg_ref[...], preferred_element_type=F32)
    u = jnp.dot(h, wu_ref[...], preferred_element_type=F32)
    prev = co_ref[0, j]
    co_ref[0, j] = g[tm - (CONV_W - 1):, :]
    row = lax.broadcasted_iota(jnp.int32, g.shape, 0)
    g1 = jnp.where(row == 0, prev[1:2, :], pltpu.roll(g, 1, axis=0))
    g2 = jnp.where(row == 0, prev[0:1, :],
                   jnp.where(row == 1, prev[1:2, :], pltpu.roll(g, 2, axis=0)))
    cw = cw_ref[...]
    c = cb_ref[...] + cw[0:1, :] * g2 + cw[1:2, :] * g1 + cw[2:3, :] * g
    act = (c * _sigmoid(c) * u).astype(BF16)
    acc_ref[...] += jnp.dot(act, wo_ref[...], preferred_element_type=F32)

    @pl.when(j == pl.num_programs(2) - 1)
    def _():
        o_ref[0] = x_ref[0] + _rms(acc_ref[...], gout_ref[...])


def _ffn(x, gin, w_in, cw, cb, w_out, gout, conv_state, name):
    b, t, d = x.shape
    f = w_out.shape[0]
    tm = _row_tile(t, FFN_ROW_TILE)
    tf = FFN_COL_TILE
    nf = f // tf
    assert f % tf == 0 and t >= CONV_W - 1
    cs = conv_state.reshape(b, CONV_W - 1, nf, tf).transpose(0, 2, 1, 3)
    cs_spec = pl.BlockSpec((1, nf, CONV_W - 1, tf), lambda bi, ti, j: (bi, 0, 0, 0))
    y, co = pl.pallas_call(
        _ffn_kernel,
        out_shape=(jax.ShapeDtypeStruct((b, t, d), F32),
                   jax.ShapeDtypeStruct(cs.shape, F32)),
        grid=(b, t // tm, nf),
        in_specs=[pl.BlockSpec((1, tm, d), lambda bi, ti, j: (bi, ti, 0)),
                  pl.BlockSpec((1, d), lambda bi, ti, j: (0, 0)),
                  pl.BlockSpec((d, tf), lambda bi, ti, j: (0, j)),
                  pl.BlockSpec((d, tf), lambda bi, ti, j: (0, j + nf)),
                  pl.BlockSpec((CONV_W, tf), lambda bi, ti, j: (0, j)),
                  pl.BlockSpec((1, tf), lambda bi, ti, j: (0, j)),
                  pl.BlockSpec((tf, d), lambda bi, ti, j: (j, 0)),
                  pl.BlockSpec((1, d), lambda bi, ti, j: (0, 0)),
                  cs_spec],
        out_specs=(pl.BlockSpec((1, tm, d), lambda bi, ti, j: (bi, ti, 0)), cs_spec),
        scratch_shapes=[pltpu.VMEM((tm, d), BF16),
                        pltpu.VMEM((tm, d), F32)],
        compiler_params=_cparams("parallel", "arbitrary", "arbitrary"),
        name=name,
    )(x, gin.reshape(1, d), w_in, w_in, cw, cb.reshape(1, f), w_out,
      gout.reshape(1, d), cs)
    return y, co.transpose(0, 2, 1, 3).reshape(b, CONV_W - 1, f)


def _split3(x):
    hi = x.astype(BF16)
    r1 = x - hi.astype(F32)
    mid = r1.astype(BF16)
    lo = (r1 - mid.astype(F32)).astype(BF16)
    return hi, mid, lo


def _gla_kernel(q_ref, f_ref, i_ref, g_ref, ll_ref, l1m_ref, om_ref, gn_ref, s0_ref,
                o_ref, so_ref, st_ref, qt_ref, kt_ref, kd_ref, v_ref, dec_ref, oo_ref):
    t = pl.program_id(1)
    tc = q_ref.shape[1]
    nblk = tc // GLA_BLOCK

    @pl.when(t == 0)
    def _():
        for h in range(N_HEADS):
            st_ref[h] = s0_ref[0, h].T

    qh = q_ref[0]
    fp = f_ref[0]
    q = qh * _sigmoid(qh)
    e = jnp.exp(-jnp.abs(fp))
    logsig = jnp.minimum(fp, 0.0) - jnp.log1p(e)
    a = ll_ref[...]
    c = l1m_ref[...] + logsig
    logf = jnp.maximum(a, c) + jnp.log1p(jnp.exp(-jnp.abs(a - c)))
    kk = om_ref[...] * (jnp.where(fp >= 0.0, e, 1.0) / (1.0 + e))

    row = lax.broadcasted_iota(jnp.int32, (tc, tc), 0)
    col = lax.broadcasted_iota(jnp.int32, (tc, tc), 1)
    same = (row // GLA_BLOCK) == (col // GLA_BLOCK)
    m_incl = jnp.where(same & (col <= row), 1.0, 0.0).astype(BF16)
    m_rest = jnp.where(same & (col > row), 1.0, 0.0).astype(BF16)
    parts = _split3(logf)
    b = sum(jnp.dot(m_incl, p, preferred_element_type=F32) for p in parts)
    r = sum(jnp.dot(m_rest, p, preferred_element_type=F32) for p in parts)
    brow = lax.broadcasted_iota(jnp.int32, (nblk * 8, tc), 0)
    bcol = lax.broadcasted_iota(jnp.int32, (nblk * 8, tc), 1)
    m_tot = jnp.where((brow // 8) == (bcol // GLA_BLOCK), 1.0, 0.0).astype(BF16)
    tot = sum(jnp.dot(m_tot, p, preferred_element_type=F32) for p in parts)
    dec_ref[...] = jnp.exp(tot).reshape(nblk, 8, tot.shape[-1])

    qt_ref[...] = (q * jnp.exp(b)).astype(BF16)
    kt_ref[...] = (kk * jnp.exp(-b)).astype(BF16)
    kd_ref[...] = (kk * jnp.exp(r)).astype(BF16)
    v_ref[...] = i_ref[0].astype(BF16)

    lrow = lax.broadcasted_iota(jnp.int32, (GLA_BLOCK, GLA_BLOCK), 0)
    lcol = lax.broadcasted_iota(jnp.int32, (GLA_BLOCK, GLA_BLOCK), 1)
    causal = lcol <= lrow
    nt = (((1,), (1,)), ((), ()))
    tn = (((0,), (0,)), ((), ()))

    def block(jb, carry):
        r0 = pl.multiple_of(jb * GLA_BLOCK, GLA_BLOCK)
        rows = pl.ds(r0, GLA_BLOCK)
        for h in range(N_HEADS):
            cols = slice(h * HEAD_W, (h + 1) * HEAD_W)
            qt = qt_ref[rows, cols]
            kt = kt_ref[rows, cols]
            kd = kd_ref[rows, cols]
            vb = v_ref[rows, cols]
            s_t = st_ref[h]
            o_inter = lax.dot_general(qt, s_t.astype(BF16), nt, preferred_element_type=F32)
            att = lax.dot_general(qt, kt, nt, preferred_element_type=F32)
            att = jnp.where(causal, att, 0.0).astype(BF16)
            o_intra = jnp.dot(att, vb, preferred_element_type=F32)
            oo_ref[rows, cols] = o_inter + o_intra
            upd = lax.dot_general(vb, kd, tn, preferred_element_type=F32)
            decay = dec_ref[jb, :, cols]
            st_ref[h] = s_t * decay[0:1, :] + upd
        return carry

    lax.fori_loop(0, nblk, block, 0)

    gh = g_ref[0]
    gate = gh * _sigmoid(gh)
    for h in range(N_HEADS):
        cols = slice(h * HEAD_W, (h + 1) * HEAD_W)
        oh = _rms(oo_ref[:, cols], gn_ref[:, cols])
        o_ref[0, :, cols] = (oh * gate[:, cols]).astype(o_ref.dtype)

    @pl.when(t == pl.num_programs(1) - 1)
    def _():
        for h in range(N_HEADS):
            so_ref[0, h] = st_ref[h].T


def _gla(proj, log_lower, log1m_lower, one_m_lower, g_out, s0, name):
    b, t, _ = proj.shape
    d = D_MODEL
    tc = _row_tile(t, GLA_TILE)
    assert tc % GLA_BLOCK == 0
    vec = pl.BlockSpec((1, d), lambda bi, ti: (0, 0))
    col_spec = lambda c: pl.BlockSpec((1, tc, d), lambda bi, ti: (bi, ti, c))
    st_spec = pl.BlockSpec((1, N_HEADS, HEAD_W, HEAD_W), lambda bi, ti: (bi, 0, 0, 0))
    return pl.pallas_call(
        _gla_kernel,
        out_shape=(jax.ShapeDtypeStruct((b, t, d), BF16),
                   jax.ShapeDtypeStruct((b, N_HEADS, HEAD_W, HEAD_W), F32)),
        grid=(b, t // tc),
        in_specs=[col_spec(0), col_spec(1), col_spec(2), col_spec(3),
                  vec, vec, vec, vec, st_spec],
        out_specs=(pl.BlockSpec((1, tc, d), lambda bi, ti: (bi, ti, 0)), st_spec),
        scratch_shapes=[pltpu.VMEM((N_HEADS, HEAD_W, HEAD_W), F32),
                        pltpu.VMEM((tc, d), BF16), pltpu.VMEM((tc, d), BF16),
                        pltpu.VMEM((tc, d), BF16), pltpu.VMEM((tc, d), BF16),
                        pltpu.VMEM((tc // GLA_BLOCK, 8, d), F32), pltpu.VMEM((tc, d), F32)],
        compiler_params=_cparams("parallel", "arbitrary"),
        name=name,
    )(proj, proj, proj, proj, log_lower.reshape(1, d), log1m_lower.reshape(1, d),
      one_m_lower.reshape(1, d), g_out.reshape(1, d), s0)


def _t5_bucket(rel):
    half = NUM_BUCKETS // 2
    ret = jnp.where(rel > 0, half, 0)
    n = jnp.abs(rel)
    max_exact = half // 2
    nf = jnp.maximum(n, 1).astype(F32)
    large = max_exact + (jnp.log(nf / max_exact) / math.log(MAX_DISTANCE / max_exact)
                         * (half - max_exact)).astype(jnp.int32)
    large = jnp.minimum(large, half - 1)
    return ret + jnp.where(n < max_exact, n, large)


def _bias_tiles_t(rel_bias, d0_list, tq, tk):
    p = tq + tk
    m = jnp.arange(p, dtype=jnp.int32)
    rel = jnp.asarray(d0_list, jnp.int32)[:, None] + jnp.where(m <= tq, -m, p - m)[None, :]
    w = jnp.transpose(rel_bias[_t5_bucket(rel)], (2, 0, 1)).astype(F32)
    h, nd = w.shape[0], w.shape[1]
    flat = jnp.broadcast_to(w[:, :, None, :], (h, nd, tk, p)).reshape(h, nd, tk * p)
    return flat[:, :, :tk * (p - 1)].reshape(h, nd, tk, p - 1)[:, :, :, :tq]


def _attn_kernel(*refs, segs, q_off, tq, hb):
    nseg = len(segs)
    q_ref, lam_ref, gn_ref = refs[0], refs[1], refs[2]
    seg_refs = [refs[3 + 3 * s: 6 + 3 * s] for s in range(nseg)]
    o_ref = refs[3 + 3 * nseg]
    m_ref, l_ref, acc_ref = refs[4 + 3 * nseg: 7 + 3 * nseg]
    kv_scratch = refs[7 + 3 * nseg:]
    seg_scr = [kv_scratch[2 * s: 2 * s + 2] for s in range(nseg)]

    i = pl.program_id(2)
    q0 = q_off + i * tq

    @pl.when(i == 0)
    def _():
        for (k_ref, v_ref, _), (kb_ref, vt_ref), (_, seg_len, tk, _, _) in zip(seg_refs, seg_scr, segs):
            kb_ref[...] = k_ref[0].astype(BF16)
            for kt in range(seg_len // tk):
                vt_ref[kt] = v_ref[0, kt * tk:(kt + 1) * tk, :].astype(F32).T.astype(BF16)

    q = q_ref[0]
    lane = lax.broadcasted_iota(jnp.int32, (tq, HEAD_W), 1)
    qz = []
    for hh in range(hb):
        qh = q[:, hh * HEAD_W:(hh + 1) * HEAD_W] * (DA_HEAD_DIM ** -0.5)
        qz.append([jnp.where(lane < DA_HEAD_DIM, qh, 0).astype(BF16),
                   jnp.where(lane >= DA_HEAD_DIM, qh, 0).astype(BF16)])
    m_ref[...] = jnp.full_like(m_ref, NEG_BIG)
    l_ref[...] = jnp.zeros_like(l_ref)
    acc_ref[...] = jnp.zeros_like(acc_ref)
    q_chunk = (q0 + lax.broadcasted_iota(jnp.int32, (1, tq), 1)) // CHUNK
    vis_end = ((q0 + tq - 1) // CHUNK + 1) * CHUNK
    full_end = (q0 // CHUNK + 1) * CHUNK
    nt = (((1,), (1,)), ((), ()))

    for (_, _, b_ref), (kb_ref, vt_ref), (pos0, seg_len, tk, d0_min, d0_step) in zip(seg_refs, seg_scr, segs):
        n_tiles = seg_len // tk
        n_vis = jnp.clip((vis_end - pos0 + tk - 1) // tk, 0, n_tiles)
        n_full = jnp.clip((full_end - pos0) // tk, 0, n_vis)

        def tile(kt, carry, masked, b_ref=b_ref, kb_ref=kb_ref, vt_ref=vt_ref, pos0=pos0, tk=tk,
                 d0_min=d0_min, d0_step=d0_step):
            r0 = pl.multiple_of(kt * tk, tk)
            d_idx = (pos0 + r0 - q0 - d0_min) // d0_step
            if masked:
                k_chunk = (pos0 + r0 + lax.broadcasted_iota(jnp.int32, (tk, 1), 0)) // CHUNK
                visible = k_chunk <= q_chunk
            scores = []
            for hh in range(hb):
                kb = kb_ref[pl.ds(r0, tk), hh * HEAD_W:(hh + 1) * HEAD_W]
                scores.append([lax.dot_general(kb, qz[hh][c], nt, preferred_element_type=F32)
                               for c in range(2)])
            for hh in range(hb):
                vt = vt_ref[kt, hh * HEAD_W:(hh + 1) * HEAD_W, :]
                bias = b_ref[hh, d_idx]
                for c in range(2):
                    j = 2 * hh + c
                    s = scores[hh][c] + bias
                    if masked:
                        s = jnp.where(visible, s, NEG_BIG)
                    m_prev = m_ref[j]
                    m_new = jnp.maximum(m_prev, jnp.max(s, axis=0, keepdims=True))
                    p = jnp.exp(s - m_new)
                    alpha = jnp.exp(m_prev - m_new)
                    l_ref[j] = alpha * l_ref[j] + jnp.sum(p, axis=0, keepdims=True)
                    acc_ref[j] = alpha * acc_ref[j] + jnp.dot(vt, p.astype(BF16),
                                                              preferred_element_type=F32)
                    m_ref[j] = m_new
            return carry

        lax.fori_loop(0, n_full, functools.partial(tile, masked=False), 0)
        lax.fori_loop(n_full, n_vis, functools.partial(tile, masked=True), 0)

    lam = lam_ref[0:1, 0:1]
    for hh in range(hb):
        o_t = (acc_ref[2 * hh] * (1.0 / l_ref[2 * hh])
               - lam * (acc_ref[2 * hh + 1] * (1.0 / l_ref[2 * hh + 1])))
        cols = slice(hh * HEAD_W, (hh + 1) * HEAD_W)
        o_ref[0, :, cols] = (_rms(o_t.T, gn_ref[:, cols]) * lam_ref[0:1, 1:2]).astype(o_ref.dtype)


def _attention(q, segments, rel_bias, lam, out_scale, g_subln, q_off, name):
    b, t_real, d = q.shape
    tq_all = max(t_real, HEAD_W)
    if tq_all != t_real:
        q = jnp.pad(q, ((0, 0), (0, tq_all - t_real), (0, 0)))
    tq = _row_tile(tq_all, ATTN_TILE)
    nq = tq_all // tq
    hb = ATTN_HEADS_PER_STEP
    hw = hb * HEAD_W
    segs, args, in_specs, scratch = [], [], [], []
    for (k, v, pos0, tk) in segments:
        seg_len = k.shape[1]
        assert seg_len % tk == 0
        d0s = sorted({pos0 + kt * tk - (q_off + i * tq)
                      for i in range(nq) for kt in range(seg_len // tk)
                      if pos0 + kt * tk < ((q_off + (i + 1) * tq - 1) // CHUNK + 1) * CHUNK})
        step = math.gcd(tq, tk)
        d0s = list(range(d0s[0], d0s[-1] + 1, step))
        tiles = _bias_tiles_t(rel_bias, d0s, tq, tk)
        segs.append((pos0, seg_len, tk, d0s[0], step))
        args += [k, v, tiles]
        in_specs += [pl.BlockSpec((1, seg_len, hw), lambda bi, h, i: (bi, 0, h)),
                     pl.BlockSpec((1, seg_len, hw), lambda bi, h, i: (bi, 0, h)),
                     pl.BlockSpec((hb, len(d0s), tk, tq), lambda bi, h, i: (h, 0, 0, 0))]
        scratch += [pltpu.VMEM((seg_len, hw), BF16), pltpu.VMEM((seg_len // tk, hw, tk), BF16)]
    scal = jnp.zeros((1, HEAD_W), F32).at[0, 0].set(lam).at[0, 1].set(out_scale)
    kern = functools.partial(_attn_kernel, segs=tuple(segs), q_off=q_off, tq=tq, hb=hb)
    out = pl.pallas_call(
        kern,
        out_shape=jax.ShapeDtypeStruct((b, tq_all, d), BF16),
        grid=(b, N_HEADS // hb, nq),
        in_specs=[pl.BlockSpec((1, tq, hw), lambda bi, h, i: (bi, i, h)),
                  pl.BlockSpec((1, HEAD_W), lambda bi, h, i: (0, 0)),
                  pl.BlockSpec((1, hw), lambda bi, h, i: (0, h))] + in_specs,
        out_specs=pl.BlockSpec((1, tq, hw), lambda bi, h, i: (bi, i, h)),
        scratch_shapes=[pltpu.VMEM((2 * hb, 1, tq), F32), pltpu.VMEM((2 * hb, 1, tq), F32),
                        pltpu.VMEM((2 * hb, HEAD_W, tq), F32)] + scratch,
        compiler_params=_cparams("parallel", "parallel", "arbitrary"),
        name=name,
    )(q, scal, g_subln.reshape(1, d), *args)
    return out[:, :t_real]


def _trunk(x, ple, q_off, past_k, past_v, hg_s0, conv_s0, wts, tag):
    (g_norms, w_in_a, lower, g_hg, w_out_a, g_kv, w_kv, rel_bias, w_q_b, lam_b,
     g_subln, w_out_b, w_ffn_in, conv_w, conv_b, w_ffn_out, w_ple, w_ple_gate) = wts
    b, t, d = x.shape
    depth = g_norms.shape[0]
    n_a = w_in_a.shape[0]
    m = b * t
    hg_out, conv_out = [], []
    k_new = v_new = None
    segments = None
    for i in range(depth):
        nm = f"{tag}{i}"
        x2 = x.reshape(m, d)
        if i < n_a:
            lw = lower[i]
            proj = _norm_matmul(x2, g_norms[i, 0], w_in_a[i], F32, nm + "_hg_in")
            o, s = _gla(proj.reshape(b, t, 4 * d), jnp.log(lw), jnp.log1p(-lw), 1.0 - lw,
                        g_hg[i], hg_s0[i], nm + "_gla")
            hg_out.append(s)
            x2 = _matmul_norm_res(o.reshape(m, d), w_out_a[i], g_norms[i, 1], x2, nm + "_hg_out")
        else:
            j = i - n_a
            q = _norm_matmul(x2, g_norms[i, 0], w_q_b[j], BF16, nm + "_q")
            lam_init = 0.8 - 0.6 * math.exp(-0.3 * i)
            lp = lam_b[j].astype(F32)
            lam = jnp.exp(jnp.sum(lp[0] * lp[1])) - jnp.exp(jnp.sum(lp[2] * lp[3])) + lam_init
            o = _attention(q.reshape(b, t, d), segments, rel_bias, lam, 1.0 - lam_init,
                           g_subln[j], q_off, nm + "_attn")
            x2 = _matmul_norm_res(o.reshape(m, d), w_out_b[j], g_norms[i, 1], x2, nm + "_attn_out")
        x3, cbuf = _ffn(x2.reshape(b, t, d), g_norms[i, 2], w_ffn_in[i], conv_w[i], conv_b[i],
                        w_ffn_out[i], g_norms[i, 3], conv_s0[i], nm + "_ffn")
        conv_out.append(cbuf)
        x2 = _ple(x3.reshape(m, d), ple[i].reshape(m, -1), w_ple_gate[i], w_ple[i], nm + "_ple")
        x = x2.reshape(b, t, d)
        if i == n_a - 1:
            k_new, v_new = _norm_kv(x2, g_kv, w_kv, nm + "_kv")
            k_new = k_new.reshape(b, t, d)
            v_new = v_new.reshape(b, t, d)
            segments = []
            if past_k is not None:
                tp = past_k.shape[1]
                segments.append((past_k.reshape(b, tp, d), past_v.reshape(b, tp, d), 0, tp))
            segments.append((k_new, v_new, q_off, min(t, ATTN_TILE)))
    hd = (b, t, N_HEADS, HEAD_W)
    return x, k_new.reshape(hd), v_new.reshape(hd), jnp.stack(hg_out), jnp.stack(conv_out)


def kernel(x_prompt, x_sample, cache_k, cache_v, state_hgrn, state_conv, p_prompt, p_sample,
           g_norms, w_in_a, lb_raw, g_hg, w_out_a, g_kv, w_kv, rel_bias,
           w_q_b, lam_b, g_subln, w_out_b, w_ffn_in, conv_w, conv_b, w_ffn_out,
           w_ple, w_ple_gate):
    sm = jax.nn.softmax(lb_raw.astype(F32), axis=0)
    cs = jnp.cumsum(sm, axis=0)
    lower = cs - cs[0:1]
    bf = lambda w: w.astype(BF16)
    wts = (g_norms, bf(w_in_a), lower, g_hg, bf(w_out_a), g_kv, bf(w_kv), rel_bias,
           bf(w_q_b), lam_b, g_subln, bf(w_out_b), bf(w_ffn_in), conv_w, conv_b,
           bf(w_ffn_out), bf(w_ple), bf(w_ple_gate))
    bp, tp, _ = x_prompt.shape
    n_a, depth = w_in_a.shape[0], g_norms.shape[0]
    hg0 = jnp.zeros((n_a, bp) + state_hgrn.shape[2:], F32)
    cv0 = jnp.zeros((depth, bp) + state_conv.shape[2:], F32)
    y_p, k_p, v_p, hg_p, cv_p = _trunk(x_prompt, p_prompt, 0, None, None, hg0, cv0, wts, "p")
    y_s, k_s, v_s, hg_s, cv_s = _trunk(x_sample, p_sample, cache_k.shape[1], cache_k, cache_v,
                                       state_hgrn, state_conv, wts, "s")
    return (y_p, y_s, k_p, v_p, k_s, v_s, hg_p, hg_s, cv_p, cv_s)
```

```python
import functools
import math

import jax
import jax.numpy as jnp
from jax import lax
from jax.experimental import pallas as pl
from jax.experimental.pallas import tpu as pltpu

F32 = jnp.float32
BF16 = jnp.bfloat16

D_MODEL = 1024
N_HEADS = 8
HEAD_W = D_MODEL // N_HEADS
DA_HEAD_DIM = HEAD_W // 2
CHUNK = 64
GLA_BLOCK = 16
NUM_BUCKETS = 32
MAX_DISTANCE = 256
D_FF = 2816
CONV_W = 3
EPS = 1e-6
NEG_BIG = -1e30

V7X_VMEM_LIMIT_BYTES = 56 * 1024 * 1024

ROW_TILE = 512
FFN_ROW_TILE = 1024
FFN_COL_TILE = 256
GLA_TILE = 256
ATTN_TILE = 512
ATTN_KEY_TILE = 256
ATTN_HEADS_PER_STEP = 2


def _cparams(*sem):
    return pltpu.CompilerParams(dimension_semantics=sem,
                                vmem_limit_bytes=V7X_VMEM_LIMIT_BYTES)


def _rms(x, g):
    ms = jnp.mean(x * x, axis=-1, keepdims=True)
    return x * lax.rsqrt(ms + EPS) * g


def _sigmoid(z):
    return 1.0 / (1.0 + jnp.exp(-z))


def _row_tile(m, cap):
    t = min(m, cap)
    assert m % t == 0, (m, t)
    return t


def _norm_matmul_kernel(x_ref, g_ref, w_ref, o_ref, h_ref):
    @pl.when(pl.program_id(1) == 0)
    def _():
        h_ref[...] = _rms(x_ref[...], g_ref[...]).astype(BF16)

    o_ref[...] = jnp.dot(h_ref[...], w_ref[...],
                         preferred_element_type=F32).astype(o_ref.dtype)


def _norm_matmul(x2d, g, w, out_dtype, name):
    m, d = x2d.shape
    n = w.shape[1]
    tm = _row_tile(m, ROW_TILE)
    tn = _row_tile(n, 1024)
    return pl.pallas_call(
        _norm_matmul_kernel,
        out_shape=jax.ShapeDtypeStruct((m, n), out_dtype),
        grid=(m // tm, n // tn),
        in_specs=[pl.BlockSpec((tm, d), lambda i, j: (i, 0)),
                  pl.BlockSpec((1, d), lambda i, j: (0, 0)),
                  pl.BlockSpec((d, tn), lambda i, j: (0, j))],
        out_specs=pl.BlockSpec((tm, tn), lambda i, j: (i, j)),
        scratch_shapes=[pltpu.VMEM((tm, d), BF16)],
        compiler_params=_cparams("parallel", "arbitrary"),
        name=name,
    )(x2d, g.reshape(1, d), w)


def _norm_kv_kernel(x_ref, g_ref, w_ref, k_ref, v_ref):
    d = k_ref.shape[1]
    h = _rms(x_ref[...], g_ref[...]).astype(BF16)
    k_ref[...] = jnp.dot(h, w_ref[:, :d], preferred_element_type=F32)
    v_ref[...] = jnp.dot(h, w_ref[:, d:], preferred_element_type=F32)


def _norm_kv(x2d, g, w, name):
    m, d = x2d.shape
    tm = _row_tile(m, ROW_TILE)
    out = jax.ShapeDtypeStruct((m, d), F32)
    return pl.pallas_call(
        _norm_kv_kernel,
        out_shape=(out, out),
        grid=(m // tm,),
        in_specs=[pl.BlockSpec((tm, d), lambda i: (i, 0)),
                  pl.BlockSpec((1, d), lambda i: (0, 0)),
                  pl.BlockSpec((d, 2 * d), lambda i: (0, 0))],
        out_specs=(pl.BlockSpec((tm, d), lambda i: (i, 0)),
                   pl.BlockSpec((tm, d), lambda i: (i, 0))),
        compiler_params=_cparams("parallel"),
        name=name,
    )(x2d, g.reshape(1, d), w)


def _matmul_norm_res_kernel(a_ref, w_ref, g_ref, x_ref, o_ref):
    m = jnp.dot(a_ref[...], w_ref[...], preferred_element_type=F32)
    o_ref[...] = x_ref[...] + _rms(m, g_ref[...])


def _matmul_norm_res(a2d, w, g, x2d, name):
    m, k = a2d.shape
    d = w.shape[1]
    tm = _row_tile(m, ROW_TILE)
    return pl.pallas_call(
        _matmul_norm_res_kernel,
        out_shape=jax.ShapeDtypeStruct((m, d), F32),
        grid=(m // tm,),
        in_specs=[pl.BlockSpec((tm, k), lambda i: (i, 0)),
                  pl.BlockSpec((k, d), lambda i: (0, 0)),
                  pl.BlockSpec((1, d), lambda i: (0, 0)),
                  pl.BlockSpec((tm, d), lambda i: (i, 0))],
        out_specs=pl.BlockSpec((tm, d), lambda i: (i, 0)),
        compiler_params=_cparams("parallel"),
        name=name,
    )(a2d, w, g.reshape(1, d), x2d)


def _ple_kernel(x_ref, p_ref, wg_ref, wp_ref, o_ref):
    x = x_ref[...]
    z = jnp.dot(x.astype(BF16), wg_ref[...], preferred_element_type=F32)
    e = jnp.dot(p_ref[...].astype(BF16), wp_ref[...], preferred_element_type=F32)
    o_ref[...] = x + _sigmoid(z) * e


def _ple(x2d, p2d, wg, wp, name):
    m, d = x2d.shape
    pd = p2d.shape[1]
    tm = _row_tile(m, ROW_TILE)
    return pl.pallas_call(
        _ple_kernel,
        out_shape=jax.ShapeDtypeStruct((m, d), F32),
        grid=(m // tm,),
        in_specs=[pl.BlockSpec((tm, d), lambda i: (i, 0)),
                  pl.BlockSpec((tm, pd), lambda i: (i, 0)),
                  pl.BlockSpec((d, d), lambda i: (0, 0)),
                  pl.BlockSpec((pd, d), lambda i: (0, 0))],
        out_specs=pl.BlockSpec((tm, d), lambda i: (i, 0)),
        compiler_params=_cparams("parallel"),
        name=name,
    )(x2d, p2d, wg, wp)


def _ffn_kernel(x_ref, gin_ref, wg_ref, wu_ref, cw_ref, cb_ref, wo_ref, gout_ref,
                cs_ref, o_ref, co_ref, h_ref, acc_ref):
    t = pl.program_id(1)
    j = pl.program_id(2)
    tm = x_ref.shape[1]

    @pl.when(j == 0)
    def _():
        h_ref[...] = _rms(x_ref[0], gin_ref[...]).astype(BF16)
        acc_ref[...] = jnp.zeros_like(acc_ref)

    @pl.when(t == 0)
    def _():
        co_ref[0, j] = cs_ref[0, j]

    h = h_ref[...]
    g = jnp.dot(h, wg_ref[...], preferred_element_type=F32)
    u = jnp.dot(h, wu_ref[...], preferred_element_type=F32)
    prev = co_ref[0, j]
    co_ref[0, j] = g[tm - (CONV_W - 1):, :]
    row = lax.broadcasted_iota(jnp.int32, g.shape, 0)
    g1 = jnp.where(row == 0, prev[1:2, :], pltpu.roll(g, 1, axis=0))
    g2 = jnp.where(row == 0, prev[0:1, :],
                   jnp.where(row == 1, prev[1:2, :], pltpu.roll(g, 2, axis=0)))
    cw = cw_ref[...]
    c = cb_ref[...] + cw[0:1, :] * g2 + cw[1:2, :] * g1 + cw[2:3, :] * g
    act = (c * _sigmoid(c) * u).astype(BF16)
    acc_ref[...] += jnp.dot(act, wo_ref[...], preferred_element_type=F32)

    @pl.when(j == pl.num_programs(2) - 1)
    def _():
        o_ref[0] = x_ref[0] + _rms(acc_ref[...], gout_ref[...])


def _ffn(x, gin, w_in, cw, cb, w_out, gout, conv_state, name):
    b, t, d = x.shape
    f = w_out.shape[0]
    tm = _row_tile(t, FFN_ROW_TILE)
    tf = FFN_COL_TILE
    nf = f // tf
    assert f % tf == 0 and t >= CONV_W - 1
    cs = conv_state.reshape(b, CONV_W - 1, nf, tf).transpose(0, 2, 1, 3)
    cs_spec = pl.BlockSpec((1, nf, CONV_W - 1, tf), lambda bi, ti, j: (bi, 0, 0, 0))
    y, co = pl.pallas_call(
        _ffn_kernel,
        out_shape=(jax.ShapeDtypeStruct((b, t, d), F32),
                   jax.ShapeDtypeStruct(cs.shape, F32)),
        grid=(b, t // tm, nf),
        in_specs=[pl.BlockSpec((1, tm, d), lambda bi, ti, j: (bi, ti, 0)),
                  pl.BlockSpec((1, d), lambda bi, ti, j: (0, 0)),
                  pl.BlockSpec((d, tf), lambda bi, ti, j: (0, j)),
                  pl.BlockSpec((d, tf), lambda bi, ti, j: (0, j + nf)),
                  pl.BlockSpec((CONV_W, tf), lambda bi, ti, j: (0, j)),
                  pl.BlockSpec((1, tf), lambda bi, ti, j: (0, j)),
                  pl.BlockSpec((tf, d), lambda bi, ti, j: (j, 0)),
                  pl.BlockSpec((1, d), lambda bi, ti, j: (0, 0)),
                  cs_spec],
        out_specs=(pl.BlockSpec((1, tm, d), lambda bi, ti, j: (bi, ti, 0)), cs_spec),
        scratch_shapes=[pltpu.VMEM((tm, d), BF16),
                        pltpu.VMEM((tm, d), F32)],
        compiler_params=_cparams("parallel", "arbitrary", "arbitrary"),
        name=name,
    )(x, gin.reshape(1, d), w_in, w_in, cw, cb.reshape(1, f), w_out,
      gout.reshape(1, d), cs)
    return y, co.transpose(0, 2, 1, 3).reshape(b, CONV_W - 1, f)


def _split3(x):
    hi = x.astype(BF16)
    r1 = x - hi.astype(F32)
    mid = r1.astype(BF16)
    lo = (r1 - mid.astype(F32)).astype(BF16)
    return hi, mid, lo


def _gla_kernel(q_ref, f_ref, i_ref, g_ref, ll_ref, l1m_ref, om_ref, gn_ref, s0_ref,
                o_ref, so_ref, st_ref, qt_ref, kt_ref, kd_ref, v_ref, dec_ref, oo_ref):
    t = pl.program_id(1)
    tc = q_ref.shape[1]
    nblk = tc // GLA_BLOCK

    @pl.when(t == 0)
    def _():
        for h in range(N_HEADS):
            st_ref[h] = s0_ref[0, h].T

    qh = q_ref[0]
    fp = f_ref[0]
    q = qh * _sigmoid(qh)
    e = jnp.exp(-jnp.abs(fp))
    logsig = jnp.minimum(fp, 0.0) - jnp.log1p(e)
    a = ll_ref[...]
    c = l1m_ref[...] + logsig
    logf = jnp.maximum(a, c) + jnp.log1p(jnp.exp(-jnp.abs(a - c)))
    kk = om_ref[...] * (jnp.where(fp >= 0.0, e, 1.0) / (1.0 + e))

    row = lax.broadcasted_iota(jnp.int32, (tc, tc), 0)
    col = lax.broadcasted_iota(jnp.int32, (tc, tc), 1)
    same = (row // GLA_BLOCK) == (col // GLA_BLOCK)
    m_incl = jnp.where(same & (col <= row), 1.0, 0.0).astype(BF16)
    m_rest = jnp.where(same & (col > row), 1.0, 0.0).astype(BF16)
    parts = _split3(logf)
    b = sum(jnp.dot(m_incl, p, preferred_element_type=F32) for p in parts)
    r = sum(jnp.dot(m_rest, p, preferred_element_type=F32) for p in parts)
    brow = lax.broadcasted_iota(jnp.int32, (nblk * 8, tc), 0)
    bcol = lax.broadcasted_iota(jnp.int32, (nblk * 8, tc), 1)
    m_tot = jnp.where((brow // 8) == (bcol // GLA_BLOCK), 1.0, 0.0).astype(BF16)
    tot = sum(jnp.dot(m_tot, p, preferred_element_type=F32) for p in parts)
    dec_ref[...] = jnp.exp(tot).reshape(nblk, 8, tot.shape[-1])

    qt_ref[...] = (q * jnp.exp(b)).astype(BF16)
    kt_ref[...] = (kk * jnp.exp(-b)).astype(BF16)
    kd_ref[...] = (kk * jnp.exp(r)).astype(BF16)
    v_ref[...] = i_ref[0].astype(BF16)

    lrow = lax.broadcasted_iota(jnp.int32, (GLA_BLOCK, GLA_BLOCK), 0)
    lcol = lax.broadcasted_iota(jnp.int32, (GLA_BLOCK, GLA_BLOCK), 1)
    causal = lcol <= lrow
    nt = (((1,), (1,)), ((), ()))
    tn = (((0,), (0,)), ((), ()))

    def block(jb, carry):
        r0 = pl.multiple_of(jb * GLA_BLOCK, GLA_BLOCK)
        rows = pl.ds(r0, GLA_BLOCK)
        for h in range(N_HEADS):
            cols = slice(h * HEAD_W, (h + 1) * HEAD_W)
            qt = qt_ref[rows, cols]
            kt = kt_ref[rows, cols]
            kd = kd_ref[rows, cols]
            vb = v_ref[rows, cols]
            s_t = st_ref[h]
            o_inter = lax.dot_general(qt, s_t.astype(BF16), nt, preferred_element_type=F32)
            att = lax.dot_general(qt, kt, nt, preferred_element_type=F32)
            att = jnp.where(causal, att, 0.0).astype(BF16)
            o_intra = jnp.dot(att, vb, preferred_element_type=F32)
            oo_ref[rows, cols] = o_inter + o_intra
            upd = lax.dot_general(vb, kd, tn, preferred_element_type=F32)
            decay = dec_ref[jb, :, cols]
            st_ref[h] = s_t * decay[0:1, :] + upd
        return carry

    lax.fori_loop(0, nblk, block, 0)

    gh = g_ref[0]
    gate = gh * _sigmoid(gh)
    for h in range(N_HEADS):
        cols = slice(h * HEAD_W, (h + 1) * HEAD_W)
        oh = _rms(oo_ref[:, cols], gn_ref[:, cols])
        o_ref[0, :, cols] = (oh * gate[:, cols]).astype(o_ref.dtype)

    @pl.when(t == pl.num_programs(1) - 1)
    def _():
        for h in range(N_HEADS):
            so_ref[0, h] = st_ref[h].T


def _gla(proj, log_lower, log1m_lower, one_m_lower, g_out, s0, name):
    b, t, _ = proj.shape
    d = D_MODEL
    tc = _row_tile(t, GLA_TILE)
    assert tc % GLA_BLOCK == 0
    vec = pl.BlockSpec((1, d), lambda bi, ti: (0, 0))
    col_spec = lambda c: pl.BlockSpec((1, tc, d), lambda bi, ti: (bi, ti, c))
    st_spec = pl.BlockSpec((1, N_HEADS, HEAD_W, HEAD_W), lambda bi, ti: (bi, 0, 0, 0))
    return pl.pallas_call(
        _gla_kernel,
        out_shape=(jax.ShapeDtypeStruct((b, t, d), BF16),
                   jax.ShapeDtypeStruct((b, N_HEADS, HEAD_W, HEAD_W), F32)),
        grid=(b, t // tc),
        in_specs=[col_spec(0), col_spec(1), col_spec(2), col_spec(3),
                  vec, vec, vec, vec, st_spec],
        out_specs=(pl.BlockSpec((1, tc, d), lambda bi, ti: (bi, ti, 0)), st_spec),
        scratch_shapes=[pltpu.VMEM((N_HEADS, HEAD_W, HEAD_W), F32),
                        pltpu.VMEM((tc, d), BF16), pltpu.VMEM((tc, d), BF16),
                        pltpu.VMEM((tc, d), BF16), pltpu.VMEM((tc, d), BF16),
                        pltpu.VMEM((tc // GLA_BLOCK, 8, d), F32), pltpu.VMEM((tc, d), F32)],
        compiler_params=_cparams("parallel", "arbitrary"),
        name=name,
    )(proj, proj, proj, proj, log_lower.reshape(1, d), log1m_lower.reshape(1, d),
      one_m_lower.reshape(1, d), g_out.reshape(1, d), s0)


def _t5_bucket(rel):
    half = NUM_BUCKETS // 2
    ret = jnp.where(rel > 0, half, 0)
    n = jnp.abs(rel)
    max_exact = half // 2
    nf = jnp.maximum(n, 1).astype(F32)
    large = max_exact + (jnp.log(nf / max_exact) / math.log(MAX_DISTANCE / max_exact)
                         * (half - max_exact)).astype(jnp.int32)
    large = jnp.minimum(large, half - 1)
    return ret + jnp.where(n < max_exact, n, large)


def _bias_tiles_t(rel_bias, d0_list, tq, tk):
    p = tq + tk
    m = jnp.arange(p, dtype=jnp.int32)
    rel = jnp.asarray(d0_list, jnp.int32)[:, None] + jnp.where(m <= tq, -m, p - m)[None, :]
    onehot = (_t5_bucket(rel)[:, :, None] == jnp.arange(NUM_BUCKETS, dtype=jnp.int32)).astype(F32)
    w = jnp.einsum("dpn,nh->hdp", onehot, rel_bias.astype(F32),
                   precision=lax.Precision.HIGHEST)
    h, nd = w.shape[0], w.shape[1]
    flat = jnp.broadcast_to(w[:, :, None, :], (h, nd, tk, p)).reshape(h, nd, tk * p)
    return flat[:, :, :tk * (p - 1)].reshape(h, nd, tk, p - 1)[:, :, :, :tq]


def _attn_kernel(*refs, segs, q_off, tq, hb):
    nseg = len(segs)
    q_ref, lam_ref, gn_ref = refs[0], refs[1], refs[2]
    seg_refs = [refs[3 + 3 * s: 6 + 3 * s] for s in range(nseg)]
    o_ref = refs[3 + 3 * nseg]
    m_ref, l_ref, acc_ref = refs[4 + 3 * nseg: 7 + 3 * nseg]
    kv_scratch = refs[7 + 3 * nseg:]
    seg_scr = [kv_scratch[4 * s: 4 * s + 4] for s in range(nseg)]

    i = pl.program_id(2)
    q0 = q_off + i * tq

    @pl.when(i == 0)
    def _():
        for (k_ref, v_ref, _), (kb_ref, vt_ref, _, _), (_, seg_len, tk, _, _) in zip(seg_refs, seg_scr, segs):
            kb_ref[...] = k_ref[0].astype(BF16)
            for kt in range(seg_len // tk):
                vt_ref[kt] = v_ref[0, kt * tk:(kt + 1) * tk, :].astype(F32).T.astype(BF16)

    q = q_ref[0]
    lane = lax.broadcasted_iota(jnp.int32, (tq, HEAD_W), 1)
    qz = []
    for hh in range(hb):
        qh = q[:, hh * HEAD_W:(hh + 1) * HEAD_W] * (DA_HEAD_DIM ** -0.5)
        qz.append([jnp.where(lane < DA_HEAD_DIM, qh, 0).astype(BF16),
                   jnp.where(lane >= DA_HEAD_DIM, qh, 0).astype(BF16)])
    m_ref[...] = jnp.full_like(m_ref, NEG_BIG)
    l_ref[...] = jnp.zeros_like(l_ref)
    acc_ref[...] = jnp.zeros_like(acc_ref)
    q_chunk = (q0 + lax.broadcasted_iota(jnp.int32, (1, tq), 1)) // CHUNK
    vis_end = ((q0 + tq - 1) // CHUNK + 1) * CHUNK
    full_end = (q0 // CHUNK + 1) * CHUNK
    nt = (((1,), (1,)), ((), ()))

    for (_, _, b_ref), (kb_ref, vt_ref, sa_ref, sb_ref), (pos0, seg_len, tk, d0_min, d0_step) in zip(
            seg_refs, seg_scr, segs):
        n_tiles = seg_len // tk
        n_vis = jnp.clip((vis_end - pos0 + tk - 1) // tk, 0, n_tiles)
        n_full = jnp.clip((full_end - pos0) // tk, 0, n_vis)

        def scores_into(kt, s_ref, kb_ref=kb_ref, tk=tk, n_tiles=n_tiles):
            r0 = pl.multiple_of(jnp.minimum(kt, n_tiles - 1) * tk, tk)
            for hh in range(hb):
                kb = kb_ref[pl.ds(r0, tk), hh * HEAD_W:(hh + 1) * HEAD_W]
                for c in range(2):
                    s_ref[2 * hh + c] = lax.dot_general(kb, qz[hh][c], nt, preferred_element_type=F32)

        def consume(kt, s_ref, masked, b_ref=b_ref, vt_ref=vt_ref, pos0=pos0, tk=tk,
                    d0_min=d0_min, d0_step=d0_step, n_tiles=n_tiles, n_vis=n_vis):
            ktc = jnp.minimum(kt, n_tiles - 1)
            r0 = ktc * tk
            d_idx = (pos0 + r0 - q0 - d0_min) // d0_step
            if masked:
                k0 = jnp.where(kt < n_vis, pos0 + r0, jnp.int32(1 << 28))
                k_chunk = (k0 + lax.broadcasted_iota(jnp.int32, (tk, 1), 0)) // CHUNK
                visible = k_chunk <= q_chunk
            for hh in range(hb):
                vt = vt_ref[ktc, hh * HEAD_W:(hh + 1) * HEAD_W, :]
                bias = b_ref[hh, d_idx]
                for c in range(2):
                    j = 2 * hh + c
                    s = s_ref[j] + bias
                    if masked:
                        s = jnp.where(visible, s, NEG_BIG)
                    m_prev = m_ref[j]
                    m_new = jnp.maximum(m_prev, jnp.max(s, axis=0, keepdims=True))
                    p = jnp.exp(s - m_new)
                    alpha = jnp.exp(m_prev - m_new)
                    l_ref[j] = alpha * l_ref[j] + jnp.sum(p, axis=0, keepdims=True)
                    acc_ref[j] = alpha * acc_ref[j] + jnp.dot(vt, p.astype(BF16),
                                                              preferred_element_type=F32)
                    m_ref[j] = m_new

        def pair(u, carry, masked, base, sa_ref=sa_ref, sb_ref=sb_ref):
            t0 = base + 2 * u
            scores_into(t0 + 1, sb_ref)
            consume(t0, sa_ref, masked)
            scores_into(t0 + 2, sa_ref)
            consume(t0 + 1, sb_ref, masked)
            return carry

        n_pairs_full = n_full // 2
        base = 2 * n_pairs_full
        scores_into(0, sa_ref)
        lax.fori_loop(0, n_pairs_full, functools.partial(pair, masked=False, base=0), 0)
        lax.fori_loop(0, (n_vis - base + 1) // 2, functools.partial(pair, masked=True, base=base), 0)

    lam = lam_ref[0:1, 0:1]
    for hh in range(hb):
        o_t = (acc_ref[2 * hh] * (1.0 / l_ref[2 * hh])
               - lam * (acc_ref[2 * hh + 1] * (1.0 / l_ref[2 * hh + 1])))
        cols = slice(hh * HEAD_W, (hh + 1) * HEAD_W)
        o_ref[0, :, cols] = (_rms(o_t.T, gn_ref[:, cols]) * lam_ref[0:1, 1:2]).astype(o_ref.dtype)


def _attention(q, segments, rel_bias, lam, out_scale, g_subln, q_off, name):
    b, t_real, d = q.shape
    tq_all = max(t_real, HEAD_W)
    if tq_all != t_real:
        q = jnp.pad(q, ((0, 0), (0, tq_all - t_real), (0, 0)))
    tq = _row_tile(tq_all, ATTN_TILE)
    nq = tq_all // tq
    hb = ATTN_HEADS_PER_STEP
    hw = hb * HEAD_W
    segs, args, in_specs, scratch = [], [], [], []
    for (k, v, pos0, tk) in segments:
        seg_len = k.shape[1]
        assert seg_len % tk == 0
        d0s = sorted({pos0 + kt * tk - (q_off + i * tq)
                      for i in range(nq) for kt in range(seg_len // tk)
                      if pos0 + kt * tk < ((q_off + (i + 1) * tq - 1) // CHUNK + 1) * CHUNK})
        step = math.gcd(tq, tk)
        d0s = list(range(d0s[0], d0s[-1] + 1, step))
        tiles = _bias_tiles_t(rel_bias, d0s, tq, tk)
        segs.append((pos0, seg_len, tk, d0s[0], step))
        args += [k, v, tiles]
        in_specs += [pl.BlockSpec((1, seg_len, hw), lambda bi, h, i: (bi, 0, h)),
                     pl.BlockSpec((1, seg_len, hw), lambda bi, h, i: (bi, 0, h)),
                     pl.BlockSpec((hb, len(d0s), tk, tq), lambda bi, h, i: (h, 0, 0, 0))]
        scratch += [pltpu.VMEM((seg_len, hw), BF16), pltpu.VMEM((seg_len // tk, hw, tk), BF16),
                    pltpu.VMEM((2 * hb, tk, tq), F32), pltpu.VMEM((2 * hb, tk, tq), F32)]
    scal = jnp.zeros((1, HEAD_W), F32).at[0, 0].set(lam).at[0, 1].set(out_scale)
    kern = functools.partial(_attn_kernel, segs=tuple(segs), q_off=q_off, tq=tq, hb=hb)
    out = pl.pallas_call(
        kern,
        out_shape=jax.ShapeDtypeStruct((b, tq_all, d), BF16),
        grid=(b, N_HEADS // hb, nq),
        in_specs=[pl.BlockSpec((1, tq, hw), lambda bi, h, i: (bi, i, h)),
                  pl.BlockSpec((1, HEAD_W), lambda bi, h, i: (0, 0)),
                  pl.BlockSpec((1, hw), lambda bi, h, i: (0, h))] + in_specs,
        out_specs=pl.BlockSpec((1, tq, hw), lambda bi, h, i: (bi, i, h)),
        scratch_shapes=[pltpu.VMEM((2 * hb, 1, tq), F32), pltpu.VMEM((2 * hb, 1, tq), F32),
                        pltpu.VMEM((2 * hb, HEAD_W, tq), F32)] + scratch,
        compiler_params=_cparams("parallel", "parallel", "arbitrary"),
        name=name,
    )(q, scal, g_subln.reshape(1, d), *args)
    return out[:, :t_real]


def _trunk(x, ple, q_off, past_k, past_v, hg_s0, conv_s0, wts, tag):
    (g_norms, w_in_a, lower, g_hg, w_out_a, g_kv, w_kv, rel_bias, w_q_b, lam_b,
     g_subln, w_out_b, w_ffn_in, conv_w, conv_b, w_ffn_out, w_ple, w_ple_gate) = wts
    b, t, d = x.shape
    depth = g_norms.shape[0]
    n_a = w_in_a.shape[0]
    m = b * t
    hg_out, conv_out = [], []
    k_new = v_new = None
    segments = None
    for i in range(depth):
        nm = f"{tag}{i}"
        x2 = x.reshape(m, d)
        if i < n_a:
            lw = lower[i]
            proj = _norm_matmul(x2, g_norms[i, 0], w_in_a[i], F32, nm + "_hg_in")
            o, s = _gla(proj.reshape(b, t, 4 * d), jnp.log(lw), jnp.log1p(-lw), 1.0 - lw,
                        g_hg[i], hg_s0[i], nm + "_gla")
            hg_out.append(s)
            x2 = _matmul_norm_res(o.reshape(m, d), w_out_a[i], g_norms[i, 1], x2, nm + "_hg_out")
        else:
            j = i - n_a
            q = _norm_matmul(x2, g_norms[i, 0], w_q_b[j], BF16, nm + "_q")
            lam_init = 0.8 - 0.6 * math.exp(-0.3 * i)
            lp = lam_b[j].astype(F32)
            lam = jnp.exp(jnp.sum(lp[0] * lp[1])) - jnp.exp(jnp.sum(lp[2] * lp[3])) + lam_init
            o = _attention(q.reshape(b, t, d), segments, rel_bias, lam, 1.0 - lam_init,
                           g_subln[j], q_off, nm + "_attn")
            x2 = _matmul_norm_res(o.reshape(m, d), w_out_b[j], g_norms[i, 1], x2, nm + "_attn_out")
        x3, cbuf = _ffn(x2.reshape(b, t, d), g_norms[i, 2], w_ffn_in[i], conv_w[i], conv_b[i],
                        w_ffn_out[i], g_norms[i, 3], conv_s0[i], nm + "_ffn")
        conv_out.append(cbuf)
        x2 = _ple(x3.reshape(m, d), ple[i].reshape(m, -1), w_ple_gate[i], w_ple[i], nm + "_ple")
        x = x2.reshape(b, t, d)
        if i == n_a - 1:
            k_new, v_new = _norm_kv(x2, g_kv, w_kv, nm + "_kv")
            k_new = k_new.reshape(b, t, d)
            v_new = v_new.reshape(b, t, d)
            segments = []
            if past_k is not None:
                tp = past_k.shape[1]
                segments.append((past_k.reshape(b, tp, d), past_v.reshape(b, tp, d), 0, tp))
            segments.append((k_new, v_new, q_off, min(t, ATTN_KEY_TILE)))
    hd = (b, t, N_HEADS, HEAD_W)
    return x, k_new.reshape(hd), v_new.reshape(hd), jnp.stack(hg_out), jnp.stack(conv_out)


def kernel(x_prompt, x_sample, cache_k, cache_v, state_hgrn, state_conv, p_prompt, p_sample,
           g_norms, w_in_a, lb_raw, g_hg, w_out_a, g_kv, w_kv, rel_bias,
           w_q_b, lam_b, g_subln, w_out_b, w_ffn_in, conv_w, conv_b, w_ffn_out,
           w_ple, w_ple_gate):
    sm = jax.nn.softmax(lb_raw.astype(F32), axis=0)
    cs = jnp.cumsum(sm, axis=0)
    lower = cs - cs[0:1]
    bf = lambda w: w.astype(BF16)
    wts = (g_norms, bf(w_in_a), lower, g_hg, bf(w_out_a), g_kv, bf(w_kv), rel_bias,
           bf(w_q_b), lam_b, g_subln, bf(w_out_b), bf(w_ffn_in), conv_w, conv_b,
           bf(w_ffn_out), bf(w_ple), bf(w_ple_gate))
    bp, tp, _ = x_prompt.shape
    n_a, depth = w_in_a.shape[0], g_norms.shape[0]
    hg0 = jnp.zeros((n_a, bp) + state_hgrn.shape[2:], F32)
    cv0 = jnp.zeros((depth, bp) + state_conv.shape[2:], F32)
    y_p, k_p, v_p, hg_p, cv_p = _trunk(x_prompt, p_prompt, 0, None, None, hg0, cv0, wts, "p")
    y_s, k_s, v_s, hg_s, cv_s = _trunk(x_sample, p_sample, cache_k.shape[1], cache_k, cache_v,
                                       state_hgrn, state_conv, wts, "s")
    return (y_p, y_s, k_p, v_p, k_s, v_s, hg_p, hg_s, cv_p, cv_s)
```

```python
import functools
import math

import numpy as np
import jax
import jax.numpy as jnp
from jax import lax
from jax.experimental import pallas as pl
from jax.experimental.pallas import tpu as pltpu

F32 = jnp.float32
BF16 = jnp.bfloat16

D_MODEL = 1024
N_HEADS = 8
HEAD_W = D_MODEL // N_HEADS
DA_HEAD_DIM = HEAD_W // 2
CHUNK = 64
GLA_BLOCK = 16
NUM_BUCKETS = 32
MAX_DISTANCE = 256
D_FF = 2816
CONV_W = 3
EPS = 1e-6
NEG_BIG = -1e30

V7X_VMEM_LIMIT_BYTES = 56 * 1024 * 1024

ROW_TILE = 512
FFN_ROW_TILE = 1024
FFN_COL_TILE = 256
GLA_TILE = 256
ATTN_TILE = 512
ATTN_KEY_TILE = 256
ATTN_HEADS_PER_STEP = 2


def _cparams(*sem):
    return pltpu.CompilerParams(dimension_semantics=sem,
                                vmem_limit_bytes=V7X_VMEM_LIMIT_BYTES)


def _rms(x, g):
    ms = jnp.mean(x * x, axis=-1, keepdims=True)
    return x * lax.rsqrt(ms + EPS) * g


def _sigmoid(z):
    return 1.0 / (1.0 + jnp.exp(-z))


def _row_tile(m, cap):
    t = min(m, cap)
    assert m % t == 0, (m, t)
    return t


def _norm_matmul_kernel(x_ref, g_ref, w_ref, o_ref, h_ref):
    @pl.when(pl.program_id(1) == 0)
    def _():
        h_ref[...] = _rms(x_ref[...], g_ref[...]).astype(BF16)

    o_ref[...] = jnp.dot(h_ref[...], w_ref[...],
                         preferred_element_type=F32).astype(o_ref.dtype)


def _norm_matmul(x2d, g, w, out_dtype, name):
    m, d = x2d.shape
    n = w.shape[1]
    tm = _row_tile(m, ROW_TILE)
    tn = _row_tile(n, 1024)
    return pl.pallas_call(
        _norm_matmul_kernel,
        out_shape=jax.ShapeDtypeStruct((m, n), out_dtype),
        grid=(m // tm, n // tn),
        in_specs=[pl.BlockSpec((tm, d), lambda i, j: (i, 0)),
                  pl.BlockSpec((1, d), lambda i, j: (0, 0)),
                  pl.BlockSpec((d, tn), lambda i, j: (0, j))],
        out_specs=pl.BlockSpec((tm, tn), lambda i, j: (i, j)),
        scratch_shapes=[pltpu.VMEM((tm, d), BF16)],
        compiler_params=_cparams("parallel", "arbitrary"),
        name=name,
    )(x2d, g.reshape(1, d), w)


def _norm_kv_kernel(x_ref, g_ref, w_ref, k_ref, v_ref):
    d = k_ref.shape[1]
    h = _rms(x_ref[...], g_ref[...]).astype(BF16)
    k_ref[...] = jnp.dot(h, w_ref[:, :d], preferred_element_type=F32)
    v_ref[...] = jnp.dot(h, w_ref[:, d:], preferred_element_type=F32)


def _norm_kv(x2d, g, w, name):
    m, d = x2d.shape
    tm = _row_tile(m, ROW_TILE)
    out = jax.ShapeDtypeStruct((m, d), F32)
    return pl.pallas_call(
        _norm_kv_kernel,
        out_shape=(out, out),
        grid=(m // tm,),
        in_specs=[pl.BlockSpec((tm, d), lambda i: (i, 0)),
                  pl.BlockSpec((1, d), lambda i: (0, 0)),
                  pl.BlockSpec((d, 2 * d), lambda i: (0, 0))],
        out_specs=(pl.BlockSpec((tm, d), lambda i: (i, 0)),
                   pl.BlockSpec((tm, d), lambda i: (i, 0))),
        compiler_params=_cparams("parallel"),
        name=name,
    )(x2d, g.reshape(1, d), w)


def _matmul_norm_res_kernel(a_ref, w_ref, g_ref, x_ref, o_ref):
    m = jnp.dot(a_ref[...], w_ref[...], preferred_element_type=F32)
    o_ref[...] = x_ref[...] + _rms(m, g_ref[...])


def _matmul_norm_res(a2d, w, g, x2d, name):
    m, k = a2d.shape
    d = w.shape[1]
    tm = _row_tile(m, ROW_TILE)
    return pl.pallas_call(
        _matmul_norm_res_kernel,
        out_shape=jax.ShapeDtypeStruct((m, d), F32),
        grid=(m // tm,),
        in_specs=[pl.BlockSpec((tm, k), lambda i: (i, 0)),
                  pl.BlockSpec((k, d), lambda i: (0, 0)),
                  pl.BlockSpec((1, d), lambda i: (0, 0)),
                  pl.BlockSpec((tm, d), lambda i: (i, 0))],
        out_specs=pl.BlockSpec((tm, d), lambda i: (i, 0)),
        compiler_params=_cparams("parallel"),
        name=name,
    )(a2d, w, g.reshape(1, d), x2d)


def _ple_kernel(x_ref, p_ref, wg_ref, wp_ref, o_ref):
    x = x_ref[...]
    z = jnp.dot(x.astype(BF16), wg_ref[...], preferred_element_type=F32)
    e = jnp.dot(p_ref[...].astype(BF16), wp_ref[...], preferred_element_type=F32)
    o_ref[...] = x + _sigmoid(z) * e


def _ple(x2d, p2d, wg, wp, name):
    m, d = x2d.shape
    pd = p2d.shape[1]
    tm = _row_tile(m, ROW_TILE)
    return pl.pallas_call(
        _ple_kernel,
        out_shape=jax.ShapeDtypeStruct((m, d), F32),
        grid=(m // tm,),
        in_specs=[pl.BlockSpec((tm, d), lambda i: (i, 0)),
                  pl.BlockSpec((tm, pd), lambda i: (i, 0)),
                  pl.BlockSpec((d, d), lambda i: (0, 0)),
                  pl.BlockSpec((pd, d), lambda i: (0, 0))],
        out_specs=pl.BlockSpec((tm, d), lambda i: (i, 0)),
        compiler_params=_cparams("parallel"),
        name=name,
    )(x2d, p2d, wg, wp)


def _ffn_kernel(x_ref, gin_ref, wg_ref, wu_ref, cw_ref, cb_ref, wo_ref, gout_ref,
                cs_ref, o_ref, co_ref, h_ref, acc_ref):
    t = pl.program_id(1)
    j = pl.program_id(2)
    tm = x_ref.shape[1]

    @pl.when(j == 0)
    def _():
        h_ref[...] = _rms(x_ref[0], gin_ref[...]).astype(BF16)
        acc_ref[...] = jnp.zeros_like(acc_ref)

    @pl.when(t == 0)
    def _():
        co_ref[0, j] = cs_ref[0, j]

    h = h_ref[...]
    g = jnp.dot(h, wg_ref[...], preferred_element_type=F32)
    u = jnp.dot(h, wu_ref[...], preferred_element_type=F32)
    prev = co_ref[0, j]
    co_ref[0, j] = g[tm - (CONV_W - 1):, :]
    row = lax.broadcasted_iota(jnp.int32, g.shape, 0)
    g1 = jnp.where(row == 0, prev[1:2, :], pltpu.roll(g, 1, axis=0))
    g2 = jnp.where(row == 0, prev[0:1, :],
                   jnp.where(row == 1, prev[1:2, :], pltpu.roll(g, 2, axis=0)))
    cw = cw_ref[...]
    c = cb_ref[...] + cw[0:1, :] * g2 + cw[1:2, :] * g1 + cw[2:3, :] * g
    act = (c * _sigmoid(c) * u).astype(BF16)
    acc_ref[...] += jnp.dot(act, wo_ref[...], preferred_element_type=F32)

    @pl.when(j == pl.num_programs(2) - 1)
    def _():
        o_ref[0] = x_ref[0] + _rms(acc_ref[...], gout_ref[...])


def _ffn(x, gin, w_in, cw, cb, w_out, gout, conv_state, name):
    b, t, d = x.shape
    f = w_out.shape[0]
    tm = _row_tile(t, FFN_ROW_TILE)
    tf = FFN_COL_TILE
    nf = f // tf
    assert f % tf == 0 and t >= CONV_W - 1
    cs = conv_state.reshape(b, CONV_W - 1, nf, tf).transpose(0, 2, 1, 3)
    cs_spec = pl.BlockSpec((1, nf, CONV_W - 1, tf), lambda bi, ti, j: (bi, 0, 0, 0))
    y, co = pl.pallas_call(
        _ffn_kernel,
        out_shape=(jax.ShapeDtypeStruct((b, t, d), F32),
                   jax.ShapeDtypeStruct(cs.shape, F32)),
        grid=(b, t // tm, nf),
        in_specs=[pl.BlockSpec((1, tm, d), lambda bi, ti, j: (bi, ti, 0)),
                  pl.BlockSpec((1, d), lambda bi, ti, j: (0, 0)),
                  pl.BlockSpec((d, tf), lambda bi, ti, j: (0, j)),
                  pl.BlockSpec((d, tf), lambda bi, ti, j: (0, j + nf)),
                  pl.BlockSpec((CONV_W, tf), lambda bi, ti, j: (0, j)),
                  pl.BlockSpec((1, tf), lambda bi, ti, j: (0, j)),
                  pl.BlockSpec((tf, d), lambda bi, ti, j: (j, 0)),
                  pl.BlockSpec((1, d), lambda bi, ti, j: (0, 0)),
                  cs_spec],
        out_specs=(pl.BlockSpec((1, tm, d), lambda bi, ti, j: (bi, ti, 0)), cs_spec),
        scratch_shapes=[pltpu.VMEM((tm, d), BF16),
                        pltpu.VMEM((tm, d), F32)],
        compiler_params=_cparams("parallel", "arbitrary", "arbitrary"),
        name=name,
    )(x, gin.reshape(1, d), w_in, w_in, cw, cb.reshape(1, f), w_out,
      gout.reshape(1, d), cs)
    return y, co.transpose(0, 2, 1, 3).reshape(b, CONV_W - 1, f)


def _split3(x):
    hi = x.astype(BF16)
    r1 = x - hi.astype(F32)
    mid = r1.astype(BF16)
    lo = (r1 - mid.astype(F32)).astype(BF16)
    return hi, mid, lo


def _gla_levels(tc):
    return [s for s in (16, 32, 64, 128, 256, 512, 1024) if 2 * s <= tc]


def _gla_level_table(tc):
    r = np.arange(tc)[:, None]
    c = np.arange(tc)[None, :]
    table = np.full((tc, tc), -1, np.int32)
    table[(r // GLA_BLOCK == c // GLA_BLOCK) & (c <= r)] = 0
    for i, s in enumerate(_gla_levels(tc)):
        table[(r // (2 * s) == c // (2 * s)) & ((r // s) % 2 == 1) & ((c // s) % 2 == 0)] = i + 1
    return table


def _gla_kernel(q_ref, f_ref, i_ref, g_ref, ll_ref, l1m_ref, om_ref, gn_ref, s0_ref, lvl_ref,
                o_ref, so_ref, st_ref, bg_ref, fac_ref, qi_ref, ks_ref, v_ref, a_ref, oo_ref):
    t = pl.program_id(1)
    tc = q_ref.shape[1]
    d = q_ref.shape[2]
    levels = _gla_levels(tc)

    @pl.when(t == 0)
    def _():
        for h in range(N_HEADS):
            st_ref[h] = s0_ref[0, h].T

    qh = q_ref[0]
    fp = f_ref[0]
    q = qh * _sigmoid(qh)
    e = jnp.exp(-jnp.abs(fp))
    logsig = jnp.minimum(fp, 0.0) - jnp.log1p(e)
    a = ll_ref[...]
    c = l1m_ref[...] + logsig
    logf = jnp.maximum(a, c) + jnp.log1p(jnp.exp(-jnp.abs(a - c)))
    kk = om_ref[...] * (jnp.where(fp >= 0.0, e, 1.0) / (1.0 + e))

    row = lax.broadcasted_iota(jnp.int32, (tc, tc), 0)
    col = lax.broadcasted_iota(jnp.int32, (tc, tc), 1)
    m_incl = jnp.where(col <= row, 1.0, 0.0).astype(BF16)
    bg = sum(jnp.dot(m_incl, p, preferred_element_type=F32) for p in _split3(logf))
    bg_ref[...] = bg
    b_last = bg[tc - 1:tc, :]

    qi_ref[...] = (q * jnp.exp(bg)).astype(BF16)
    ks_ref[...] = (kk * jnp.exp(b_last - bg)).astype(BF16)
    v_ref[...] = i_ref[0].astype(BF16)

    def ref_rows(span, offset):
        pieces = []
        for a0 in range(0, tc, span):
            r = a0 + offset
            src = bg_ref[r:r + 1, :] if r >= 0 else jnp.zeros((1, d), F32)
            pieces.append(jnp.broadcast_to(src, (span, d)))
        return pieces[0] if len(pieces) == 1 else jnp.concatenate(pieces, axis=0)

    dl = bg - ref_rows(GLA_BLOCK, -1)
    fac_ref[0] = (q * jnp.exp(dl)).astype(BF16)
    fac_ref[1] = (kk * jnp.exp(-dl)).astype(BF16)
    rid = lax.broadcasted_iota(jnp.int32, (tc, 1), 0)
    for li, s in enumerate(levels):
        mid = ref_rows(2 * s, s - 1)
        right = ((rid // s) % 2) == 1
        x = jnp.exp(jnp.where(right, bg - mid, mid - bg))
        fac_ref[2 + 2 * li] = (q * x).astype(BF16)
        fac_ref[3 + 2 * li] = (kk * x).astype(BF16)

    nt = (((1,), (1,)), ((), ()))
    tn = (((0,), (0,)), ((), ()))
    lvl = lvl_ref[...]
    for h in range(N_HEADS):
        cols = slice(h * HEAD_W, (h + 1) * HEAD_W)
        att = jnp.zeros((tc, tc), F32)
        for ci in range(len(levels), -1, -1):
            prod = lax.dot_general(fac_ref[2 * ci, :, cols], fac_ref[2 * ci + 1, :, cols], nt,
                                   preferred_element_type=F32)
            att = jnp.where(lvl == ci, prod, att)
        a_ref[h] = att.astype(BF16)

    for h in range(N_HEADS):
        cols = slice(h * HEAD_W, (h + 1) * HEAD_W)
        vb = v_ref[:, cols]
        s_t = st_ref[h]
        o_inter = lax.dot_general(qi_ref[:, cols], s_t.astype(BF16), nt, preferred_element_type=F32)
        oo_ref[:, cols] = o_inter + jnp.dot(a_ref[h], vb, preferred_element_type=F32)
        upd = lax.dot_general(vb, ks_ref[:, cols], tn, preferred_element_type=F32)
        st_ref[h] = s_t * jnp.exp(b_last[:, cols]) + upd

    gh = g_ref[0]
    gate = gh * _sigmoid(gh)
    for h in range(N_HEADS):
        cols = slice(h * HEAD_W, (h + 1) * HEAD_W)
        oh = _rms(oo_ref[:, cols], gn_ref[:, cols])
        o_ref[0, :, cols] = (oh * gate[:, cols]).astype(o_ref.dtype)

    @pl.when(t == pl.num_programs(1) - 1)
    def _():
        for h in range(N_HEADS):
            so_ref[0, h] = st_ref[h].T


def _gla(proj, log_lower, log1m_lower, one_m_lower, g_out, s0, name):
    b, t, _ = proj.shape
    d = D_MODEL
    tc = _row_tile(t, GLA_TILE)
    assert tc % GLA_BLOCK == 0
    vec = pl.BlockSpec((1, d), lambda bi, ti: (0, 0))
    col_spec = lambda c: pl.BlockSpec((1, tc, d), lambda bi, ti: (bi, ti, c))
    st_spec = pl.BlockSpec((1, N_HEADS, HEAD_W, HEAD_W), lambda bi, ti: (bi, 0, 0, 0))
    n_fac = 2 * (1 + len(_gla_levels(tc)))
    return pl.pallas_call(
        _gla_kernel,
        out_shape=(jax.ShapeDtypeStruct((b, t, d), BF16),
                   jax.ShapeDtypeStruct((b, N_HEADS, HEAD_W, HEAD_W), F32)),
        grid=(b, t // tc),
        in_specs=[col_spec(0), col_spec(1), col_spec(2), col_spec(3),
                  vec, vec, vec, vec, st_spec,
                  pl.BlockSpec((tc, tc), lambda bi, ti: (0, 0))],
        out_specs=(pl.BlockSpec((1, tc, d), lambda bi, ti: (bi, ti, 0)), st_spec),
        scratch_shapes=[pltpu.VMEM((N_HEADS, HEAD_W, HEAD_W), F32),
                        pltpu.VMEM((tc, d), F32),
                        pltpu.VMEM((n_fac, tc, d), BF16),
                        pltpu.VMEM((tc, d), BF16), pltpu.VMEM((tc, d), BF16),
                        pltpu.VMEM((tc, d), BF16),
                        pltpu.VMEM((N_HEADS, tc, tc), BF16),
                        pltpu.VMEM((tc, d), F32)],
        compiler_params=_cparams("parallel", "arbitrary"),
        name=name,
    )(proj, proj, proj, proj, log_lower.reshape(1, d), log1m_lower.reshape(1, d),
      one_m_lower.reshape(1, d), g_out.reshape(1, d), s0, jnp.asarray(_gla_level_table(tc)))


def _t5_bucket(rel):
    half = NUM_BUCKETS // 2
    ret = jnp.where(rel > 0, half, 0)
    n = jnp.abs(rel)
    max_exact = half // 2
    nf = jnp.maximum(n, 1).astype(F32)
    large = max_exact + (jnp.log(nf / max_exact) / math.log(MAX_DISTANCE / max_exact)
                         * (half - max_exact)).astype(jnp.int32)
    large = jnp.minimum(large, half - 1)
    return ret + jnp.where(n < max_exact, n, large)


def _toeplitz_kernel(w_ref, o_ref, *, tq):
    tk = o_ref.shape[2]
    p = w_ref.shape[3]
    x = jnp.broadcast_to(w_ref[0, 0], (tk, p))
    o_ref[0, 0] = pltpu.roll(x, 0, axis=1, stride=1, stride_axis=0)[:, :tq]


def _bias_tiles_t(rel_bias, d0_list, tq, tk):
    p = pl.cdiv(tq + tk, HEAD_W) * HEAD_W
    m = jnp.arange(p, dtype=jnp.int32)
    rel = jnp.asarray(d0_list, jnp.int32)[:, None] + jnp.where(m <= tq, -m, p - m)[None, :]
    onehot = (_t5_bucket(rel)[:, :, None] == jnp.arange(NUM_BUCKETS, dtype=jnp.int32)).astype(F32)
    w = jnp.einsum("dpn,nh->hdp", onehot, rel_bias.astype(F32),
                   precision=lax.Precision.HIGHEST)
    h, nd = w.shape[0], w.shape[1]
    return pl.pallas_call(
        functools.partial(_toeplitz_kernel, tq=tq),
        out_shape=jax.ShapeDtypeStruct((h, nd, tk, tq), F32),
        grid=(h, nd),
        in_specs=[pl.BlockSpec((1, 1, 1, p), lambda hi, di: (hi, di, 0, 0))],
        out_specs=pl.BlockSpec((1, 1, tk, tq), lambda hi, di: (hi, di, 0, 0)),
        compiler_params=_cparams("parallel", "parallel"),
        name="bias_tiles",
    )(w.reshape(h, nd, 1, p))


def _attn_kernel(*refs, segs, q_off, tq, hb):
    nseg = len(segs)
    q_ref, lam_ref, gn_ref = refs[0], refs[1], refs[2]
    seg_refs = [refs[3 + 3 * s: 6 + 3 * s] for s in range(nseg)]
    o_ref = refs[3 + 3 * nseg]
    m_ref, l_ref, acc_ref = refs[4 + 3 * nseg: 7 + 3 * nseg]
    kv_scratch = refs[7 + 3 * nseg:]
    seg_scr = [kv_scratch[4 * s: 4 * s + 4] for s in range(nseg)]

    i = pl.program_id(2)
    q0 = q_off + i * tq

    @pl.when(i == 0)
    def _():
        for (k_ref, v_ref, _), (kb_ref, vt_ref, _, _), (_, seg_len, tk, _, _) in zip(seg_refs, seg_scr, segs):
            kb_ref[...] = k_ref[0].astype(BF16)
            for kt in range(seg_len // tk):
                vt_ref[kt] = v_ref[0, kt * tk:(kt + 1) * tk, :].astype(F32).T.astype(BF16)

    q = q_ref[0]
    lane = lax.broadcasted_iota(jnp.int32, (tq, HEAD_W), 1)
    qz = []
    for hh in range(hb):
        qh = q[:, hh * HEAD_W:(hh + 1) * HEAD_W] * (DA_HEAD_DIM ** -0.5)
        qz.append([jnp.where(lane < DA_HEAD_DIM, qh, 0).astype(BF16),
                   jnp.where(lane >= DA_HEAD_DIM, qh, 0).astype(BF16)])
    m_ref[...] = jnp.full_like(m_ref, NEG_BIG)
    l_ref[...] = jnp.zeros_like(l_ref)
    acc_ref[...] = jnp.zeros_like(acc_ref)
    q_chunk = (q0 + lax.broadcasted_iota(jnp.int32, (1, tq), 1)) // CHUNK
    vis_end = ((q0 + tq - 1) // CHUNK + 1) * CHUNK
    full_end = (q0 // CHUNK + 1) * CHUNK
    nt = (((1,), (1,)), ((), ()))

    for (_, _, b_ref), (kb_ref, vt_ref, sa_ref, sb_ref), (pos0, seg_len, tk, d0_min, d0_step) in zip(
            seg_refs, seg_scr, segs):
        n_tiles = seg_len // tk
        n_vis = jnp.clip((vis_end - pos0 + tk - 1) // tk, 0, n_tiles)
        n_full = jnp.clip((full_end - pos0) // tk, 0, n_vis)

        def scores_into(kt, s_ref, kb_ref=kb_ref, tk=tk, n_tiles=n_tiles):
            r0 = pl.multiple_of(jnp.minimum(kt, n_tiles - 1) * tk, tk)
            for hh in range(hb):
                kb = kb_ref[pl.ds(r0, tk), hh * HEAD_W:(hh + 1) * HEAD_W]
                for c in range(2):
                    s_ref[2 * hh + c] = lax.dot_general(kb, qz[hh][c], nt, preferred_element_type=F32)

        def consume(kt, s_ref, masked, b_ref=b_ref, vt_ref=vt_ref, pos0=pos0, tk=tk,
                    d0_min=d0_min, d0_step=d0_step, n_tiles=n_tiles, n_vis=n_vis):
            ktc = jnp.minimum(kt, n_tiles - 1)
            r0 = ktc * tk
            d_idx = (pos0 + r0 - q0 - d0_min) // d0_step
            if masked:
                k0 = jnp.where(kt < n_vis, pos0 + r0, jnp.int32(1 << 28))
                k_chunk = (k0 + lax.broadcasted_iota(jnp.int32, (tk, 1), 0)) // CHUNK
                visible = k_chunk <= q_chunk
            for hh in range(hb):
                vt = vt_ref[ktc, hh * HEAD_W:(hh + 1) * HEAD_W, :]
                bias = b_ref[hh, d_idx]
                for c in range(2):
                    j = 2 * hh + c
                    s = s_ref[j] + bias
                    if masked:
                        s = jnp.where(visible, s, NEG_BIG)
                    m_prev = m_ref[j]
                    m_new = jnp.maximum(m_prev, jnp.max(s, axis=0, keepdims=True))
                    p = jnp.exp(s - m_new)
                    alpha = jnp.exp(m_prev - m_new)
                    l_ref[j] = alpha * l_ref[j] + jnp.sum(p, axis=0, keepdims=True)
                    acc_ref[j] = alpha * acc_ref[j] + jnp.dot(vt, p.astype(BF16),
                                                              preferred_element_type=F32)
                    m_ref[j] = m_new

        def pair(u, carry, masked, base, sa_ref=sa_ref, sb_ref=sb_ref):
            t0 = base + 2 * u
            scores_into(t0 + 1, sb_ref)
            consume(t0, sa_ref, masked)
            scores_into(t0 + 2, sa_ref)
            consume(t0 + 1, sb_ref, masked)
            return carry

        n_pairs_full = n_full // 2
        base = 2 * n_pairs_full
        scores_into(0, sa_ref)
        lax.fori_loop(0, n_pairs_full, functools.partial(pair, masked=False, base=0), 0)
        lax.fori_loop(0, (n_vis - base + 1) // 2, functools.partial(pair, masked=True, base=base), 0)

    lam = lam_ref[0:1, 0:1]
    for hh in range(hb):
        o_t = (acc_ref[2 * hh] * (1.0 / l_ref[2 * hh])
               - lam * (acc_ref[2 * hh + 1] * (1.0 / l_ref[2 * hh + 1])))
        cols = slice(hh * HEAD_W, (hh + 1) * HEAD_W)
        o_ref[0, :, cols] = (_rms(o_t.T, gn_ref[:, cols]) * lam_ref[0:1, 1:2]).astype(o_ref.dtype)


def _attention(q, segments, rel_bias, lam, out_scale, g_subln, q_off, name):
    b, t_real, d = q.shape
    tq_all = max(t_real, HEAD_W)
    if tq_all != t_real:
        q = jnp.pad(q, ((0, 0), (0, tq_all - t_real), (0, 0)))
    tq = _row_tile(tq_all, ATTN_TILE)
    nq = tq_all // tq
    hb = ATTN_HEADS_PER_STEP
    hw = hb * HEAD_W
    segs, args, in_specs, scratch = [], [], [], []
    for (k, v, pos0, tk) in segments:
        seg_len = k.shape[1]
        assert seg_len % tk == 0
        d0s = sorted({pos0 + kt * tk - (q_off + i * tq)
                      for i in range(nq) for kt in range(seg_len // tk)
                      if pos0 + kt * tk < ((q_off + (i + 1) * tq - 1) // CHUNK + 1) * CHUNK})
        step = math.gcd(tq, tk)
        d0s = list(range(d0s[0], d0s[-1] + 1, step))
        tiles = _bias_tiles_t(rel_bias, d0s, tq, tk)
        segs.append((pos0, seg_len, tk, d0s[0], step))
        args += [k, v, tiles]
        in_specs += [pl.BlockSpec((1, seg_len, hw), lambda bi, h, i: (bi, 0, h)),
                     pl.BlockSpec((1, seg_len, hw), lambda bi, h, i: (bi, 0, h)),
                     pl.BlockSpec((hb, len(d0s), tk, tq), lambda bi, h, i: (h, 0, 0, 0))]
        scratch += [pltpu.VMEM((seg_len, hw), BF16), pltpu.VMEM((seg_len // tk, hw, tk), BF16),
                    pltpu.VMEM((2 * hb, tk, tq), F32), pltpu.VMEM((2 * hb, tk, tq), F32)]
    scal = jnp.zeros((1, HEAD_W), F32).at[0, 0].set(lam).at[0, 1].set(out_scale)
    kern = functools.partial(_attn_kernel, segs=tuple(segs), q_off=q_off, tq=tq, hb=hb)
    out = pl.pallas_call(
        kern,
        out_shape=jax.ShapeDtypeStruct((b, tq_all, d), BF16),
        grid=(b, N_HEADS // hb, nq),
        in_specs=[pl.BlockSpec((1, tq, hw), lambda bi, h, i: (bi, i, h)),
                  pl.BlockSpec((1, HEAD_W), lambda bi, h, i: (0, 0)),
                  pl.BlockSpec((1, hw), lambda bi, h, i: (0, h))] + in_specs,
        out_specs=pl.BlockSpec((1, tq, hw), lambda bi, h, i: (bi, i, h)),
        scratch_shapes=[pltpu.VMEM((2 * hb, 1, tq), F32), pltpu.VMEM((2 * hb, 1, tq), F32),
                        pltpu.VMEM((2 * hb, HEAD_W, tq), F32)] + scratch,
        compiler_params=_cparams("parallel", "parallel", "arbitrary"),
        name=name,
    )(q, scal, g_subln.reshape(1, d), *args)
    return out[:, :t_real]


def _trunk(x, ple, q_off, past_k, past_v, hg_s0, conv_s0, wts, tag):
    (g_norms, w_in_a, lower, g_hg, w_out_a, g_kv, w_kv, rel_bias, w_q_b, lam_b,
     g_subln, w_out_b, w_ffn_in, conv_w, conv_b, w_ffn_out, w_ple, w_ple_gate) = wts
    b, t, d = x.shape
    depth = g_norms.shape[0]
    n_a = w_in_a.shape[0]
    m = b * t
    hg_out, conv_out = [], []
    k_new = v_new = None
    segments = None
    for i in range(depth):
        nm = f"{tag}{i}"
        x2 = x.reshape(m, d)
        if i < n_a:
            lw = lower[i]
            proj = _norm_matmul(x2, g_norms[i, 0], w_in_a[i], F32, nm + "_hg_in")
            o, s = _gla(proj.reshape(b, t, 4 * d), jnp.log(lw), jnp.log1p(-lw), 1.0 - lw,
                        g_hg[i], hg_s0[i], nm + "_gla")
            hg_out.append(s)
            x2 = _matmul_norm_res(o.reshape(m, d), w_out_a[i], g_norms[i, 1], x2, nm + "_hg_out")
        else:
            j = i - n_a
            q = _norm_matmul(x2, g_norms[i, 0], w_q_b[j], BF16, nm + "_q")
            lam_init = 0.8 - 0.6 * math.exp(-0.3 * i)
            lp = lam_b[j].astype(F32)
            lam = jnp.exp(jnp.sum(lp[0] * lp[1])) - jnp.exp(jnp.sum(lp[2] * lp[3])) + lam_init
            o = _attention(q.reshape(b, t, d), segments, rel_bias, lam, 1.0 - lam_init,
                           g_subln[j], q_off, nm + "_attn")
            x2 = _matmul_norm_res(o.reshape(m, d), w_out_b[j], g_norms[i, 1], x2, nm + "_attn_out")
        x3, cbuf = _ffn(x2.reshape(b, t, d), g_norms[i, 2], w_ffn_in[i], conv_w[i], conv_b[i],
                        w_ffn_out[i], g_norms[i, 3], conv_s0[i], nm + "_ffn")
        conv_out.append(cbuf)
        x2 = _ple(x3.reshape(m, d), ple[i].reshape(m, -1), w_ple_gate[i], w_ple[i], nm + "_ple")
        x = x2.reshape(b, t, d)
        if i == n_a - 1:
            k_new, v_new = _norm_kv(x2, g_kv, w_kv, nm + "_kv")
            k_new = k_new.reshape(b, t, d)
            v_new = v_new.reshape(b, t, d)
            segments = []
            if past_k is not None:
                tp = past_k.shape[1]
                segments.append((past_k.reshape(b, tp, d), past_v.reshape(b, tp, d), 0, tp))
            segments.append((k_new, v_new, q_off, min(t, ATTN_KEY_TILE)))
    hd = (b, t, N_HEADS, HEAD_W)
    return x, k_new.reshape(hd), v_new.reshape(hd), jnp.stack(hg_out), jnp.stack(conv_out)


def kernel(x_prompt, x_sample, cache_k, cache_v, state_hgrn, state_conv, p_prompt, p_sample,
           g_norms, w_in_a, lb_raw, g_hg, w_out_a, g_kv, w_kv, rel_bias,
           w_q_b, lam_b, g_subln, w_out_b, w_ffn_in, conv_w, conv_b, w_ffn_out,
           w_ple, w_ple_gate):
    sm = jax.nn.softmax(lb_raw.astype(F32), axis=0)
    cs = jnp.cumsum(sm, axis=0)
    lower = cs - cs[0:1]
    bf = lambda w: w.astype(BF16)
    wts = (g_norms, bf(w_in_a), lower, g_hg, bf(w_out_a), g_kv, bf(w_kv), rel_bias,
           bf(w_q_b), lam_b, g_subln, bf(w_out_b), bf(w_ffn_in), conv_w, conv_b,
           bf(w_ffn_out), bf(w_ple), bf(w_ple_gate))
    bp, tp, _ = x_prompt.shape
    n_a, depth = w_in_a.shape[0], g_norms.shape[0]
    hg0 = jnp.zeros((n_a, bp) + state_hgrn.shape[2:], F32)
    cv0 = jnp.zeros((depth, bp) + state_conv.shape[2:], F32)
    y_p, k_p, v_p, hg_p, cv_p = _trunk(x_prompt, p_prompt, 0, None, None, hg0, cv0, wts, "p")
    y_s, k_s, v_s, hg_s, cv_s = _trunk(x_sample, p_sample, cache_k.shape[1], cache_k, cache_v,
                                       state_hgrn, state_conv, wts, "s")
    return (y_p, y_s, k_p, v_p, k_s, v_s, hg_p, hg_s, cv_p, cv_s)
```

```python
import functools
import math

import numpy as np
import jax
import jax.numpy as jnp
from jax import lax
from jax.experimental import pallas as pl
from jax.experimental.pallas import tpu as pltpu

F32 = jnp.float32
BF16 = jnp.bfloat16

D_MODEL = 1024
N_HEADS = 8
HEAD_W = D_MODEL // N_HEADS
DA_HEAD_DIM = HEAD_W // 2
CHUNK = 64
GLA_BLOCK = 16
NUM_BUCKETS = 32
MAX_DISTANCE = 256
D_FF = 2816
CONV_W = 3
EPS = 1e-6
NEG_BIG = -1e30

V7X_VMEM_LIMIT_BYTES = 56 * 1024 * 1024

ROW_TILE = 512
FFN_ROW_TILE = 1024
FFN_COL_TILE = 256
FFN_CONV_ROWS = 128
GLA_TILE = 256
ATTN_TILE = 512
ATTN_KEY_TILE = 256
ATTN_HEADS_PER_STEP = 2


def _cparams(*sem):
    return pltpu.CompilerParams(dimension_semantics=sem,
                                vmem_limit_bytes=V7X_VMEM_LIMIT_BYTES)


def _rms(x, g):
    ms = jnp.mean(x * x, axis=-1, keepdims=True)
    return x * lax.rsqrt(ms + EPS) * g


def _sigmoid(z):
    return 1.0 / (1.0 + jnp.exp(-z))


def _row_tile(m, cap):
    t = min(m, cap)
    assert m % t == 0, (m, t)
    return t


def _norm_matmul_kernel(x_ref, g_ref, w_ref, o_ref):
    h = _rms(x_ref[...], g_ref[...]).astype(BF16)
    n = w_ref.shape[1]
    tn = min(n, D_MODEL)
    for n0 in range(0, n, tn):
        o_ref[:, n0:n0 + tn] = jnp.dot(h, w_ref[:, n0:n0 + tn],
                                       preferred_element_type=F32).astype(o_ref.dtype)


def _norm_matmul(x2d, g, w, out_dtype, name):
    m, d = x2d.shape
    n = w.shape[1]
    tm = _row_tile(m, ROW_TILE)
    return pl.pallas_call(
        _norm_matmul_kernel,
        out_shape=jax.ShapeDtypeStruct((m, n), out_dtype),
        grid=(m // tm,),
        in_specs=[pl.BlockSpec((tm, d), lambda i: (i, 0)),
                  pl.BlockSpec((1, d), lambda i: (0, 0)),
                  pl.BlockSpec((d, n), lambda i: (0, 0))],
        out_specs=pl.BlockSpec((tm, n), lambda i: (i, 0)),
        compiler_params=_cparams("parallel"),
        name=name,
    )(x2d, g.reshape(1, d), w)


def _norm_kv_kernel(x_ref, g_ref, w_ref, k_ref, v_ref):
    d = k_ref.shape[1]
    h = _rms(x_ref[...], g_ref[...]).astype(BF16)
    k_ref[...] = jnp.dot(h, w_ref[:, :d], preferred_element_type=F32)
    v_ref[...] = jnp.dot(h, w_ref[:, d:], preferred_element_type=F32)


def _norm_kv(x2d, g, w, name):
    m, d = x2d.shape
    tm = _row_tile(m, ROW_TILE)
    out = jax.ShapeDtypeStruct((m, d), F32)
    return pl.pallas_call(
        _norm_kv_kernel,
        out_shape=(out, out),
        grid=(m // tm,),
        in_specs=[pl.BlockSpec((tm, d), lambda i: (i, 0)),
                  pl.BlockSpec((1, d), lambda i: (0, 0)),
                  pl.BlockSpec((d, 2 * d), lambda i: (0, 0))],
        out_specs=(pl.BlockSpec((tm, d), lambda i: (i, 0)),
                   pl.BlockSpec((tm, d), lambda i: (i, 0))),
        compiler_params=_cparams("parallel"),
        name=name,
    )(x2d, g.reshape(1, d), w)


def _matmul_norm_res_kernel(a_ref, w_ref, g_ref, x_ref, o_ref):
    m = jnp.dot(a_ref[...], w_ref[...], preferred_element_type=F32)
    o_ref[...] = x_ref[...] + _rms(m, g_ref[...])


def _matmul_norm_res(a2d, w, g, x2d, name):
    m, k = a2d.shape
    d = w.shape[1]
    tm = _row_tile(m, ROW_TILE)
    return pl.pallas_call(
        _matmul_norm_res_kernel,
        out_shape=jax.ShapeDtypeStruct((m, d), F32),
        grid=(m // tm,),
        in_specs=[pl.BlockSpec((tm, k), lambda i: (i, 0)),
                  pl.BlockSpec((k, d), lambda i: (0, 0)),
                  pl.BlockSpec((1, d), lambda i: (0, 0)),
                  pl.BlockSpec((tm, d), lambda i: (i, 0))],
        out_specs=pl.BlockSpec((tm, d), lambda i: (i, 0)),
        compiler_params=_cparams("parallel"),
        name=name,
    )(a2d, w, g.reshape(1, d), x2d)


def _ple_kernel(x_ref, p_ref, wg_ref, wp_ref, o_ref):
    x = x_ref[...]
    z = jnp.dot(x.astype(BF16), wg_ref[...], preferred_element_type=F32)
    e = jnp.dot(p_ref[...].astype(BF16), wp_ref[...], preferred_element_type=F32)
    o_ref[...] = x + _sigmoid(z) * e


def _ple(x2d, p2d, wg, wp, name):
    m, d = x2d.shape
    pd = p2d.shape[1]
    tm = _row_tile(m, ROW_TILE)
    return pl.pallas_call(
        _ple_kernel,
        out_shape=jax.ShapeDtypeStruct((m, d), F32),
        grid=(m // tm,),
        in_specs=[pl.BlockSpec((tm, d), lambda i: (i, 0)),
                  pl.BlockSpec((tm, pd), lambda i: (i, 0)),
                  pl.BlockSpec((d, d), lambda i: (0, 0)),
                  pl.BlockSpec((pd, d), lambda i: (0, 0))],
        out_specs=pl.BlockSpec((tm, d), lambda i: (i, 0)),
        compiler_params=_cparams("parallel"),
        name=name,
    )(x2d, p2d, wg, wp)


def _ffn_kernel(x_ref, gin_ref, wg_ref, wu_ref, cw_ref, cb_ref, wo_ref, gout_ref, cs_ref,
                o_ref, co_ref, h_ref, acc_ref, g0_ref, u0_ref, g1_ref, u1_ref, a0_ref, a1_ref, *, nf):
    t = pl.program_id(1)
    s = pl.program_id(2)
    tm = x_ref.shape[1]

    @pl.when(s == 0)
    def _():
        h_ref[...] = _rms(x_ref[0], gin_ref[...]).astype(BF16)
        acc_ref[...] = jnp.zeros_like(acc_ref)
        g1_ref[...] = jnp.zeros_like(g1_ref)
        u1_ref[...] = jnp.zeros_like(u1_ref)
        a0_ref[...] = jnp.zeros_like(a0_ref)

    @pl.when((s == 0) & (t == 0))
    def _():
        co_ref[0] = cs_ref[0]

    def step(ga_ref, ua_ref, gb_ref, ub_ref, ab_ref, ac_ref):
        valid = (s >= 1) & (s <= nf)
        jb = jnp.clip(s - 1, 0, nf - 1)
        prev = co_ref[0, jb]
        co_ref[0, jb] = jnp.where(valid, gb_ref[tm - (CONV_W - 1):, :], prev)
        cw = cw_ref[...]
        cb = cb_ref[...]
        r8 = lax.broadcasted_iota(jnp.int32, (8, prev.shape[1]), 0)
        head = jnp.where(r8 == 6, prev[0:1, :], prev[1:2, :])
        rc = min(tm, FFN_CONV_ROWS)
        for r0 in range(0, tm, rc):
            if r0 > 0:
                head = gb_ref[r0 - 8:r0, :]
            gc = gb_ref[r0:r0 + rc, :]
            ext = jnp.concatenate([head, gc], axis=0)
            g1 = pltpu.roll(ext, 1, axis=0)[8:, :]
            g2 = pltpu.roll(ext, 2, axis=0)[8:, :]
            c = cb + cw[0:1, :] * g2 + cw[1:2, :] * g1 + cw[2:3, :] * gc
            ab_ref[r0:r0 + rc, :] = (c * _sigmoid(c) * ub_ref[r0:r0 + rc, :]).astype(BF16)
        h = h_ref[...]
        ga_ref[...] = jnp.dot(h, wg_ref[...], preferred_element_type=F32)
        ua_ref[...] = jnp.dot(h, wu_ref[...], preferred_element_type=F32)
        acc_ref[...] += jnp.dot(ac_ref[...], wo_ref[...], preferred_element_type=F32)

    @pl.when(s % 2 == 0)
    def _():
        step(g0_ref, u0_ref, g1_ref, u1_ref, a1_ref, a0_ref)

    @pl.when(s % 2 == 1)
    def _():
        step(g1_ref, u1_ref, g0_ref, u0_ref, a0_ref, a1_ref)

    @pl.when(s == pl.num_programs(2) - 1)
    def _():
        o_ref[0] = x_ref[0] + _rms(acc_ref[...], gout_ref[...])


def _ffn(x, gin, w_in, cw, cb, w_out, gout, conv_state, name):
    b, t, d = x.shape
    f = w_out.shape[0]
    tm = _row_tile(t, FFN_ROW_TILE)
    tf = FFN_COL_TILE
    nf = f // tf
    assert f % tf == 0 and t >= CONV_W - 1 and tm % 8 == 0
    cs = conv_state.reshape(b, CONV_W - 1, nf, tf).transpose(0, 2, 1, 3)
    cs_spec = pl.BlockSpec((1, nf, CONV_W - 1, tf), lambda bi, ti, s: (bi, 0, 0, 0))
    col_a = lambda s: jnp.minimum(s, nf - 1)
    col_b = lambda s: jnp.clip(s - 1, 0, nf - 1)
    col_c = lambda s: jnp.clip(s - 2, 0, nf - 1)
    y, co = pl.pallas_call(
        functools.partial(_ffn_kernel, nf=nf),
        out_shape=(jax.ShapeDtypeStruct((b, t, d), F32),
                   jax.ShapeDtypeStruct(cs.shape, F32)),
        grid=(b, t // tm, nf + 2),
        in_specs=[pl.BlockSpec((1, tm, d), lambda bi, ti, s: (bi, ti, 0)),
                  pl.BlockSpec((1, d), lambda bi, ti, s: (0, 0)),
                  pl.BlockSpec((d, tf), lambda bi, ti, s: (0, col_a(s))),
                  pl.BlockSpec((d, tf), lambda bi, ti, s: (0, col_a(s) + nf)),
                  pl.BlockSpec((CONV_W, tf), lambda bi, ti, s: (0, col_b(s))),
                  pl.BlockSpec((1, tf), lambda bi, ti, s: (0, col_b(s))),
                  pl.BlockSpec((tf, d), lambda bi, ti, s: (col_c(s), 0)),
                  pl.BlockSpec((1, d), lambda bi, ti, s: (0, 0)),
                  cs_spec],
        out_specs=(pl.BlockSpec((1, tm, d), lambda bi, ti, s: (bi, ti, 0)), cs_spec),
        scratch_shapes=[pltpu.VMEM((tm, d), BF16),
                        pltpu.VMEM((tm, d), F32),
                        pltpu.VMEM((tm, tf), F32), pltpu.VMEM((tm, tf), F32),
                        pltpu.VMEM((tm, tf), F32), pltpu.VMEM((tm, tf), F32),
                        pltpu.VMEM((tm, tf), BF16), pltpu.VMEM((tm, tf), BF16)],
        compiler_params=_cparams("parallel", "arbitrary", "arbitrary"),
        name=name,
    )(x, gin.reshape(1, d), w_in, w_in, cw, cb.reshape(1, f), w_out,
      gout.reshape(1, d), cs)
    return y, co.transpose(0, 2, 1, 3).reshape(b, CONV_W - 1, f)


def _split3(x):
    hi = x.astype(BF16)
    r1 = x - hi.astype(F32)
    mid = r1.astype(BF16)
    lo = (r1 - mid.astype(F32)).astype(BF16)
    return hi, mid, lo


def _gla_levels(tc):
    return [s for s in (16, 32, 64, 128, 256, 512, 1024) if 2 * s <= tc]


def _gla_level_table(tc):
    r = np.arange(tc)[:, None]
    c = np.arange(tc)[None, :]
    table = np.full((tc, tc), -1, np.int32)
    table[(r // GLA_BLOCK == c // GLA_BLOCK) & (c <= r)] = 0
    for i, s in enumerate(_gla_levels(tc)):
        table[(r // (2 * s) == c // (2 * s)) & ((r // s) % 2 == 1) & ((c // s) % 2 == 0)] = i + 1
    return table


def _gla_kernel(q_ref, f_ref, i_ref, g_ref, ll_ref, l1m_ref, om_ref, gn_ref, s0_ref, lvl_ref,
                o_ref, so_ref, st_ref, bg_ref, fac_ref, qi_ref, ks_ref, v_ref, a_ref, oo_ref):
    t = pl.program_id(1)
    tc = q_ref.shape[1]
    d = q_ref.shape[2]
    levels = _gla_levels(tc)

    @pl.when(t == 0)
    def _():
        for h in range(N_HEADS):
            st_ref[h] = s0_ref[0, h].T

    qh = q_ref[0]
    fp = f_ref[0]
    q = qh * _sigmoid(qh)
    e = jnp.exp(-jnp.abs(fp))
    logsig = jnp.minimum(fp, 0.0) - jnp.log1p(e)
    a = ll_ref[...]
    c = l1m_ref[...] + logsig
    logf = jnp.maximum(a, c) + jnp.log1p(jnp.exp(-jnp.abs(a - c)))
    kk = om_ref[...] * (jnp.where(fp >= 0.0, e, 1.0) / (1.0 + e))

    row = lax.broadcasted_iota(jnp.int32, (tc, tc), 0)
    col = lax.broadcasted_iota(jnp.int32, (tc, tc), 1)
    m_incl = jnp.where(col <= row, 1.0, 0.0).astype(BF16)
    bg = sum(jnp.dot(m_incl, p, preferred_element_type=F32) for p in _split3(logf))
    bg_ref[...] = bg
    b_last = bg[tc - 1:tc, :]

    qi_ref[...] = (q * jnp.exp(bg)).astype(BF16)
    ks_ref[...] = (kk * jnp.exp(b_last - bg)).astype(BF16)
    v_ref[...] = i_ref[0].astype(BF16)

    def ref_rows(span, offset):
        pieces = []
        for a0 in range(0, tc, span):
            r = a0 + offset
            src = bg_ref[r:r + 1, :] if r >= 0 else jnp.zeros((1, d), F32)
            pieces.append(jnp.broadcast_to(src, (span, d)))
        return pieces[0] if len(pieces) == 1 else jnp.concatenate(pieces, axis=0)

    dl = bg - ref_rows(GLA_BLOCK, -1)
    fac_ref[0] = (q * jnp.exp(dl)).astype(BF16)
    fac_ref[1] = (kk * jnp.exp(-dl)).astype(BF16)
    rid = lax.broadcasted_iota(jnp.int32, (tc, 1), 0)
    for li, s in enumerate(levels):
        mid = ref_rows(2 * s, s - 1)
        right = ((rid // s) % 2) == 1
        x = jnp.exp(jnp.where(right, bg - mid, mid - bg))
        fac_ref[2 + 2 * li] = (q * x).astype(BF16)
        fac_ref[3 + 2 * li] = (kk * x).astype(BF16)

    nt = (((1,), (1,)), ((), ()))
    tn = (((0,), (0,)), ((), ()))
    lvl = lvl_ref[...]
    for h in range(N_HEADS):
        cols = slice(h * HEAD_W, (h + 1) * HEAD_W)
        att = jnp.zeros((tc, tc), F32)
        for ci in range(len(levels), -1, -1):
            prod = lax.dot_general(fac_ref[2 * ci, :, cols], fac_ref[2 * ci + 1, :, cols], nt,
                                   preferred_element_type=F32)
            att = jnp.where(lvl == ci, prod, att)
        a_ref[h] = att.astype(BF16)

    for h in range(N_HEADS):
        cols = slice(h * HEAD_W, (h + 1) * HEAD_W)
        vb = v_ref[:, cols]
        s_t = st_ref[h]
        o_inter = lax.dot_general(qi_ref[:, cols], s_t.astype(BF16), nt, preferred_element_type=F32)
        oo_ref[:, cols] = o_inter + jnp.dot(a_ref[h], vb, preferred_element_type=F32)
        upd = lax.dot_general(vb, ks_ref[:, cols], tn, preferred_element_type=F32)
        st_ref[h] = s_t * jnp.exp(b_last[:, cols]) + upd

    gh = g_ref[0]
    gate = gh * _sigmoid(gh)
    for h in range(N_HEADS):
        cols = slice(h * HEAD_W, (h + 1) * HEAD_W)
        oh = _rms(oo_ref[:, cols], gn_ref[:, cols])
        o_ref[0, :, cols] = (oh * gate[:, cols]).astype(o_ref.dtype)

    @pl.when(t == pl.num_programs(1) - 1)
    def _():
        for h in range(N_HEADS):
            so_ref[0, h] = st_ref[h].T


def _gla(proj, log_lower, log1m_lower, one_m_lower, g_out, s0, name):
    b, t, _ = proj.shape
    d = D_MODEL
    tc = _row_tile(t, GLA_TILE)
    assert tc % GLA_BLOCK == 0
    vec = pl.BlockSpec((1, d), lambda bi, ti: (0, 0))
    col_spec = lambda c: pl.BlockSpec((1, tc, d), lambda bi, ti: (bi, ti, c))
    st_spec = pl.BlockSpec((1, N_HEADS, HEAD_W, HEAD_W), lambda bi, ti: (bi, 0, 0, 0))
    n_fac = 2 * (1 + len(_gla_levels(tc)))
    return pl.pallas_call(
        _gla_kernel,
        out_shape=(jax.ShapeDtypeStruct((b, t, d), BF16),
                   jax.ShapeDtypeStruct((b, N_HEADS, HEAD_W, HEAD_W), F32)),
        grid=(b, t // tc),
        in_specs=[col_spec(0), col_spec(1), col_spec(2), col_spec(3),
                  vec, vec, vec, vec, st_spec,
                  pl.BlockSpec((tc, tc), lambda bi, ti: (0, 0))],
        out_specs=(pl.BlockSpec((1, tc, d), lambda bi, ti: (bi, ti, 0)), st_spec),
        scratch_shapes=[pltpu.VMEM((N_HEADS, HEAD_W, HEAD_W), F32),
                        pltpu.VMEM((tc, d), F32),
                        pltpu.VMEM((n_fac, tc, d), BF16),
                        pltpu.VMEM((tc, d), BF16), pltpu.VMEM((tc, d), BF16),
                        pltpu.VMEM((tc, d), BF16),
                        pltpu.VMEM((N_HEADS, tc, tc), BF16),
                        pltpu.VMEM((tc, d), F32)],
        compiler_params=_cparams("parallel", "arbitrary"),
        name=name,
    )(proj, proj, proj, proj, log_lower.reshape(1, d), log1m_lower.reshape(1, d),
      one_m_lower.reshape(1, d), g_out.reshape(1, d), s0, jnp.asarray(_gla_level_table(tc)))


def _t5_bucket(rel):
    half = NUM_BUCKETS // 2
    ret = jnp.where(rel > 0, half, 0)
    n = jnp.abs(rel)
    max_exact = half // 2
    nf = jnp.maximum(n, 1).astype(F32)
    large = max_exact + (jnp.log(nf / max_exact) / math.log(MAX_DISTANCE / max_exact)
                         * (half - max_exact)).astype(jnp.int32)
    large = jnp.minimum(large, half - 1)
    return ret + jnp.where(n < max_exact, n, large)


def _toeplitz_kernel(w_ref, o_ref, *, tq):
    tk = o_ref.shape[2]
    p = w_ref.shape[3]
    x = jnp.broadcast_to(w_ref[0, 0], (tk, p))
    o_ref[0, 0] = pltpu.roll(x, 0, axis=1, stride=1, stride_axis=0)[:, :tq]


def _bias_tiles_t(rel_bias, d0_list, tq, tk):
    p = pl.cdiv(tq + tk, HEAD_W) * HEAD_W
    m = jnp.arange(p, dtype=jnp.int32)
    rel = jnp.asarray(d0_list, jnp.int32)[:, None] + jnp.where(m <= tq, -m, p - m)[None, :]
    onehot = (_t5_bucket(rel)[:, :, None] == jnp.arange(NUM_BUCKETS, dtype=jnp.int32)).astype(F32)
    w = jnp.einsum("dpn,nh->hdp", onehot, rel_bias.astype(F32),
                   precision=lax.Precision.HIGHEST)
    h, nd = w.shape[0], w.shape[1]
    return pl.pallas_call(
        functools.partial(_toeplitz_kernel, tq=tq),
        out_shape=jax.ShapeDtypeStruct((h, nd, tk, tq), F32),
        grid=(h, nd),
        in_specs=[pl.BlockSpec((1, 1, 1, p), lambda hi, di: (hi, di, 0, 0))],
        out_specs=pl.BlockSpec((1, 1, tk, tq), lambda hi, di: (hi, di, 0, 0)),
        compiler_params=_cparams("parallel", "parallel"),
        name="bias_tiles",
    )(w.reshape(h, nd, 1, p))


def _attn_kernel(*refs, segs, q_off, tq, hb):
    nseg = len(segs)
    q_ref, lam_ref, gn_ref = refs[0], refs[1], refs[2]
    seg_refs = [refs[3 + 3 * s: 6 + 3 * s] for s in range(nseg)]
    o_ref = refs[3 + 3 * nseg]
    m_ref, l_ref, acc_ref = refs[4 + 3 * nseg: 7 + 3 * nseg]
    kv_scratch = refs[7 + 3 * nseg:]
    seg_scr = [kv_scratch[4 * s: 4 * s + 4] for s in range(nseg)]

    i = pl.program_id(2)
    q0 = q_off + i * tq

    @pl.when(i == 0)
    def _():
        for (k_ref, v_ref, _), (kb_ref, vt_ref, _, _), (_, seg_len, tk, _, _) in zip(seg_refs, seg_scr, segs):
            kb_ref[...] = k_ref[0].astype(BF16)
            for kt in range(seg_len // tk):
                vt_ref[kt] = v_ref[0, kt * tk:(kt + 1) * tk, :].astype(F32).T.astype(BF16)

    q = q_ref[0]
    lane = lax.broadcasted_iota(jnp.int32, (tq, HEAD_W), 1)
    qz = []
    for hh in range(hb):
        qh = q[:, hh * HEAD_W:(hh + 1) * HEAD_W] * (DA_HEAD_DIM ** -0.5)
        qz.append([jnp.where(lane < DA_HEAD_DIM, qh, 0).astype(BF16),
                   jnp.where(lane >= DA_HEAD_DIM, qh, 0).astype(BF16)])
    m_ref[...] = jnp.full_like(m_ref, NEG_BIG)
    l_ref[...] = jnp.zeros_like(l_ref)
    acc_ref[...] = jnp.zeros_like(acc_ref)
    q_chunk = (q0 + lax.broadcasted_iota(jnp.int32, (1, tq), 1)) // CHUNK
    vis_end = ((q0 + tq - 1) // CHUNK + 1) * CHUNK
    full_end = (q0 // CHUNK + 1) * CHUNK
    nt = (((1,), (1,)), ((), ()))

    for (_, _, b_ref), (kb_ref, vt_ref, sa_ref, sb_ref), (pos0, seg_len, tk, d0_min, d0_step) in zip(
            seg_refs, seg_scr, segs):
        n_tiles = seg_len // tk
        n_vis = jnp.clip((vis_end - pos0 + tk - 1) // tk, 0, n_tiles)
        n_full = jnp.clip((full_end - pos0) // tk, 0, n_vis)

        def scores_into(kt, s_ref, kb_ref=kb_ref, tk=tk, n_tiles=n_tiles):
            r0 = pl.multiple_of(jnp.minimum(kt, n_tiles - 1) * tk, tk)
            for hh in range(hb):
                kb = kb_ref[pl.ds(r0, tk), hh * HEAD_W:(hh + 1) * HEAD_W]
                for c in range(2):
                    s_ref[2 * hh + c] = lax.dot_general(kb, qz[hh][c], nt, preferred_element_type=F32)

        def consume(kt, s_ref, masked, b_ref=b_ref, vt_ref=vt_ref, pos0=pos0, tk=tk,
                    d0_min=d0_min, d0_step=d0_step, n_tiles=n_tiles, n_vis=n_vis):
            ktc = jnp.minimum(kt, n_tiles - 1)
            r0 = ktc * tk
            d_idx = (pos0 + r0 - q0 - d0_min) // d0_step
            if masked:
                k0 = jnp.where(kt < n_vis, pos0 + r0, jnp.int32(1 << 28))
                k_chunk = (k0 + lax.broadcasted_iota(jnp.int32, (tk, 1), 0)) // CHUNK
                visible = k_chunk <= q_chunk
            for hh in range(hb):
                vt = vt_ref[ktc, hh * HEAD_W:(hh + 1) * HEAD_W, :]
                bias = b_ref[hh, d_idx]
                for c in range(2):
                    j = 2 * hh + c
                    s = s_ref[j] + bias
                    if masked:
                        s = jnp.where(visible, s, NEG_BIG)
                    m_prev = m_ref[j]
                    m_new = jnp.maximum(m_prev, jnp.max(s, axis=0, keepdims=True))
                    p = jnp.exp(s - m_new)
                    alpha = jnp.exp(m_prev - m_new)
                    l_ref[j] = alpha * l_ref[j] + jnp.sum(p, axis=0, keepdims=True)
                    acc_ref[j] = alpha * acc_ref[j] + jnp.dot(vt, p.astype(BF16),
                                                              preferred_element_type=F32)
                    m_ref[j] = m_new

        def pair(u, carry, masked, base, sa_ref=sa_ref, sb_ref=sb_ref):
            t0 = base + 2 * u
            scores_into(t0 + 1, sb_ref)
            consume(t0, sa_ref, masked)
            scores_into(t0 + 2, sa_ref)
            consume(t0 + 1, sb_ref, masked)
            return carry

        n_pairs_full = n_full // 2
        base = 2 * n_pairs_full
        scores_into(0, sa_ref)
        lax.fori_loop(0, n_pairs_full, functools.partial(pair, masked=False, base=0), 0)
        lax.fori_loop(0, (n_vis - base + 1) // 2, functools.partial(pair, masked=True, base=base), 0)

    lam = lam_ref[0:1, 0:1]
    for hh in range(hb):
        o_t = (acc_ref[2 * hh] * (1.0 / l_ref[2 * hh])
               - lam * (acc_ref[2 * hh + 1] * (1.0 / l_ref[2 * hh + 1])))
        cols = slice(hh * HEAD_W, (hh + 1) * HEAD_W)
        o_ref[0, :, cols] = (_rms(o_t.T, gn_ref[:, cols]) * lam_ref[0:1, 1:2]).astype(o_ref.dtype)


def _attention(q, segments, rel_bias, lam, out_scale, g_subln, q_off, name):
    b, t_real, d = q.shape
    tq_all = max(t_real, HEAD_W)
    if tq_all != t_real:
        q = jnp.pad(q, ((0, 0), (0, tq_all - t_real), (0, 0)))
    tq = _row_tile(tq_all, ATTN_TILE)
    nq = tq_all // tq
    hb = ATTN_HEADS_PER_STEP
    hw = hb * HEAD_W
    segs, args, in_specs, scratch = [], [], [], []
    for (k, v, pos0, tk) in segments:
        seg_len = k.shape[1]
        assert seg_len % tk == 0
        d0s = sorted({pos0 + kt * tk - (q_off + i * tq)
                      for i in range(nq) for kt in range(seg_len // tk)
                      if pos0 + kt * tk < ((q_off + (i + 1) * tq - 1) // CHUNK + 1) * CHUNK})
        step = math.gcd(tq, tk)
        d0s = list(range(d0s[0], d0s[-1] + 1, step))
        tiles = _bias_tiles_t(rel_bias, d0s, tq, tk)
        segs.append((pos0, seg_len, tk, d0s[0], step))
        args += [k, v, tiles]
        in_specs += [pl.BlockSpec((1, seg_len, hw), lambda bi, h, i: (bi, 0, h)),
                     pl.BlockSpec((1, seg_len, hw), lambda bi, h, i: (bi, 0, h)),
                     pl.BlockSpec((hb, len(d0s), tk, tq), lambda bi, h, i: (h, 0, 0, 0))]
        scratch += [pltpu.VMEM((seg_len, hw), BF16), pltpu.VMEM((seg_len // tk, hw, tk), BF16),
                    pltpu.VMEM((2 * hb, tk, tq), F32), pltpu.VMEM((2 * hb, tk, tq), F32)]
    scal = jnp.zeros((1, HEAD_W), F32).at[0, 0].set(lam).at[0, 1].set(out_scale)
    kern = functools.partial(_attn_kernel, segs=tuple(segs), q_off=q_off, tq=tq, hb=hb)
    out = pl.pallas_call(
        kern,
        out_shape=jax.ShapeDtypeStruct((b, tq_all, d), BF16),
        grid=(b, N_HEADS // hb, nq),
        in_specs=[pl.BlockSpec((1, tq, hw), lambda bi, h, i: (bi, i, h)),
                  pl.BlockSpec((1, HEAD_W), lambda bi, h, i: (0, 0)),
                  pl.BlockSpec((1, hw), lambda bi, h, i: (0, h))] + in_specs,
        out_specs=pl.BlockSpec((1, tq, hw), lambda bi, h, i: (bi, i, h)),
        scratch_shapes=[pltpu.VMEM((2 * hb, 1, tq), F32), pltpu.VMEM((2 * hb, 1, tq), F32),
                        pltpu.VMEM((2 * hb, HEAD_W, tq), F32)] + scratch,
        compiler_params=_cparams("parallel", "parallel", "arbitrary"),
        name=name,
    )(q, scal, g_subln.reshape(1, d), *args)
    return out[:, :t_real]


def _trunk(x, ple, q_off, past_k, past_v, hg_s0, conv_s0, wts, tag):
    (g_norms, w_in_a, lower, g_hg, w_out_a, g_kv, w_kv, rel_bias, w_q_b, lam_b,
     g_subln, w_out_b, w_ffn_in, conv_w, conv_b, w_ffn_out, w_ple, w_ple_gate) = wts
    b, t, d = x.shape
    depth = g_norms.shape[0]
    n_a = w_in_a.shape[0]
    m = b * t
    hg_out, conv_out = [], []
    k_new = v_new = None
    segments = None
    for i in range(depth):
        nm = f"{tag}{i}"
        x2 = x.reshape(m, d)
        if i < n_a:
            lw = lower[i]
            proj = _norm_matmul(x2, g_norms[i, 0], w_in_a[i], F32, nm + "_hg_in")
            o, s = _gla(proj.reshape(b, t, 4 * d), jnp.log(lw), jnp.log1p(-lw), 1.0 - lw,
                        g_hg[i], hg_s0[i], nm + "_gla")
            hg_out.append(s)
            x2 = _matmul_norm_res(o.reshape(m, d), w_out_a[i], g_norms[i, 1], x2, nm + "_hg_out")
        else:
            j = i - n_a
            q = _norm_matmul(x2, g_norms[i, 0], w_q_b[j], BF16, nm + "_q")
            lam_init = 0.8 - 0.6 * math.exp(-0.3 * i)
            lp = lam_b[j].astype(F32)
            lam = jnp.exp(jnp.sum(lp[0] * lp[1])) - jnp.exp(jnp.sum(lp[2] * lp[3])) + lam_init
            o = _attention(q.reshape(b, t, d), segments, rel_bias, lam, 1.0 - lam_init,
                           g_subln[j], q_off, nm + "_attn")
            x2 = _matmul_norm_res(o.reshape(m, d), w_out_b[j], g_norms[i, 1], x2, nm + "_attn_out")
        x3, cbuf = _ffn(x2.reshape(b, t, d), g_norms[i, 2], w_ffn_in[i], conv_w[i], conv_b[i],
                        w_ffn_out[i], g_norms[i, 3], conv_s0[i], nm + "_ffn")
        conv_out.append(cbuf)
        x2 = _ple(x3.reshape(m, d), ple[i].reshape(m, -1), w_ple_gate[i], w_ple[i], nm + "_ple")
        x = x2.reshape(b, t, d)
        if i == n_a - 1:
            k_new, v_new = _norm_kv(x2, g_kv, w_kv, nm + "_kv")
            k_new = k_new.reshape(b, t, d)
            v_new = v_new.reshape(b, t, d)
            segments = []
            if past_k is not None:
                tp = past_k.shape[1]
                segments.append((past_k.reshape(b, tp, d), past_v.reshape(b, tp, d), 0, tp))
            segments.append((k_new, v_new, q_off, min(t, ATTN_KEY_TILE)))
    hd = (b, t, N_HEADS, HEAD_W)
    return x, k_new.reshape(hd), v_new.reshape(hd), jnp.stack(hg_out), jnp.stack(conv_out)


def kernel(x_prompt, x_sample, cache_k, cache_v, state_hgrn, state_conv, p_prompt, p_sample,
           g_norms, w_in_a, lb_raw, g_hg, w_out_a, g_kv, w_kv, rel_bias,
           w_q_b, lam_b, g_subln, w_out_b, w_ffn_in, conv_w, conv_b, w_ffn_out,
           w_ple, w_ple_gate):
    sm = jax.nn.softmax(lb_raw.astype(F32), axis=0)
    cs = jnp.cumsum(sm, axis=0)
    lower = cs - cs[0:1]
    bf = lambda w: w.astype(BF16)
    wts = (g_norms, bf(w_in_a), lower, g_hg, bf(w_out_a), g_kv, bf(w_kv), rel_bias,
           bf(w_q_b), lam_b, g_subln, bf(w_out_b), bf(w_ffn_in), conv_w, conv_b,
           bf(w_ffn_out), bf(w_ple), bf(w_ple_gate))
    bp, tp, _ = x_prompt.shape
    n_a, depth = w_in_a.shape[0], g_norms.shape[0]
    hg0 = jnp.zeros((n_a, bp) + state_hgrn.shape[2:], F32)
    cv0 = jnp.zeros((depth, bp) + state_conv.shape[2:], F32)
    y_p, k_p, v_p, hg_p, cv_p = _trunk(x_prompt, p_prompt, 0, None, None, hg0, cv0, wts, "p")
    y_s, k_s, v_s, hg_s, cv_s = _trunk(x_sample, p_sample, cache_k.shape[1], cache_k, cache_v,
                                       state_hgrn, state_conv, wts, "s")
    return (y_p, y_s, k_p, v_p, k_s, v_s, hg_p, hg_s, cv_p, cv_s)
```

```python
import functools
import math

import numpy as np
import jax
import jax.numpy as jnp
from jax import lax
from jax.experimental import pallas as pl
from jax.experimental.pallas import tpu as pltpu

F32 = jnp.float32
BF16 = jnp.bfloat16

D_MODEL = 1024
N_HEADS = 8
HEAD_W = D_MODEL // N_HEADS
DA_HEAD_DIM = HEAD_W // 2
CHUNK = 64
GLA_BLOCK = 16
NUM_BUCKETS = 32
MAX_DISTANCE = 256
D_FF = 2816
CONV_W = 3
EPS = 1e-6
NEG_BIG = -1e30

V7X_VMEM_LIMIT_BYTES = 56 * 1024 * 1024

ROW_TILE = 512
FFN_ROW_TILE = 1024
FFN_COL_TILE = 256
FFN_CONV_ROWS = 128
GLA_TILE = 256
ATTN_TILE = 512
ATTN_KEY_TILE = 256
ATTN_HEADS_PER_STEP = 2


def _cparams(*sem):
    return pltpu.CompilerParams(dimension_semantics=sem,
                                vmem_limit_bytes=V7X_VMEM_LIMIT_BYTES)


def _rms(x, g):
    ms = jnp.mean(x * x, axis=-1, keepdims=True)
    return x * lax.rsqrt(ms + EPS) * g


def _sigmoid(z):
    return 1.0 / (1.0 + jnp.exp(-z))


def _row_tile(m, cap):
    t = min(m, cap)
    assert m % t == 0, (m, t)
    return t


def _norm_matmul_kernel(x_ref, g_ref, w_ref, o_ref):
    h = _rms(x_ref[...], g_ref[...]).astype(BF16)
    n = w_ref.shape[1]
    tn = min(n, D_MODEL)
    for n0 in range(0, n, tn):
        o_ref[:, n0:n0 + tn] = jnp.dot(h, w_ref[:, n0:n0 + tn],
                                       preferred_element_type=F32).astype(o_ref.dtype)


def _norm_matmul(x2d, g, w, out_dtype, name):
    m, d = x2d.shape
    n = w.shape[1]
    tm = _row_tile(m, ROW_TILE)
    return pl.pallas_call(
        _norm_matmul_kernel,
        out_shape=jax.ShapeDtypeStruct((m, n), out_dtype),
        grid=(m // tm,),
        in_specs=[pl.BlockSpec((tm, d), lambda i: (i, 0)),
                  pl.BlockSpec((1, d), lambda i: (0, 0)),
                  pl.BlockSpec((d, n), lambda i: (0, 0))],
        out_specs=pl.BlockSpec((tm, n), lambda i: (i, 0)),
        compiler_params=_cparams("parallel"),
        name=name,
    )(x2d, g.reshape(1, d), w)


def _norm_kv_kernel(x_ref, g_ref, w_ref, k_ref, v_ref):
    d = k_ref.shape[1]
    h = _rms(x_ref[...], g_ref[...]).astype(BF16)
    k_ref[...] = jnp.dot(h, w_ref[:, :d], preferred_element_type=F32)
    v_ref[...] = jnp.dot(h, w_ref[:, d:], preferred_element_type=F32)


def _norm_kv(x2d, g, w, name):
    m, d = x2d.shape
    tm = _row_tile(m, ROW_TILE)
    out = jax.ShapeDtypeStruct((m, d), F32)
    return pl.pallas_call(
        _norm_kv_kernel,
        out_shape=(out, out),
        grid=(m // tm,),
        in_specs=[pl.BlockSpec((tm, d), lambda i: (i, 0)),
                  pl.BlockSpec((1, d), lambda i: (0, 0)),
                  pl.BlockSpec((d, 2 * d), lambda i: (0, 0))],
        out_specs=(pl.BlockSpec((tm, d), lambda i: (i, 0)),
                   pl.BlockSpec((tm, d), lambda i: (i, 0))),
        compiler_params=_cparams("parallel"),
        name=name,
    )(x2d, g.reshape(1, d), w)


def _matmul_norm_res_kernel(a_ref, w_ref, g_ref, x_ref, o_ref):
    m = jnp.dot(a_ref[...], w_ref[...], preferred_element_type=F32)
    o_ref[...] = x_ref[...] + _rms(m, g_ref[...])


def _matmul_norm_res(a2d, w, g, x2d, name):
    m, k = a2d.shape
    d = w.shape[1]
    tm = _row_tile(m, ROW_TILE)
    return pl.pallas_call(
        _matmul_norm_res_kernel,
        out_shape=jax.ShapeDtypeStruct((m, d), F32),
        grid=(m // tm,),
        in_specs=[pl.BlockSpec((tm, k), lambda i: (i, 0)),
                  pl.BlockSpec((k, d), lambda i: (0, 0)),
                  pl.BlockSpec((1, d), lambda i: (0, 0)),
                  pl.BlockSpec((tm, d), lambda i: (i, 0))],
        out_specs=pl.BlockSpec((tm, d), lambda i: (i, 0)),
        compiler_params=_cparams("parallel"),
        name=name,
    )(a2d, w, g.reshape(1, d), x2d)


def _ple_kernel(x_ref, p_ref, wg_ref, wp_ref, o_ref):
    x = x_ref[...]
    z = jnp.dot(x.astype(BF16), wg_ref[...], preferred_element_type=F32)
    e = jnp.dot(p_ref[...].astype(BF16), wp_ref[...], preferred_element_type=F32)
    o_ref[...] = x + _sigmoid(z) * e


def _ple(x2d, p2d, wg, wp, name):
    m, d = x2d.shape
    pd = p2d.shape[1]
    tm = _row_tile(m, ROW_TILE)
    return pl.pallas_call(
        _ple_kernel,
        out_shape=jax.ShapeDtypeStruct((m, d), F32),
        grid=(m // tm,),
        in_specs=[pl.BlockSpec((tm, d), lambda i: (i, 0)),
                  pl.BlockSpec((tm, pd), lambda i: (i, 0)),
                  pl.BlockSpec((d, d), lambda i: (0, 0)),
                  pl.BlockSpec((pd, d), lambda i: (0, 0))],
        out_specs=pl.BlockSpec((tm, d), lambda i: (i, 0)),
        compiler_params=_cparams("parallel"),
        name=name,
    )(x2d, p2d, wg, wp)


def _ffn_kernel(x_ref, gin_ref, wg_ref, wu_ref, cw_ref, cb_ref, wo_ref, gout_ref, cs_ref,
                o_ref, co_ref, h_ref, acc_ref, g0_ref, u0_ref, g1_ref, u1_ref, a0_ref, a1_ref, *, nf):
    t = pl.program_id(1)
    s = pl.program_id(2)
    nb, tm, d = x_ref.shape
    rows = nb * tm

    @pl.when(s == 0)
    def _():
        h_ref[...] = _rms(x_ref[...].reshape(rows, d), gin_ref[...]).astype(BF16)
        acc_ref[...] = jnp.zeros_like(acc_ref)
        g1_ref[...] = jnp.zeros_like(g1_ref)
        u1_ref[...] = jnp.zeros_like(u1_ref)
        a0_ref[...] = jnp.zeros_like(a0_ref)

    @pl.when((s == 0) & (t == 0))
    def _():
        co_ref[...] = cs_ref[...]

    def step(ga_ref, ua_ref, gb_ref, ub_ref, ab_ref, ac_ref):
        valid = (s >= 1) & (s <= nf)
        jb = jnp.clip(s - 1, 0, nf - 1)
        cw = cw_ref[...]
        cb = cb_ref[...]
        r8 = lax.broadcasted_iota(jnp.int32, (8, cw.shape[1]), 0)
        rc = min(tm, FFN_CONV_ROWS)
        for bi in range(nb):
            base = bi * tm
            prev = co_ref[bi, jb]
            co_ref[bi, jb] = jnp.where(valid, gb_ref[base + tm - (CONV_W - 1):base + tm, :], prev)
            head = jnp.where(r8 == 6, prev[0:1, :], prev[1:2, :])
            for r0 in range(base, base + tm, rc):
                if r0 > base:
                    head = gb_ref[r0 - 8:r0, :]
                gc = gb_ref[r0:r0 + rc, :]
                ext = jnp.concatenate([head, gc], axis=0)
                g1 = pltpu.roll(ext, 1, axis=0)[8:, :]
                g2 = pltpu.roll(ext, 2, axis=0)[8:, :]
                c = cb + cw[0:1, :] * g2 + cw[1:2, :] * g1 + cw[2:3, :] * gc
                ab_ref[r0:r0 + rc, :] = (c * _sigmoid(c) * ub_ref[r0:r0 + rc, :]).astype(BF16)
        h = h_ref[...]
        ga_ref[...] = jnp.dot(h, wg_ref[0], preferred_element_type=F32)
        ua_ref[...] = jnp.dot(h, wu_ref[0], preferred_element_type=F32)
        acc_ref[...] += jnp.dot(ac_ref[...], wo_ref[...], preferred_element_type=F32)

    @pl.when(s % 2 == 0)
    def _():
        step(g0_ref, u0_ref, g1_ref, u1_ref, a1_ref, a0_ref)

    @pl.when(s % 2 == 1)
    def _():
        step(g1_ref, u1_ref, g0_ref, u0_ref, a0_ref, a1_ref)

    @pl.when(s == pl.num_programs(2) - 1)
    def _():
        y = x_ref[...].reshape(rows, d) + _rms(acc_ref[...], gout_ref[...])
        o_ref[...] = y.reshape(nb, tm, d)


def _ffn(x, gin, w_in, cw, cb, w_out, gout, conv_state, name):
    b, t, d = x.shape
    f = w_out.shape[0]
    tm = _row_tile(t, FFN_ROW_TILE)
    nb = max(1, min(b, FFN_ROW_TILE // t))
    assert b % nb == 0
    rows = nb * tm
    tf = FFN_COL_TILE
    nf = f // tf
    assert f % tf == 0 and t >= CONV_W - 1 and tm % 8 == 0
    cs = conv_state.reshape(b, CONV_W - 1, nf, tf).transpose(0, 2, 1, 3)
    cs_spec = pl.BlockSpec((nb, nf, CONV_W - 1, tf), lambda bi, ti, s: (bi, 0, 0, 0))
    w_tiles = w_in.reshape(d, 2 * nf, tf).transpose(1, 0, 2)
    col_a = lambda s: jnp.minimum(s, nf - 1)
    col_b = lambda s: jnp.clip(s - 1, 0, nf - 1)
    col_c = lambda s: jnp.clip(s - 2, 0, nf - 1)
    y, co = pl.pallas_call(
        functools.partial(_ffn_kernel, nf=nf),
        out_shape=(jax.ShapeDtypeStruct((b, t, d), F32),
                   jax.ShapeDtypeStruct(cs.shape, F32)),
        grid=(b // nb, t // tm, nf + 2),
        in_specs=[pl.BlockSpec((nb, tm, d), lambda bi, ti, s: (bi, ti, 0)),
                  pl.BlockSpec((1, d), lambda bi, ti, s: (0, 0)),
                  pl.BlockSpec((1, d, tf), lambda bi, ti, s: (col_a(s), 0, 0)),
                  pl.BlockSpec((1, d, tf), lambda bi, ti, s: (col_a(s) + nf, 0, 0)),
                  pl.BlockSpec((CONV_W, tf), lambda bi, ti, s: (0, col_b(s))),
                  pl.BlockSpec((1, tf), lambda bi, ti, s: (0, col_b(s))),
                  pl.BlockSpec((tf, d), lambda bi, ti, s: (col_c(s), 0)),
                  pl.BlockSpec((1, d), lambda bi, ti, s: (0, 0)),
                  cs_spec],
        out_specs=(pl.BlockSpec((nb, tm, d), lambda bi, ti, s: (bi, ti, 0)), cs_spec),
        scratch_shapes=[pltpu.VMEM((rows, d), BF16),
                        pltpu.VMEM((rows, d), F32),
                        pltpu.VMEM((rows, tf), F32), pltpu.VMEM((rows, tf), F32),
                        pltpu.VMEM((rows, tf), F32), pltpu.VMEM((rows, tf), F32),
                        pltpu.VMEM((rows, tf), BF16), pltpu.VMEM((rows, tf), BF16)],
        compiler_params=_cparams("parallel", "arbitrary", "arbitrary"),
        name=name,
    )(x, gin.reshape(1, d), w_tiles, w_tiles, cw, cb.reshape(1, f), w_out,
      gout.reshape(1, d), cs)
    return y, co.transpose(0, 2, 1, 3).reshape(b, CONV_W - 1, f)


def _split3(x):
    hi = x.astype(BF16)
    r1 = x - hi.astype(F32)
    mid = r1.astype(BF16)
    lo = (r1 - mid.astype(F32)).astype(BF16)
    return hi, mid, lo


def _gla_levels(tc):
    return [s for s in (16, 32, 64, 128, 256, 512, 1024) if 2 * s <= tc]


def _gla_level_table(tc):
    r = np.arange(tc)[:, None]
    c = np.arange(tc)[None, :]
    table = np.full((tc, tc), -1, np.int32)
    table[(r // GLA_BLOCK == c // GLA_BLOCK) & (c <= r)] = 0
    for i, s in enumerate(_gla_levels(tc)):
        table[(r // (2 * s) == c // (2 * s)) & ((r // s) % 2 == 1) & ((c // s) % 2 == 0)] = i + 1
    return table


def _gla_kernel(q_ref, f_ref, i_ref, g_ref, ll_ref, l1m_ref, om_ref, gn_ref, s0_ref, lvl_ref,
                o_ref, so_ref, st_ref, bg_ref, fac_ref, qi_ref, ks_ref, v_ref, a_ref, oo_ref):
    t = pl.program_id(1)
    tc = q_ref.shape[1]
    d = q_ref.shape[2]
    levels = _gla_levels(tc)

    @pl.when(t == 0)
    def _():
        for h in range(N_HEADS):
            st_ref[h] = s0_ref[0, h].T

    qh = q_ref[0]
    fp = f_ref[0]
    q = qh * _sigmoid(qh)
    e = jnp.exp(-jnp.abs(fp))
    logsig = jnp.minimum(fp, 0.0) - jnp.log1p(e)
    a = ll_ref[...]
    c = l1m_ref[...] + logsig
    logf = jnp.maximum(a, c) + jnp.log1p(jnp.exp(-jnp.abs(a - c)))
    kk = om_ref[...] * (jnp.where(fp >= 0.0, e, 1.0) / (1.0 + e))

    row = lax.broadcasted_iota(jnp.int32, (tc, tc), 0)
    col = lax.broadcasted_iota(jnp.int32, (tc, tc), 1)
    m_incl = jnp.where(col <= row, 1.0, 0.0).astype(BF16)
    bg = sum(jnp.dot(m_incl, p, preferred_element_type=F32) for p in _split3(logf))
    bg_ref[...] = bg
    b_last = bg[tc - 1:tc, :]

    qi_ref[...] = (q * jnp.exp(bg)).astype(BF16)
    ks_ref[...] = (kk * jnp.exp(b_last - bg)).astype(BF16)
    v_ref[...] = i_ref[0].astype(BF16)

    def ref_rows(span, offset):
        pieces = []
        for a0 in range(0, tc, span):
            r = a0 + offset
            src = bg_ref[r:r + 1, :] if r >= 0 else jnp.zeros((1, d), F32)
            pieces.append(jnp.broadcast_to(src, (span, d)))
        return pieces[0] if len(pieces) == 1 else jnp.concatenate(pieces, axis=0)

    dl = bg - ref_rows(GLA_BLOCK, -1)
    fac_ref[0] = (q * jnp.exp(dl)).astype(BF16)
    fac_ref[1] = (kk * jnp.exp(-dl)).astype(BF16)
    rid = lax.broadcasted_iota(jnp.int32, (tc, 1), 0)
    for li, s in enumerate(levels):
        mid = ref_rows(2 * s, s - 1)
        right = ((rid // s) % 2) == 1
        x = jnp.exp(jnp.where(right, bg - mid, mid - bg))
        fac_ref[2 + 2 * li] = (q * x).astype(BF16)
        fac_ref[3 + 2 * li] = (kk * x).astype(BF16)

    nt = (((1,), (1,)), ((), ()))
    tn = (((0,), (0,)), ((), ()))
    lvl = lvl_ref[...]
    for h in range(N_HEADS):
        cols = slice(h * HEAD_W, (h + 1) * HEAD_W)
        att = jnp.zeros((tc, tc), F32)
        for ci in range(len(levels), -1, -1):
            prod = lax.dot_general(fac_ref[2 * ci, :, cols], fac_ref[2 * ci + 1, :, cols], nt,
                                   preferred_element_type=F32)
            att = jnp.where(lvl == ci, prod, att)
        a_ref[h] = att.astype(BF16)

    for h in range(N_HEADS):
        cols = slice(h * HEAD_W, (h + 1) * HEAD_W)
        vb = v_ref[:, cols]
        s_t = st_ref[h]
        o_inter = lax.dot_general(qi_ref[:, cols], s_t.astype(BF16), nt, preferred_element_type=F32)
        oo_ref[:, cols] = o_inter + jnp.dot(a_ref[h], vb, preferred_element_type=F32)
        upd = lax.dot_general(vb, ks_ref[:, cols], tn, preferred_element_type=F32)
        st_ref[h] = s_t * jnp.exp(b_last[:, cols]) + upd

    gh = g_ref[0]
    gate = gh * _sigmoid(gh)
    for h in range(N_HEADS):
        cols = slice(h * HEAD_W, (h + 1) * HEAD_W)
        oh = _rms(oo_ref[:, cols], gn_ref[:, cols])
        o_ref[0, :, cols] = (oh * gate[:, cols]).astype(o_ref.dtype)

    @pl.when(t == pl.num_programs(1) - 1)
    def _():
        for h in range(N_HEADS):
            so_ref[0, h] = st_ref[h].T


def _gla(proj, log_lower, log1m_lower, one_m_lower, g_out, s0, name):
    b, t, _ = proj.shape
    d = D_MODEL
    tc = _row_tile(t, GLA_TILE)
    assert tc % GLA_BLOCK == 0
    vec = pl.BlockSpec((1, d), lambda bi, ti: (0, 0))
    col_spec = lambda c: pl.BlockSpec((1, tc, d), lambda bi, ti: (bi, ti, c))
    st_spec = pl.BlockSpec((1, N_HEADS, HEAD_W, HEAD_W), lambda bi, ti: (bi, 0, 0, 0))
    n_fac = 2 * (1 + len(_gla_levels(tc)))
    return pl.pallas_call(
        _gla_kernel,
        out_shape=(jax.ShapeDtypeStruct((b, t, d), BF16),
                   jax.ShapeDtypeStruct((b, N_HEADS, HEAD_W, HEAD_W), F32)),
        grid=(b, t // tc),
        in_specs=[col_spec(0), col_spec(1), col_spec(2), col_spec(3),
                  vec, vec, vec, vec, st_spec,
                  pl.BlockSpec((tc, tc), lambda bi, ti: (0, 0))],
        out_specs=(pl.BlockSpec((1, tc, d), lambda bi, ti: (bi, ti, 0)), st_spec),
        scratch_shapes=[pltpu.VMEM((N_HEADS, HEAD_W, HEAD_W), F32),
                        pltpu.VMEM((tc, d), F32),
                        pltpu.VMEM((n_fac, tc, d), BF16),
                        pltpu.VMEM((tc, d), BF16), pltpu.VMEM((tc, d), BF16),
                        pltpu.VMEM((tc, d), BF16),
                        pltpu.VMEM((N_HEADS, tc, tc), BF16),
                        pltpu.VMEM((tc, d), F32)],
        compiler_params=_cparams("parallel", "arbitrary"),
        name=name,
    )(proj, proj, proj, proj, log_lower.reshape(1, d), log1m_lower.reshape(1, d),
      one_m_lower.reshape(1, d), g_out.reshape(1, d), s0, jnp.asarray(_gla_level_table(tc)))


def _t5_bucket(rel):
    half = NUM_BUCKETS // 2
    ret = jnp.where(rel > 0, half, 0)
    n = jnp.abs(rel)
    max_exact = half // 2
    nf = jnp.maximum(n, 1).astype(F32)
    large = max_exact + (jnp.log(nf / max_exact) / math.log(MAX_DISTANCE / max_exact)
                         * (half - max_exact)).astype(jnp.int32)
    large = jnp.minimum(large, half - 1)
    return ret + jnp.where(n < max_exact, n, large)


def _toeplitz_kernel(w_ref, o_ref, *, tq):
    tk = o_ref.shape[2]
    p = w_ref.shape[3]
    x = jnp.broadcast_to(w_ref[0, 0], (tk, p))
    o_ref[0, 0] = pltpu.roll(x, 0, axis=1, stride=1, stride_axis=0)[:, :tq]


def _bias_tiles_t(rel_bias, d0_list, tq, tk):
    p = pl.cdiv(tq + tk, HEAD_W) * HEAD_W
    m = jnp.arange(p, dtype=jnp.int32)
    rel = jnp.asarray(d0_list, jnp.int32)[:, None] + jnp.where(m <= tq, -m, p - m)[None, :]
    onehot = (_t5_bucket(rel)[:, :, None] == jnp.arange(NUM_BUCKETS, dtype=jnp.int32)).astype(F32)
    w = jnp.einsum("dpn,nh->hdp", onehot, rel_bias.astype(F32),
                   precision=lax.Precision.HIGHEST)
    h, nd = w.shape[0], w.shape[1]
    return pl.pallas_call(
        functools.partial(_toeplitz_kernel, tq=tq),
        out_shape=jax.ShapeDtypeStruct((h, nd, tk, tq), F32),
        grid=(h, nd),
        in_specs=[pl.BlockSpec((1, 1, 1, p), lambda hi, di: (hi, di, 0, 0))],
        out_specs=pl.BlockSpec((1, 1, tk, tq), lambda hi, di: (hi, di, 0, 0)),
        compiler_params=_cparams("parallel", "parallel"),
        name="bias_tiles",
    )(w.reshape(h, nd, 1, p))


def _attn_kernel(*refs, segs, q_off, tq, hb):
    nseg = len(segs)
    q_ref, lam_ref, gn_ref = refs[0], refs[1], refs[2]
    seg_refs = [refs[3 + 3 * s: 6 + 3 * s] for s in range(nseg)]
    o_ref = refs[3 + 3 * nseg]
    m_ref, l_ref, acc_ref = refs[4 + 3 * nseg: 7 + 3 * nseg]
    kv_scratch = refs[7 + 3 * nseg:]
    seg_scr = [kv_scratch[4 * s: 4 * s + 4] for s in range(nseg)]

    i = pl.program_id(2)
    q0 = q_off + i * tq

    @pl.when(i == 0)
    def _():
        for (k_ref, v_ref, _), (kb_ref, vt_ref, _, _), (_, seg_len, tk, _, _) in zip(seg_refs, seg_scr, segs):
            kb_ref[...] = k_ref[0].astype(BF16)
            for kt in range(seg_len // tk):
                vt_ref[kt] = v_ref[0, kt * tk:(kt + 1) * tk, :].astype(F32).T.astype(BF16)

    q = q_ref[0]
    lane = lax.broadcasted_iota(jnp.int32, (tq, HEAD_W), 1)
    qz = []
    for hh in range(hb):
        qh = q[:, hh * HEAD_W:(hh + 1) * HEAD_W] * (DA_HEAD_DIM ** -0.5)
        qz.append([jnp.where(lane < DA_HEAD_DIM, qh, 0).astype(BF16),
                   jnp.where(lane >= DA_HEAD_DIM, qh, 0).astype(BF16)])
    m_ref[...] = jnp.full_like(m_ref, NEG_BIG)
    l_ref[...] = jnp.zeros_like(l_ref)
    acc_ref[...] = jnp.zeros_like(acc_ref)
    q_chunk = (q0 + lax.broadcasted_iota(jnp.int32, (1, tq), 1)) // CHUNK
    vis_end = ((q0 + tq - 1) // CHUNK + 1) * CHUNK
    full_end = (q0 // CHUNK + 1) * CHUNK
    nt = (((1,), (1,)), ((), ()))

    for (_, _, b_ref), (kb_ref, vt_ref, sa_ref, sb_ref), (pos0, seg_len, tk, d0_min, d0_step) in zip(
            seg_refs, seg_scr, segs):
        n_tiles = seg_len // tk
        n_vis = jnp.clip((vis_end - pos0 + tk - 1) // tk, 0, n_tiles)
        n_full = jnp.clip((full_end - pos0) // tk, 0, n_vis)

        def scores_into(kt, s_ref, kb_ref=kb_ref, tk=tk, n_tiles=n_tiles):
            r0 = pl.multiple_of(jnp.minimum(kt, n_tiles - 1) * tk, tk)
            for hh in range(hb):
                kb = kb_ref[pl.ds(r0, tk), hh * HEAD_W:(hh + 1) * HEAD_W]
                for c in range(2):
                    s_ref[2 * hh + c] = lax.dot_general(kb, qz[hh][c], nt, preferred_element_type=F32)

        def consume(kt, s_ref, masked, b_ref=b_ref, vt_ref=vt_ref, pos0=pos0, tk=tk,
                    d0_min=d0_min, d0_step=d0_step, n_tiles=n_tiles, n_vis=n_vis):
            ktc = jnp.minimum(kt, n_tiles - 1)
            r0 = ktc * tk
            d_idx = (pos0 + r0 - q0 - d0_min) // d0_step
            if masked:
                k0 = jnp.where(kt < n_vis, pos0 + r0, jnp.int32(1 << 28))
                k_chunk = (k0 + lax.broadcasted_iota(jnp.int32, (tk, 1), 0)) // CHUNK
                visible = k_chunk <= q_chunk
            for hh in range(hb):
                vt = vt_ref[ktc, hh * HEAD_W:(hh + 1) * HEAD_W, :]
                bias = b_ref[hh, d_idx]
                for c in range(2):
                    j = 2 * hh + c
                    s = s_ref[j] + bias
                    if masked:
                        s = jnp.where(visible, s, NEG_BIG)
                    m_prev = m_ref[j]
                    m_new = jnp.maximum(m_prev, jnp.max(s, axis=0, keepdims=True))
                    p = jnp.exp(s - m_new)
                    alpha = jnp.exp(m_prev - m_new)
                    l_ref[j] = alpha * l_ref[j] + jnp.sum(p, axis=0, keepdims=True)
                    acc_ref[j] = alpha * acc_ref[j] + jnp.dot(vt, p.astype(BF16),
                                                              preferred_element_type=F32)
                    m_ref[j] = m_new

        def pair(u, carry, masked, base, sa_ref=sa_ref, sb_ref=sb_ref):
            t0 = base + 2 * u
            scores_into(t0 + 1, sb_ref)
            consume(t0, sa_ref, masked)
            scores_into(t0 + 2, sa_ref)
            consume(t0 + 1, sb_ref, masked)
            return carry

        n_pairs_full = n_full // 2
        base = 2 * n_pairs_full
        scores_into(0, sa_ref)
        lax.fori_loop(0, n_pairs_full, functools.partial(pair, masked=False, base=0), 0)
        lax.fori_loop(0, (n_vis - base + 1) // 2, functools.partial(pair, masked=True, base=base), 0)

    lam = lam_ref[0:1, 0:1]
    for hh in range(hb):
        o_t = (acc_ref[2 * hh] * (1.0 / l_ref[2 * hh])
               - lam * (acc_ref[2 * hh + 1] * (1.0 / l_ref[2 * hh + 1])))
        cols = slice(hh * HEAD_W, (hh + 1) * HEAD_W)
        o_ref[0, :, cols] = (_rms(o_t.T, gn_ref[:, cols]) * lam_ref[0:1, 1:2]).astype(o_ref.dtype)


def _attention(q, segments, rel_bias, lam, out_scale, g_subln, q_off, name):
    b, t_real, d = q.shape
    tq_all = max(t_real, HEAD_W)
    if tq_all != t_real:
        q = jnp.pad(q, ((0, 0), (0, tq_all - t_real), (0, 0)))
    tq = _row_tile(tq_all, ATTN_TILE)
    nq = tq_all // tq
    hb = ATTN_HEADS_PER_STEP
    hw = hb * HEAD_W
    segs, args, in_specs, scratch = [], [], [], []
    for (k, v, pos0, tk) in segments:
        seg_len = k.shape[1]
        assert seg_len % tk == 0
        d0s = sorted({pos0 + kt * tk - (q_off + i * tq)
                      for i in range(nq) for kt in range(seg_len // tk)
                      if pos0 + kt * tk < ((q_off + (i + 1) * tq - 1) // CHUNK + 1) * CHUNK})
        step = math.gcd(tq, tk)
        d0s = list(range(d0s[0], d0s[-1] + 1, step))
        tiles = _bias_tiles_t(rel_bias, d0s, tq, tk)
        segs.append((pos0, seg_len, tk, d0s[0], step))
        args += [k, v, tiles]
        in_specs += [pl.BlockSpec((1, seg_len, hw), lambda bi, h, i: (bi, 0, h)),
                     pl.BlockSpec((1, seg_len, hw), lambda bi, h, i: (bi, 0, h)),
                     pl.BlockSpec((hb, len(d0s), tk, tq), lambda bi, h, i: (h, 0, 0, 0))]
        scratch += [pltpu.VMEM((seg_len, hw), BF16), pltpu.VMEM((seg_len // tk, hw, tk), BF16),
                    pltpu.VMEM((2 * hb, tk, tq), F32), pltpu.VMEM((2 * hb, tk, tq), F32)]
    scal = jnp.zeros((1, HEAD_W), F32).at[0, 0].set(lam).at[0, 1].set(out_scale)
    kern = functools.partial(_attn_kernel, segs=tuple(segs), q_off=q_off, tq=tq, hb=hb)
    out = pl.pallas_call(
        kern,
        out_shape=jax.ShapeDtypeStruct((b, tq_all, d), BF16),
        grid=(b, N_HEADS // hb, nq),
        in_specs=[pl.BlockSpec((1, tq, hw), lambda bi, h, i: (bi, i, h)),
                  pl.BlockSpec((1, HEAD_W), lambda bi, h, i: (0, 0)),
                  pl.BlockSpec((1, hw), lambda bi, h, i: (0, h))] + in_specs,
        out_specs=pl.BlockSpec((1, tq, hw), lambda bi, h, i: (bi, i, h)),
        scratch_shapes=[pltpu.VMEM((2 * hb, 1, tq), F32), pltpu.VMEM((2 * hb, 1, tq), F32),
                        pltpu.VMEM((2 * hb, HEAD_W, tq), F32)] + scratch,
        compiler_params=_cparams("parallel", "parallel", "arbitrary"),
        name=name,
    )(q, scal, g_subln.reshape(1, d), *args)
    return out[:, :t_real]


def _trunk(x, ple, q_off, past_k, past_v, hg_s0, conv_s0, wts, tag):
    (g_norms, w_in_a, lower, g_hg, w_out_a, g_kv, w_kv, rel_bias, w_q_b, lam_b,
     g_subln, w_out_b, w_ffn_in, conv_w, conv_b, w_ffn_out, w_ple, w_ple_gate) = wts
    b, t, d = x.shape
    depth = g_norms.shape[0]
    n_a = w_in_a.shape[0]
    m = b * t
    hg_out, conv_out = [], []
    k_new = v_new = None
    segments = None
    for i in range(depth):
        nm = f"{tag}{i}"
        x2 = x.reshape(m, d)
        if i < n_a:
            lw = lower[i]
            proj = _norm_matmul(x2, g_norms[i, 0], w_in_a[i], F32, nm + "_hg_in")
            o, s = _gla(proj.reshape(b, t, 4 * d), jnp.log(lw), jnp.log1p(-lw), 1.0 - lw,
                        g_hg[i], hg_s0[i], nm + "_gla")
            hg_out.append(s)
            x2 = _matmul_norm_res(o.reshape(m, d), w_out_a[i], g_norms[i, 1], x2, nm + "_hg_out")
        else:
            j = i - n_a
            q = _norm_matmul(x2, g_norms[i, 0], w_q_b[j], BF16, nm + "_q")
            lam_init = 0.8 - 0.6 * math.exp(-0.3 * i)
            lp = lam_b[j].astype(F32)
            lam = jnp.exp(jnp.sum(lp[0] * lp[1])) - jnp.exp(jnp.sum(lp[2] * lp[3])) + lam_init
            o = _attention(q.reshape(b, t, d), segments, rel_bias, lam, 1.0 - lam_init,
                           g_subln[j], q_off, nm + "_attn")
            x2 = _matmul_norm_res(o.reshape(m, d), w_out_b[j], g_norms[i, 1], x2, nm + "_attn_out")
        x3, cbuf = _ffn(x2.reshape(b, t, d), g_norms[i, 2], w_ffn_in[i], conv_w[i], conv_b[i],
                        w_ffn_out[i], g_norms[i, 3], conv_s0[i], nm + "_ffn")
        conv_out.append(cbuf)
        x2 = _ple(x3.reshape(m, d), ple[i].reshape(m, -1), w_ple_gate[i], w_ple[i], nm + "_ple")
        x = x2.reshape(b, t, d)
        if i == n_a - 1:
            k_new, v_new = _norm_kv(x2, g_kv, w_kv, nm + "_kv")
            k_new = k_new.reshape(b, t, d)
            v_new = v_new.reshape(b, t, d)
            segments = []
            if past_k is not None:
                tp = past_k.shape[1]
                segments.append((past_k.reshape(b, tp, d), past_v.reshape(b, tp, d), 0, tp))
            segments.append((k_new, v_new, q_off, min(t, ATTN_KEY_TILE)))
    hd = (b, t, N_HEADS, HEAD_W)
    return x, k_new.reshape(hd), v_new.reshape(hd), jnp.stack(hg_out), jnp.stack(conv_out)


def kernel(x_prompt, x_sample, cache_k, cache_v, state_hgrn, state_conv, p_prompt, p_sample,
           g_norms, w_in_a, lb_raw, g_hg, w_out_a, g_kv, w_kv, rel_bias,
           w_q_b, lam_b, g_subln, w_out_b, w_ffn_in, conv_w, conv_b, w_ffn_out,
           w_ple, w_ple_gate):
    sm = jax.nn.softmax(lb_raw.astype(F32), axis=0)
    cs = jnp.cumsum(sm, axis=0)
    lower = cs - cs[0:1]
    bf = lambda w: w.astype(BF16)
    wts = (g_norms, bf(w_in_a), lower, g_hg, bf(w_out_a), g_kv, bf(w_kv), rel_bias,
           bf(w_q_b), lam_b, g_subln, bf(w_out_b), bf(w_ffn_in), conv_w, conv_b,
           bf(w_ffn_out), bf(w_ple), bf(w_ple_gate))
    bp, tp, _ = x_prompt.shape
    n_a, depth = w_in_a.shape[0], g_norms.shape[0]
    hg0 = jnp.zeros((n_a, bp) + state_hgrn.shape[2:], F32)
    cv0 = jnp.zeros((depth, bp) + state_conv.shape[2:], F32)
    y_p, k_p, v_p, hg_p, cv_p = _trunk(x_prompt, p_prompt, 0, None, None, hg0, cv0, wts, "p")
    y_s, k_s, v_s, hg_s, cv_s = _trunk(x_sample, p_sample, cache_k.shape[1], cache_k, cache_v,
                                       state_hgrn, state_conv, wts, "s")
    return (y_p, y_s, k_p, v_p, k_s, v_s, hg_p, hg_s, cv_p, cv_s)
```

```python
import functools
import math

import numpy as np
import jax
import jax.numpy as jnp
from jax import lax
from jax.experimental import pallas as pl
from jax.experimental.pallas import tpu as pltpu

F32 = jnp.float32
BF16 = jnp.bfloat16

D_MODEL = 1024
N_HEADS = 8
HEAD_W = D_MODEL // N_HEADS
DA_HEAD_DIM = HEAD_W // 2
CHUNK = 64
GLA_BLOCK = 16
NUM_BUCKETS = 32
MAX_DISTANCE = 256
D_FF = 2816
CONV_W = 3
EPS = 1e-6
NEG_BIG = -1e30

V7X_VMEM_LIMIT_BYTES = 56 * 1024 * 1024

ROW_TILE = 512
FFN_ROW_TILE = 1024
FFN_COL_TILE = 256
FFN_CONV_ROWS = 128
GLA_TILE = 256
ATTN_TILE = 512
ATTN_KEY_TILE = 256
ATTN_HEADS_PER_STEP = 2


def _cparams(*sem):
    return pltpu.CompilerParams(dimension_semantics=sem,
                                vmem_limit_bytes=V7X_VMEM_LIMIT_BYTES)


def _rms(x, g):
    ms = jnp.mean(x * x, axis=-1, keepdims=True)
    return x * lax.rsqrt(ms + EPS) * g


def _sigmoid(z):
    return 1.0 / (1.0 + jnp.exp(-z))


def _row_tile(m, cap):
    t = min(m, cap)
    assert m % t == 0, (m, t)
    return t


def _norm_matmul_kernel(x_ref, g_ref, w_ref, o_ref):
    h = _rms(x_ref[...], g_ref[...]).astype(BF16)
    n = w_ref.shape[1]
    tn = min(n, D_MODEL)
    for n0 in range(0, n, tn):
        o_ref[:, n0:n0 + tn] = jnp.dot(h, w_ref[:, n0:n0 + tn],
                                       preferred_element_type=F32).astype(o_ref.dtype)


def _norm_matmul(x2d, g, w, out_dtype, name):
    m, d = x2d.shape
    n = w.shape[1]
    tm = _row_tile(m, ROW_TILE)
    return pl.pallas_call(
        _norm_matmul_kernel,
        out_shape=jax.ShapeDtypeStruct((m, n), out_dtype),
        grid=(m // tm,),
        in_specs=[pl.BlockSpec((tm, d), lambda i: (i, 0)),
                  pl.BlockSpec((1, d), lambda i: (0, 0)),
                  pl.BlockSpec((d, n), lambda i: (0, 0))],
        out_specs=pl.BlockSpec((tm, n), lambda i: (i, 0)),
        compiler_params=_cparams("parallel"),
        name=name,
    )(x2d, g.reshape(1, d), w)


def _norm_kv_kernel(x_ref, g_ref, w_ref, k_ref, v_ref):
    d = k_ref.shape[1]
    h = _rms(x_ref[...], g_ref[...]).astype(BF16)
    k_ref[...] = jnp.dot(h, w_ref[:, :d], preferred_element_type=F32)
    v_ref[...] = jnp.dot(h, w_ref[:, d:], preferred_element_type=F32)


def _norm_kv(x2d, g, w, name):
    m, d = x2d.shape
    tm = _row_tile(m, ROW_TILE)
    out = jax.ShapeDtypeStruct((m, d), F32)
    return pl.pallas_call(
        _norm_kv_kernel,
        out_shape=(out, out),
        grid=(m // tm,),
        in_specs=[pl.BlockSpec((tm, d), lambda i: (i, 0)),
                  pl.BlockSpec((1, d), lambda i: (0, 0)),
                  pl.BlockSpec((d, 2 * d), lambda i: (0, 0))],
        out_specs=(pl.BlockSpec((tm, d), lambda i: (i, 0)),
                   pl.BlockSpec((tm, d), lambda i: (i, 0))),
        compiler_params=_cparams("parallel"),
        name=name,
    )(x2d, g.reshape(1, d), w)


def _matmul_norm_res_kernel(a_ref, w_ref, g_ref, x_ref, o_ref):
    m = jnp.dot(a_ref[...], w_ref[...], preferred_element_type=F32)
    o_ref[...] = x_ref[...] + _rms(m, g_ref[...])


def _matmul_norm_res(a2d, w, g, x2d, name):
    m, k = a2d.shape
    d = w.shape[1]
    tm = _row_tile(m, ROW_TILE)
    return pl.pallas_call(
        _matmul_norm_res_kernel,
        out_shape=jax.ShapeDtypeStruct((m, d), F32),
        grid=(m // tm,),
        in_specs=[pl.BlockSpec((tm, k), lambda i: (i, 0)),
                  pl.BlockSpec((k, d), lambda i: (0, 0)),
                  pl.BlockSpec((1, d), lambda i: (0, 0)),
                  pl.BlockSpec((tm, d), lambda i: (i, 0))],
        out_specs=pl.BlockSpec((tm, d), lambda i: (i, 0)),
        compiler_params=_cparams("parallel"),
        name=name,
    )(a2d, w, g.reshape(1, d), x2d)


def _ple_kernel(x_ref, p_ref, wg_ref, wp_ref, o_ref):
    x = x_ref[...]
    z = jnp.dot(x.astype(BF16), wg_ref[...], preferred_element_type=F32)
    e = jnp.dot(p_ref[...].astype(BF16), wp_ref[...], preferred_element_type=F32)
    o_ref[...] = x + _sigmoid(z) * e


def _ple(x2d, p2d, wg, wp, name):
    m, d = x2d.shape
    pd = p2d.shape[1]
    tm = _row_tile(m, ROW_TILE)
    return pl.pallas_call(
        _ple_kernel,
        out_shape=jax.ShapeDtypeStruct((m, d), F32),
        grid=(m // tm,),
        in_specs=[pl.BlockSpec((tm, d), lambda i: (i, 0)),
                  pl.BlockSpec((tm, pd), lambda i: (i, 0)),
                  pl.BlockSpec((d, d), lambda i: (0, 0)),
                  pl.BlockSpec((pd, d), lambda i: (0, 0))],
        out_specs=pl.BlockSpec((tm, d), lambda i: (i, 0)),
        compiler_params=_cparams("parallel"),
        name=name,
    )(x2d, p2d, wg, wp)


def _ffn_kernel(x_ref, gin_ref, wgu_ref, cw_ref, cb_ref, wo_ref, gout_ref, cs_ref,
                o_ref, co_ref, h_ref, acc_ref, gu0_ref, gu1_ref, a0_ref, a1_ref, *, nf):
    t = pl.program_id(1)
    s = pl.program_id(2)
    nb, tm, d = x_ref.shape
    rows = nb * tm

    @pl.when(s == 0)
    def _():
        h_ref[...] = _rms(x_ref[...].reshape(rows, d), gin_ref[...]).astype(BF16)
        acc_ref[...] = jnp.zeros_like(acc_ref)
        gu1_ref[...] = jnp.zeros_like(gu1_ref)
        a0_ref[...] = jnp.zeros_like(a0_ref)

    @pl.when((s == 0) & (t == 0))
    def _():
        co_ref[...] = cs_ref[...]

    def step(gua_ref, gub_ref, ab_ref, ac_ref):
        valid = (s >= 1) & (s <= nf)
        jb = jnp.clip(s - 1, 0, nf - 1)
        cw = cw_ref[...]
        cb = cb_ref[...]
        tf = cw.shape[1]
        r8 = lax.broadcasted_iota(jnp.int32, (8, tf), 0)
        rc = min(tm, FFN_CONV_ROWS)
        for bi in range(nb):
            base = bi * tm
            prev = co_ref[bi, jb]
            co_ref[bi, jb] = jnp.where(valid, gub_ref[base + tm - (CONV_W - 1):base + tm, :tf], prev)
            head = jnp.where(r8 == 6, prev[0:1, :], prev[1:2, :])
            for r0 in range(base, base + tm, rc):
                if r0 > base:
                    head = gub_ref[r0 - 8:r0, :tf]
                gc = gub_ref[r0:r0 + rc, :tf]
                ext = jnp.concatenate([head, gc], axis=0)
                g1 = pltpu.roll(ext, 1, axis=0)[8:, :]
                g2 = pltpu.roll(ext, 2, axis=0)[8:, :]
                c = cb + cw[0:1, :] * g2 + cw[1:2, :] * g1 + cw[2:3, :] * gc
                ab_ref[r0:r0 + rc, :] = (c * _sigmoid(c) * gub_ref[r0:r0 + rc, tf:]).astype(BF16)
        gua_ref[...] = jnp.dot(h_ref[...], wgu_ref[0], preferred_element_type=F32)
        acc_ref[...] += jnp.dot(ac_ref[...], wo_ref[...], preferred_element_type=F32)

    @pl.when(s % 2 == 0)
    def _():
        step(gu0_ref, gu1_ref, a1_ref, a0_ref)

    @pl.when(s % 2 == 1)
    def _():
        step(gu1_ref, gu0_ref, a0_ref, a1_ref)

    @pl.when(s == pl.num_programs(2) - 1)
    def _():
        y = x_ref[...].reshape(rows, d) + _rms(acc_ref[...], gout_ref[...])
        o_ref[...] = y.reshape(nb, tm, d)


def _ffn(x, gin, w_in, cw, cb, w_out, gout, conv_state, name):
    b, t, d = x.shape
    f = w_out.shape[0]
    tm = _row_tile(t, FFN_ROW_TILE)
    nb = max(1, min(b, FFN_ROW_TILE // t))
    assert b % nb == 0
    rows = nb * tm
    tf = FFN_COL_TILE
    nf = f // tf
    assert f % tf == 0 and t >= CONV_W - 1 and tm % 8 == 0
    cs = conv_state.reshape(b, CONV_W - 1, nf, tf).transpose(0, 2, 1, 3)
    cs_spec = pl.BlockSpec((nb, nf, CONV_W - 1, tf), lambda bi, ti, s: (bi, 0, 0, 0))
    w_tiles = w_in.reshape(d, 2, nf, tf).transpose(2, 0, 1, 3).reshape(nf, d, 2 * tf)
    col_a = lambda s: jnp.minimum(s, nf - 1)
    col_b = lambda s: jnp.clip(s - 1, 0, nf - 1)
    col_c = lambda s: jnp.clip(s - 2, 0, nf - 1)
    y, co = pl.pallas_call(
        functools.partial(_ffn_kernel, nf=nf),
        out_shape=(jax.ShapeDtypeStruct((b, t, d), F32),
                   jax.ShapeDtypeStruct(cs.shape, F32)),
        grid=(b // nb, t // tm, nf + 2),
        in_specs=[pl.BlockSpec((nb, tm, d), lambda bi, ti, s: (bi, ti, 0)),
                  pl.BlockSpec((1, d), lambda bi, ti, s: (0, 0)),
                  pl.BlockSpec((1, d, 2 * tf), lambda bi, ti, s: (col_a(s), 0, 0)),
                  pl.BlockSpec((CONV_W, tf), lambda bi, ti, s: (0, col_b(s))),
                  pl.BlockSpec((1, tf), lambda bi, ti, s: (0, col_b(s))),
                  pl.BlockSpec((tf, d), lambda bi, ti, s: (col_c(s), 0)),
                  pl.BlockSpec((1, d), lambda bi, ti, s: (0, 0)),
                  cs_spec],
        out_specs=(pl.BlockSpec((nb, tm, d), lambda bi, ti, s: (bi, ti, 0)), cs_spec),
        scratch_shapes=[pltpu.VMEM((rows, d), BF16),
                        pltpu.VMEM((rows, d), F32),
                        pltpu.VMEM((rows, 2 * tf), F32), pltpu.VMEM((rows, 2 * tf), F32),
                        pltpu.VMEM((rows, tf), BF16), pltpu.VMEM((rows, tf), BF16)],
        compiler_params=_cparams("parallel", "arbitrary", "arbitrary"),
        name=name,
    )(x, gin.reshape(1, d), w_tiles, cw, cb.reshape(1, f), w_out,
      gout.reshape(1, d), cs)
    return y, co.transpose(0, 2, 1, 3).reshape(b, CONV_W - 1, f)


def _split3(x):
    hi = x.astype(BF16)
    r1 = x - hi.astype(F32)
    mid = r1.astype(BF16)
    lo = (r1 - mid.astype(F32)).astype(BF16)
    return hi, mid, lo


def _gla_levels(tc):
    return [s for s in (16, 32, 64, 128, 256, 512, 1024) if 2 * s <= tc]


def _gla_level_table(tc):
    r = np.arange(tc)[:, None]
    c = np.arange(tc)[None, :]
    table = np.full((tc, tc), -1, np.int32)
    table[(r // GLA_BLOCK == c // GLA_BLOCK) & (c <= r)] = 0
    for i, s in enumerate(_gla_levels(tc)):
        table[(r // (2 * s) == c // (2 * s)) & ((r // s) % 2 == 1) & ((c // s) % 2 == 0)] = i + 1
    return table


def _gla_kernel(q_ref, f_ref, i_ref, g_ref, ll_ref, l1m_ref, om_ref, gn_ref, s0_ref, lvl_ref,
                o_ref, so_ref, st_ref, bg_ref, fac_ref, qi_ref, ks_ref, v_ref, a_ref, oo_ref):
    t = pl.program_id(1)
    tc = q_ref.shape[1]
    d = q_ref.shape[2]
    levels = _gla_levels(tc)

    @pl.when(t == 0)
    def _():
        for h in range(N_HEADS):
            st_ref[h] = s0_ref[0, h].T

    qh = q_ref[0]
    fp = f_ref[0]
    q = qh * _sigmoid(qh)
    e = jnp.exp(-jnp.abs(fp))
    logsig = jnp.minimum(fp, 0.0) - jnp.log1p(e)
    a = ll_ref[...]
    c = l1m_ref[...] + logsig
    logf = jnp.maximum(a, c) + jnp.log1p(jnp.exp(-jnp.abs(a - c)))
    kk = om_ref[...] * (jnp.where(fp >= 0.0, e, 1.0) / (1.0 + e))

    row = lax.broadcasted_iota(jnp.int32, (tc, tc), 0)
    col = lax.broadcasted_iota(jnp.int32, (tc, tc), 1)
    m_incl = jnp.where(col <= row, 1.0, 0.0).astype(BF16)
    bg = sum(jnp.dot(m_incl, p, preferred_element_type=F32) for p in _split3(logf))
    bg_ref[...] = bg
    b_last = bg[tc - 1:tc, :]

    qi_ref[...] = (q * jnp.exp(bg)).astype(BF16)
    ks_ref[...] = (kk * jnp.exp(b_last - bg)).astype(BF16)
    v_ref[...] = i_ref[0].astype(BF16)

    def ref_rows(span, offset):
        pieces = []
        for a0 in range(0, tc, span):
            r = a0 + offset
            src = bg_ref[r:r + 1, :] if r >= 0 else jnp.zeros((1, d), F32)
            pieces.append(jnp.broadcast_to(src, (span, d)))
        return pieces[0] if len(pieces) == 1 else jnp.concatenate(pieces, axis=0)

    dl = bg - ref_rows(GLA_BLOCK, -1)
    fac_ref[0] = (q * jnp.exp(dl)).astype(BF16)
    fac_ref[1] = (kk * jnp.exp(-dl)).astype(BF16)
    rid = lax.broadcasted_iota(jnp.int32, (tc, 1), 0)
    for li, s in enumerate(levels):
        mid = ref_rows(2 * s, s - 1)
        right = ((rid // s) % 2) == 1
        x = jnp.exp(jnp.where(right, bg - mid, mid - bg))
        fac_ref[2 + 2 * li] = (q * x).astype(BF16)
        fac_ref[3 + 2 * li] = (kk * x).astype(BF16)

    nt = (((1,), (1,)), ((), ()))
    tn = (((0,), (0,)), ((), ()))
    lvl = lvl_ref[...]
    for h in range(N_HEADS):
        cols = slice(h * HEAD_W, (h + 1) * HEAD_W)
        att = jnp.zeros((tc, tc), F32)
        for ci in range(len(levels), -1, -1):
            prod = lax.dot_general(fac_ref[2 * ci, :, cols], fac_ref[2 * ci + 1, :, cols], nt,
                                   preferred_element_type=F32)
            att = jnp.where(lvl == ci, prod, att)
        a_ref[h] = att.astype(BF16)

    for h in range(N_HEADS):
        cols = slice(h * HEAD_W, (h + 1) * HEAD_W)
        vb = v_ref[:, cols]
        s_t = st_ref[h]
        o_inter = lax.dot_general(qi_ref[:, cols], s_t.astype(BF16), nt, preferred_element_type=F32)
        oo_ref[:, cols] = o_inter + jnp.dot(a_ref[h], vb, preferred_element_type=F32)
        upd = lax.dot_general(vb, ks_ref[:, cols], tn, preferred_element_type=F32)
        st_ref[h] = s_t * jnp.exp(b_last[:, cols]) + upd

    gh = g_ref[0]
    gate = gh * _sigmoid(gh)
    for h in range(N_HEADS):
        cols = slice(h * HEAD_W, (h + 1) * HEAD_W)
        oh = _rms(oo_ref[:, cols], gn_ref[:, cols])
        o_ref[0, :, cols] = (oh * gate[:, cols]).astype(o_ref.dtype)

    @pl.when(t == pl.num_programs(1) - 1)
    def _():
        for h in range(N_HEADS):
            so_ref[0, h] = st_ref[h].T


def _gla(proj, log_lower, log1m_lower, one_m_lower, g_out, s0, name):
    b, t, _ = proj.shape
    d = D_MODEL
    tc = _row_tile(t, GLA_TILE)
    assert tc % GLA_BLOCK == 0
    vec = pl.BlockSpec((1, d), lambda bi, ti: (0, 0))
    col_spec = lambda c: pl.BlockSpec((1, tc, d), lambda bi, ti: (bi, ti, c))
    st_spec = pl.BlockSpec((1, N_HEADS, HEAD_W, HEAD_W), lambda bi, ti: (bi, 0, 0, 0))
    n_fac = 2 * (1 + len(_gla_levels(tc)))
    return pl.pallas_call(
        _gla_kernel,
        out_shape=(jax.ShapeDtypeStruct((b, t, d), BF16),
                   jax.ShapeDtypeStruct((b, N_HEADS, HEAD_W, HEAD_W), F32)),
        grid=(b, t // tc),
        in_specs=[col_spec(0), col_spec(1), col_spec(2), col_spec(3),
                  vec, vec, vec, vec, st_spec,
                  pl.BlockSpec((tc, tc), lambda bi, ti: (0, 0))],
        out_specs=(pl.BlockSpec((1, tc, d), lambda bi, ti: (bi, ti, 0)), st_spec),
        scratch_shapes=[pltpu.VMEM((N_HEADS, HEAD_W, HEAD_W), F32),
                        pltpu.VMEM((tc, d), F32),
                        pltpu.VMEM((n_fac, tc, d), BF16),
                        pltpu.VMEM((tc, d), BF16), pltpu.VMEM((tc, d), BF16),
                        pltpu.VMEM((tc, d), BF16),
                        pltpu.VMEM((N_HEADS, tc, tc), BF16),
                        pltpu.VMEM((tc, d), F32)],
        compiler_params=_cparams("parallel", "arbitrary"),
        name=name,
    )(proj, proj, proj, proj, log_lower.reshape(1, d), log1m_lower.reshape(1, d),
      one_m_lower.reshape(1, d), g_out.reshape(1, d), s0, jnp.asarray(_gla_level_table(tc)))


def _t5_bucket(rel):
    half = NUM_BUCKETS // 2
    ret = jnp.where(rel > 0, half, 0)
    n = jnp.abs(rel)
    max_exact = half // 2
    nf = jnp.maximum(n, 1).astype(F32)
    large = max_exact + (jnp.log(nf / max_exact) / math.log(MAX_DISTANCE / max_exact)
                         * (half - max_exact)).astype(jnp.int32)
    large = jnp.minimum(large, half - 1)
    return ret + jnp.where(n < max_exact, n, large)


def _toeplitz_kernel(w_ref, o_ref, *, tq):
    tk = o_ref.shape[2]
    p = w_ref.shape[3]
    x = jnp.broadcast_to(w_ref[0, 0], (tk, p))
    o_ref[0, 0] = pltpu.roll(x, 0, axis=1, stride=1, stride_axis=0)[:, :tq]


def _bias_tiles_t(rel_bias, d0_list, tq, tk):
    p = pl.cdiv(tq + tk, HEAD_W) * HEAD_W
    m = jnp.arange(p, dtype=jnp.int32)
    rel = jnp.asarray(d0_list, jnp.int32)[:, None] + jnp.where(m <= tq, -m, p - m)[None, :]
    onehot = (_t5_bucket(rel)[:, :, None] == jnp.arange(NUM_BUCKETS, dtype=jnp.int32)).astype(F32)
    w = jnp.einsum("dpn,nh->hdp", onehot, rel_bias.astype(F32),
                   precision=lax.Precision.HIGHEST)
    h, nd = w.shape[0], w.shape[1]
    return pl.pallas_call(
        functools.partial(_toeplitz_kernel, tq=tq),
        out_shape=jax.ShapeDtypeStruct((h, nd, tk, tq), F32),
        grid=(h, nd),
        in_specs=[pl.BlockSpec((1, 1, 1, p), lambda hi, di: (hi, di, 0, 0))],
        out_specs=pl.BlockSpec((1, 1, tk, tq), lambda hi, di: (hi, di, 0, 0)),
        compiler_params=_cparams("parallel", "parallel"),
        name="bias_tiles",
    )(w.reshape(h, nd, 1, p))


def _attn_kernel(*refs, segs, q_off, tq, hb):
    nseg = len(segs)
    q_ref, lam_ref, gn_ref = refs[0], refs[1], refs[2]
    seg_refs = [refs[3 + 3 * s: 6 + 3 * s] for s in range(nseg)]
    o_ref = refs[3 + 3 * nseg]
    m_ref, l_ref, acc_ref = refs[4 + 3 * nseg: 7 + 3 * nseg]
    kv_scratch = refs[7 + 3 * nseg:]
    seg_scr = [kv_scratch[4 * s: 4 * s + 4] for s in range(nseg)]

    i = pl.program_id(2)
    q0 = q_off + i * tq

    @pl.when(i == 0)
    def _():
        for (k_ref, v_ref, _), (kb_ref, vt_ref, _, _), (_, seg_len, tk, _, _) in zip(seg_refs, seg_scr, segs):
            kb_ref[...] = k_ref[0].astype(BF16)
            for kt in range(seg_len // tk):
                vt_ref[kt] = v_ref[0, kt * tk:(kt + 1) * tk, :].astype(F32).T.astype(BF16)

    q = q_ref[0]
    lane = lax.broadcasted_iota(jnp.int32, (tq, HEAD_W), 1)
    qz = []
    for hh in range(hb):
        qh = q[:, hh * HEAD_W:(hh + 1) * HEAD_W] * (DA_HEAD_DIM ** -0.5)
        qz.append([jnp.where(lane < DA_HEAD_DIM, qh, 0).astype(BF16),
                   jnp.where(lane >= DA_HEAD_DIM, qh, 0).astype(BF16)])
    m_ref[...] = jnp.full_like(m_ref, NEG_BIG)
    l_ref[...] = jnp.zeros_like(l_ref)
    acc_ref[...] = jnp.zeros_like(acc_ref)
    q_chunk = (q0 + lax.broadcasted_iota(jnp.int32, (1, tq), 1)) // CHUNK
    vis_end = ((q0 + tq - 1) // CHUNK + 1) * CHUNK
    full_end = (q0 // CHUNK + 1) * CHUNK
    nt = (((1,), (1,)), ((), ()))

    for (_, _, b_ref), (kb_ref, vt_ref, sa_ref, sb_ref), (pos0, seg_len, tk, d0_min, d0_step) in zip(
            seg_refs, seg_scr, segs):
        n_tiles = seg_len // tk
        n_vis = jnp.clip((vis_end - pos0 + tk - 1) // tk, 0, n_tiles)
        n_full = jnp.clip((full_end - pos0) // tk, 0, n_vis)

        def scores_into(kt, s_ref, kb_ref=kb_ref, tk=tk, n_tiles=n_tiles):
            r0 = pl.multiple_of(jnp.minimum(kt, n_tiles - 1) * tk, tk)
            for hh in range(hb):
                kb = kb_ref[pl.ds(r0, tk), hh * HEAD_W:(hh + 1) * HEAD_W]
                for c in range(2):
                    s_ref[2 * hh + c] = lax.dot_general(kb, qz[hh][c], nt, preferred_element_type=F32)

        def consume(kt, s_ref, masked, b_ref=b_ref, vt_ref=vt_ref, pos0=pos0, tk=tk,
                    d0_min=d0_min, d0_step=d0_step, n_tiles=n_tiles, n_vis=n_vis):
            ktc = jnp.minimum(kt, n_tiles - 1)
            r0 = ktc * tk
            d_idx = (pos0 + r0 - q0 - d0_min) // d0_step
            if masked:
                k0 = jnp.where(kt < n_vis, pos0 + r0, jnp.int32(1 << 28))
                k_chunk = (k0 + lax.broadcasted_iota(jnp.int32, (tk, 1), 0)) // CHUNK
                visible = k_chunk <= q_chunk
            for hh in range(hb):
                vt = vt_ref[ktc, hh * HEAD_W:(hh + 1) * HEAD_W, :]
                bias = b_ref[hh, d_idx]
                for c in range(2):
                    j = 2 * hh + c
                    s = s_ref[j] + bias
                    if masked:
                        s = jnp.where(visible, s, NEG_BIG)
                    m_prev = m_ref[j]
                    m_new = jnp.maximum(m_prev, jnp.max(s, axis=0, keepdims=True))
                    p = jnp.exp(s - m_new)
                    alpha = jnp.exp(m_prev - m_new)
                    l_ref[j] = alpha * l_ref[j] + jnp.sum(p, axis=0, keepdims=True)
                    acc_ref[j] = alpha * acc_ref[j] + jnp.dot(vt, p.astype(BF16),
                                                              preferred_element_type=F32)
                    m_ref[j] = m_new

        def pair(u, carry, masked, base, sa_ref=sa_ref, sb_ref=sb_ref):
            t0 = base + 2 * u
            scores_into(t0 + 1, sb_ref)
            consume(t0, sa_ref, masked)
            scores_into(t0 + 2, sa_ref)
            consume(t0 + 1, sb_ref, masked)
            return carry

        n_pairs_full = n_full // 2
        base = 2 * n_pairs_full
        scores_into(0, sa_ref)
        lax.fori_loop(0, n_pairs_full, functools.partial(pair, masked=False, base=0), 0)
        lax.fori_loop(0, (n_vis - base + 1) // 2, functools.partial(pair, masked=True, base=base), 0)

    lam = lam_ref[0:1, 0:1]
    for hh in range(hb):
        o_t = (acc_ref[2 * hh] * (1.0 / l_ref[2 * hh])
               - lam * (acc_ref[2 * hh + 1] * (1.0 / l_ref[2 * hh + 1])))
        cols = slice(hh * HEAD_W, (hh + 1) * HEAD_W)
        o_ref[0, :, cols] = (_rms(o_t.T, gn_ref[:, cols]) * lam_ref[0:1, 1:2]).astype(o_ref.dtype)


def _attention(q, segments, rel_bias, lam, out_scale, g_subln, q_off, name):
    b, t_real, d = q.shape
    tq_all = max(t_real, HEAD_W)
    if tq_all != t_real:
        q = jnp.pad(q, ((0, 0), (0, tq_all - t_real), (0, 0)))
    tq = _row_tile(tq_all, ATTN_TILE)
    nq = tq_all // tq
    hb = ATTN_HEADS_PER_STEP
    hw = hb * HEAD_W
    segs, args, in_specs, scratch = [], [], [], []
    for (k, v, pos0, tk) in segments:
        seg_len = k.shape[1]
        assert seg_len % tk == 0
        d0s = sorted({pos0 + kt * tk - (q_off + i * tq)
                      for i in range(nq) for kt in range(seg_len // tk)
                      if pos0 + kt * tk < ((q_off + (i + 1) * tq - 1) // CHUNK + 1) * CHUNK})
        step = math.gcd(tq, tk)
        d0s = list(range(d0s[0], d0s[-1] + 1, step))
        tiles = _bias_tiles_t(rel_bias, d0s, tq, tk)
        segs.append((pos0, seg_len, tk, d0s[0], step))
        args += [k, v, tiles]
        in_specs += [pl.BlockSpec((1, seg_len, hw), lambda bi, h, i: (bi, 0, h)),
                     pl.BlockSpec((1, seg_len, hw), lambda bi, h, i: (bi, 0, h)),
                     pl.BlockSpec((hb, len(d0s), tk, tq), lambda bi, h, i: (h, 0, 0, 0))]
        scratch += [pltpu.VMEM((seg_len, hw), BF16), pltpu.VMEM((seg_len // tk, hw, tk), BF16),
                    pltpu.VMEM((2 * hb, tk, tq), F32), pltpu.VMEM((2 * hb, tk, tq), F32)]
    scal = jnp.zeros((1, HEAD_W), F32).at[0, 0].set(lam).at[0, 1].set(out_scale)
    kern = functools.partial(_attn_kernel, segs=tuple(segs), q_off=q_off, tq=tq, hb=hb)
    out = pl.pallas_call(
        kern,
        out_shape=jax.ShapeDtypeStruct((b, tq_all, d), BF16),
        grid=(b, N_HEADS // hb, nq),
        in_specs=[pl.BlockSpec((1, tq, hw), lambda bi, h, i: (bi, i, h)),
                  pl.BlockSpec((1, HEAD_W), lambda bi, h, i: (0, 0)),
                  pl.BlockSpec((1, hw), lambda bi, h, i: (0, h))] + in_specs,
        out_specs=pl.BlockSpec((1, tq, hw), lambda bi, h, i: (bi, i, h)),
        scratch_shapes=[pltpu.VMEM((2 * hb, 1, tq), F32), pltpu.VMEM((2 * hb, 1, tq), F32),
                        pltpu.VMEM((2 * hb, HEAD_W, tq), F32)] + scratch,
        compiler_params=_cparams("parallel", "parallel", "arbitrary"),
        name=name,
    )(q, scal, g_subln.reshape(1, d), *args)
    return out[:, :t_real]


def _trunk(x, ple, q_off, past_k, past_v, hg_s0, conv_s0, wts, tag):
    (g_norms, w_in_a, lower, g_hg, w_out_a, g_kv, w_kv, rel_bias, w_q_b, lam_b,
     g_subln, w_out_b, w_ffn_in, conv_w, conv_b, w_ffn_out, w_ple, w_ple_gate) = wts
    b, t, d = x.shape
    depth = g_norms.shape[0]
    n_a = w_in_a.shape[0]
    m = b * t
    hg_out, conv_out = [], []
    k_new = v_new = None
    segments = None
    for i in range(depth):
        nm = f"{tag}{i}"
        x2 = x.reshape(m, d)
        if i < n_a:
            lw = lower[i]
            proj = _norm_matmul(x2, g_norms[i, 0], w_in_a[i], F32, nm + "_hg_in")
            o, s = _gla(proj.reshape(b, t, 4 * d), jnp.log(lw), jnp.log1p(-lw), 1.0 - lw,
                        g_hg[i], hg_s0[i], nm + "_gla")
            hg_out.append(s)
            x2 = _matmul_norm_res(o.reshape(m, d), w_out_a[i], g_norms[i, 1], x2, nm + "_hg_out")
        else:
            j = i - n_a
            q = _norm_matmul(x2, g_norms[i, 0], w_q_b[j], BF16, nm + "_q")
            lam_init = 0.8 - 0.6 * math.exp(-0.3 * i)
            lp = lam_b[j].astype(F32)
            lam = jnp.exp(jnp.sum(lp[0] * lp[1])) - jnp.exp(jnp.sum(lp[2] * lp[3])) + lam_init
            o = _attention(q.reshape(b, t, d), segments, rel_bias, lam, 1.0 - lam_init,
                           g_subln[j], q_off, nm + "_attn")
            x2 = _matmul_norm_res(o.reshape(m, d), w_out_b[j], g_norms[i, 1], x2, nm + "_attn_out")
        x3, cbuf = _ffn(x2.reshape(b, t, d), g_norms[i, 2], w_ffn_in[i], conv_w[i], conv_b[i],
                        w_ffn_out[i], g_norms[i, 3], conv_s0[i], nm + "_ffn")
        conv_out.append(cbuf)
        x2 = _ple(x3.reshape(m, d), ple[i].reshape(m, -1), w_ple_gate[i], w_ple[i], nm + "_ple")
        x = x2.reshape(b, t, d)
        if i == n_a - 1:
            k_new, v_new = _norm_kv(x2, g_kv, w_kv, nm + "_kv")
            k_new = k_new.reshape(b, t, d)
            v_new = v_new.reshape(b, t, d)
            segments = []
            if past_k is not None:
                tp = past_k.shape[1]
                segments.append((past_k.reshape(b, tp, d), past_v.reshape(b, tp, d), 0, tp))
            segments.append((k_new, v_new, q_off, min(t, ATTN_KEY_TILE)))
    hd = (b, t, N_HEADS, HEAD_W)
    return x, k_new.reshape(hd), v_new.reshape(hd), jnp.stack(hg_out), jnp.stack(conv_out)


def kernel(x_prompt, x_sample, cache_k, cache_v, state_hgrn, state_conv, p_prompt, p_sample,
           g_norms, w_in_a, lb_raw, g_hg, w_out_a, g_kv, w_kv, rel_bias,
           w_q_b, lam_b, g_subln, w_out_b, w_ffn_in, conv_w, conv_b, w_ffn_out,
           w_ple, w_ple_gate):
    sm = jax.nn.softmax(lb_raw.astype(F32), axis=0)
    cs = jnp.cumsum(sm, axis=0)
    lower = cs - cs[0:1]
    bf = lambda w: w.astype(BF16)
    wts = (g_norms, bf(w_in_a), lower, g_hg, bf(w_out_a), g_kv, bf(w_kv), rel_bias,
           bf(w_q_b), lam_b, g_subln, bf(w_out_b), bf(w_ffn_in), conv_w, conv_b,
           bf(w_ffn_out), bf(w_ple), bf(w_ple_gate))
    bp, tp, _ = x_prompt.shape
    n_a, depth = w_in_a.shape[0], g_norms.shape[0]
    hg0 = jnp.zeros((n_a, bp) + state_hgrn.shape[2:], F32)
    cv0 = jnp.zeros((depth, bp) + state_conv.shape[2:], F32)
    y_p, k_p, v_p, hg_p, cv_p = _trunk(x_prompt, p_prompt, 0, None, None, hg0, cv0, wts, "p")
    y_s, k_s, v_s, hg_s, cv_s = _trunk(x_sample, p_sample, cache_k.shape[1], cache_k, cache_v,
                                       state_hgrn, state_conv, wts, "s")
    return (y_p, y_s, k_p, v_p, k_s, v_s, hg_p, hg_s, cv_p, cv_s)
```

```python
import functools
import math

import numpy as np
import jax
import jax.numpy as jnp
from jax import lax
from jax.experimental import pallas as pl
from jax.experimental.pallas import tpu as pltpu

F32 = jnp.float32
BF16 = jnp.bfloat16

D_MODEL = 1024
N_HEADS = 8
HEAD_W = D_MODEL // N_HEADS
DA_HEAD_DIM = HEAD_W // 2
CHUNK = 64
GLA_BLOCK = 16
NUM_BUCKETS = 32
MAX_DISTANCE = 256
D_FF = 2816
CONV_W = 3
EPS = 1e-6
NEG_BIG = -1e30

V7X_VMEM_LIMIT_BYTES = 56 * 1024 * 1024

ROW_TILE = 512
FFN_ROW_TILE = 1024
FFN_COL_TILE = 256
FFN_CONV_ROWS = 128
FFN_EDGE_ROWS = 256
GLA_TILE = 256
ATTN_TILE = 512
ATTN_KEY_TILE = 256
ATTN_HEADS_PER_STEP = 2


def _cparams(*sem):
    return pltpu.CompilerParams(dimension_semantics=sem,
                                vmem_limit_bytes=V7X_VMEM_LIMIT_BYTES)


def _rms(x, g):
    ms = jnp.mean(x * x, axis=-1, keepdims=True)
    return x * lax.rsqrt(ms + EPS) * g


def _sigmoid(z):
    return 1.0 / (1.0 + jnp.exp(-z))


def _row_tile(m, cap):
    t = min(m, cap)
    assert m % t == 0, (m, t)
    return t


def _norm_matmul_kernel(x_ref, g_ref, w_ref, o_ref):
    h = _rms(x_ref[...], g_ref[...]).astype(BF16)
    n = w_ref.shape[1]
    tn = min(n, D_MODEL)
    for n0 in range(0, n, tn):
        o_ref[:, n0:n0 + tn] = jnp.dot(h, w_ref[:, n0:n0 + tn],
                                       preferred_element_type=F32).astype(o_ref.dtype)


def _norm_matmul(x2d, g, w, out_dtype, name):
    m, d = x2d.shape
    n = w.shape[1]
    tm = _row_tile(m, ROW_TILE)
    return pl.pallas_call(
        _norm_matmul_kernel,
        out_shape=jax.ShapeDtypeStruct((m, n), out_dtype),
        grid=(m // tm,),
        in_specs=[pl.BlockSpec((tm, d), lambda i: (i, 0)),
                  pl.BlockSpec((1, d), lambda i: (0, 0)),
                  pl.BlockSpec((d, n), lambda i: (0, 0))],
        out_specs=pl.BlockSpec((tm, n), lambda i: (i, 0)),
        compiler_params=_cparams("parallel"),
        name=name,
    )(x2d, g.reshape(1, d), w)


def _norm_kv_kernel(x_ref, g_ref, w_ref, k_ref, v_ref):
    d = k_ref.shape[1]
    h = _rms(x_ref[...], g_ref[...]).astype(BF16)
    k_ref[...] = jnp.dot(h, w_ref[:, :d], preferred_element_type=F32)
    v_ref[...] = jnp.dot(h, w_ref[:, d:], preferred_element_type=F32)


def _norm_kv(x2d, g, w, name):
    m, d = x2d.shape
    tm = _row_tile(m, ROW_TILE)
    out = jax.ShapeDtypeStruct((m, d), F32)
    return pl.pallas_call(
        _norm_kv_kernel,
        out_shape=(out, out),
        grid=(m // tm,),
        in_specs=[pl.BlockSpec((tm, d), lambda i: (i, 0)),
                  pl.BlockSpec((1, d), lambda i: (0, 0)),
                  pl.BlockSpec((d, 2 * d), lambda i: (0, 0))],
        out_specs=(pl.BlockSpec((tm, d), lambda i: (i, 0)),
                   pl.BlockSpec((tm, d), lambda i: (i, 0))),
        compiler_params=_cparams("parallel"),
        name=name,
    )(x2d, g.reshape(1, d), w)


def _ffn_kernel(x_ref, a_ref, wm_ref, gm_ref, gin_ref, wgu_ref, cw_ref, cb_ref, wo_ref, gout_ref,
                cs_ref, p_ref, wgate_ref, wple_ref,
                o_ref, co_ref, xm_ref, h_ref, acc_ref, gu_ref, act_ref):
    t = pl.program_id(1)
    j = pl.program_id(2)
    nb, tm, d = x_ref.shape
    rows = nb * tm
    rb = min(rows, FFN_EDGE_ROWS)

    def chunk(ref, r0):
        return ref[0, r0:r0 + rb, :] if nb == 1 else ref[...].reshape(rows, ref.shape[2])

    @pl.when(j == 0)
    def _():
        for r0 in range(0, rows, rb):
            m = jnp.dot(chunk(a_ref, r0), wm_ref[...], preferred_element_type=F32)
            xm = chunk(x_ref, r0) + _rms(m, gm_ref[...])
            xm_ref[r0:r0 + rb, :] = xm
            h_ref[r0:r0 + rb, :] = _rms(xm, gin_ref[...]).astype(BF16)
        acc_ref[...] = jnp.zeros_like(acc_ref)

    @pl.when((j == 0) & (t == 0))
    def _():
        co_ref[...] = cs_ref[...]

    gu_ref[...] = jnp.dot(h_ref[...], wgu_ref[0], preferred_element_type=F32)
    cw = cw_ref[...]
    cb = cb_ref[...]
    tf = cw.shape[1]
    r8 = lax.broadcasted_iota(jnp.int32, (8, tf), 0)
    rc = min(tm, FFN_CONV_ROWS)
    for bi in range(nb):
        base = bi * tm
        prev = co_ref[bi, j]
        co_ref[bi, j] = gu_ref[base + tm - (CONV_W - 1):base + tm, :tf]
        head = jnp.where(r8 == 6, prev[0:1, :], prev[1:2, :])
        for r0 in range(base, base + tm, rc):
            if r0 > base:
                head = gu_ref[r0 - 8:r0, :tf]
            gc = gu_ref[r0:r0 + rc, :tf]
            ext = jnp.concatenate([head, gc], axis=0)
            g1 = pltpu.roll(ext, 1, axis=0)[8:, :]
            g2 = pltpu.roll(ext, 2, axis=0)[8:, :]
            c = cb + cw[0:1, :] * g2 + cw[1:2, :] * g1 + cw[2:3, :] * gc
            act_ref[r0:r0 + rc, :] = (c * _sigmoid(c) * gu_ref[r0:r0 + rc, tf:]).astype(BF16)
    acc_ref[...] += jnp.dot(act_ref[...], wo_ref[...], preferred_element_type=F32)

    @pl.when(j == pl.num_programs(2) - 1)
    def _():
        for r0 in range(0, rows, rb):
            y = xm_ref[r0:r0 + rb, :] + _rms(acc_ref[r0:r0 + rb, :], gout_ref[...])
            z = jnp.dot(y.astype(BF16), wgate_ref[...], preferred_element_type=F32)
            e = jnp.dot(chunk(p_ref, r0).astype(BF16), wple_ref[...], preferred_element_type=F32)
            out = y + _sigmoid(z) * e
            if nb == 1:
                o_ref[0, r0:r0 + rb, :] = out
            else:
                o_ref[...] = out.reshape(nb, tm, d)


def _ffn(x, a, w_mix, g_mix, gin, w_in, cw, cb, w_out, gout, conv_state, p, w_gate, w_ple, name):
    b, t, d = x.shape
    f = w_out.shape[0]
    pd = p.shape[2]
    tm = _row_tile(t, FFN_ROW_TILE)
    nb = max(1, min(b, FFN_ROW_TILE // t))
    assert b % nb == 0
    rows = nb * tm
    assert rows % min(rows, FFN_EDGE_ROWS) == 0 and (nb == 1 or rows <= FFN_EDGE_ROWS)
    tf = FFN_COL_TILE
    nf = f // tf
    assert f % tf == 0 and t >= CONV_W - 1 and tm % 8 == 0
    cs = conv_state.reshape(b, CONV_W - 1, nf, tf).transpose(0, 2, 1, 3)
    cs_spec = pl.BlockSpec((nb, nf, CONV_W - 1, tf), lambda bi, ti, j: (bi, 0, 0, 0))
    w_tiles = w_in.reshape(d, 2, nf, tf).transpose(2, 0, 1, 3).reshape(nf, d, 2 * tf)
    row_spec = lambda width: pl.BlockSpec((nb, tm, width), lambda bi, ti, j: (bi, ti, 0))
    vec_spec = pl.BlockSpec((1, d), lambda bi, ti, j: (0, 0))
    const_spec = lambda shape: pl.BlockSpec(shape, lambda bi, ti, j: (0, 0),
                                            pipeline_mode=pl.Buffered(1))
    y, co = pl.pallas_call(
        _ffn_kernel,
        out_shape=(jax.ShapeDtypeStruct((b, t, d), F32),
                   jax.ShapeDtypeStruct(cs.shape, F32)),
        grid=(b // nb, t // tm, nf),
        in_specs=[row_spec(d), row_spec(d), const_spec((d, d)), vec_spec,
                  vec_spec,
                  pl.BlockSpec((1, d, 2 * tf), lambda bi, ti, j: (j, 0, 0)),
                  pl.BlockSpec((CONV_W, tf), lambda bi, ti, j: (0, j)),
                  pl.BlockSpec((1, tf), lambda bi, ti, j: (0, j)),
                  pl.BlockSpec((tf, d), lambda bi, ti, j: (j, 0)),
                  vec_spec,
                  cs_spec, row_spec(pd), const_spec((d, d)), const_spec((pd, d))],
        out_specs=(row_spec(d), cs_spec),
        scratch_shapes=[pltpu.VMEM((rows, d), F32),
                        pltpu.VMEM((rows, d), BF16),
                        pltpu.VMEM((rows, d), F32),
                        pltpu.VMEM((rows, 2 * tf), F32),
                        pltpu.VMEM((rows, tf), BF16)],
        compiler_params=_cparams("parallel", "arbitrary", "arbitrary"),
        name=name,
    )(x, a, w_mix, g_mix.reshape(1, d), gin.reshape(1, d), w_tiles, cw, cb.reshape(1, f), w_out,
      gout.reshape(1, d), cs, p, w_gate, w_ple)
    return y, co.transpose(0, 2, 1, 3).reshape(b, CONV_W - 1, f)


def _split3(x):
    hi = x.astype(BF16)
    r1 = x - hi.astype(F32)
    mid = r1.astype(BF16)
    lo = (r1 - mid.astype(F32)).astype(BF16)
    return hi, mid, lo


def _gla_levels(tc):
    return [s for s in (16, 32, 64, 128, 256, 512, 1024) if 2 * s <= tc]


def _gla_level_table(tc):
    r = np.arange(tc)[:, None]
    c = np.arange(tc)[None, :]
    table = np.full((tc, tc), -1, np.int32)
    table[(r // GLA_BLOCK == c // GLA_BLOCK) & (c <= r)] = 0
    for i, s in enumerate(_gla_levels(tc)):
        table[(r // (2 * s) == c // (2 * s)) & ((r // s) % 2 == 1) & ((c // s) % 2 == 0)] = i + 1
    return table


def _gla_kernel(q_ref, f_ref, i_ref, g_ref, ll_ref, l1m_ref, om_ref, gn_ref, s0_ref, lvl_ref,
                o_ref, so_ref, st_ref, bg_ref, fac_ref, qi_ref, ks_ref, v_ref, a_ref, oo_ref):
    t = pl.program_id(1)
    tc = q_ref.shape[1]
    d = q_ref.shape[2]
    levels = _gla_levels(tc)

    @pl.when(t == 0)
    def _():
        for h in range(N_HEADS):
            st_ref[h] = s0_ref[0, h].T

    qh = q_ref[0]
    fp = f_ref[0]
    q = qh * _sigmoid(qh)
    e = jnp.exp(-jnp.abs(fp))
    logsig = jnp.minimum(fp, 0.0) - jnp.log1p(e)
    a = ll_ref[...]
    c = l1m_ref[...] + logsig
    logf = jnp.maximum(a, c) + jnp.log1p(jnp.exp(-jnp.abs(a - c)))
    kk = om_ref[...] * (jnp.where(fp >= 0.0, e, 1.0) / (1.0 + e))

    row = lax.broadcasted_iota(jnp.int32, (tc, tc), 0)
    col = lax.broadcasted_iota(jnp.int32, (tc, tc), 1)
    m_incl = jnp.where(col <= row, 1.0, 0.0).astype(BF16)
    bg = sum(jnp.dot(m_incl, p, preferred_element_type=F32) for p in _split3(logf))
    bg_ref[...] = bg
    b_last = bg[tc - 1:tc, :]

    qi_ref[...] = (q * jnp.exp(bg)).astype(BF16)
    ks_ref[...] = (kk * jnp.exp(b_last - bg)).astype(BF16)
    v_ref[...] = i_ref[0].astype(BF16)

    def ref_rows(span, offset):
        pieces = []
        for a0 in range(0, tc, span):
            r = a0 + offset
            src = bg_ref[r:r + 1, :] if r >= 0 else jnp.zeros((1, d), F32)
            pieces.append(jnp.broadcast_to(src, (span, d)))
        return pieces[0] if len(pieces) == 1 else jnp.concatenate(pieces, axis=0)

    dl = bg - ref_rows(GLA_BLOCK, -1)
    fac_ref[0] = (q * jnp.exp(dl)).astype(BF16)
    fac_ref[1] = (kk * jnp.exp(-dl)).astype(BF16)
    rid = lax.broadcasted_iota(jnp.int32, (tc, 1), 0)
    for li, s in enumerate(levels):
        mid = ref_rows(2 * s, s - 1)
        right = ((rid // s) % 2) == 1
        x = jnp.exp(jnp.where(right, bg - mid, mid - bg))
        fac_ref[2 + 2 * li] = (q * x).astype(BF16)
        fac_ref[3 + 2 * li] = (kk * x).astype(BF16)

    nt = (((1,), (1,)), ((), ()))
    tn = (((0,), (0,)), ((), ()))
    lvl = lvl_ref[...]
    for h in range(N_HEADS):
        cols = slice(h * HEAD_W, (h + 1) * HEAD_W)
        att = jnp.zeros((tc, tc), F32)
        for ci in range(len(levels), -1, -1):
            prod = lax.dot_general(fac_ref[2 * ci, :, cols], fac_ref[2 * ci + 1, :, cols], nt,
                                   preferred_element_type=F32)
            att = jnp.where(lvl == ci, prod, att)
        a_ref[h] = att.astype(BF16)

    for h in range(N_HEADS):
        cols = slice(h * HEAD_W, (h + 1) * HEAD_W)
        vb = v_ref[:, cols]
        s_t = st_ref[h]
        o_inter = lax.dot_general(qi_ref[:, cols], s_t.astype(BF16), nt, preferred_element_type=F32)
        oo_ref[:, cols] = o_inter + jnp.dot(a_ref[h], vb, preferred_element_type=F32)
        upd = lax.dot_general(vb, ks_ref[:, cols], tn, preferred_element_type=F32)
        st_ref[h] = s_t * jnp.exp(b_last[:, cols]) + upd

    gh = g_ref[0]
    gate = gh * _sigmoid(gh)
    for h in range(N_HEADS):
        cols = slice(h * HEAD_W, (h + 1) * HEAD_W)
        oh = _rms(oo_ref[:, cols], gn_ref[:, cols])
        o_ref[0, :, cols] = (oh * gate[:, cols]).astype(o_ref.dtype)

    @pl.when(t == pl.num_programs(1) - 1)
    def _():
        for h in range(N_HEADS):
            so_ref[0, h] = st_ref[h].T


def _gla(proj, log_lower, log1m_lower, one_m_lower, g_out, s0, name):
    b, t, _ = proj.shape
    d = D_MODEL
    tc = _row_tile(t, GLA_TILE)
    assert tc % GLA_BLOCK == 0
    vec = pl.BlockSpec((1, d), lambda bi, ti: (0, 0))
    col_spec = lambda c: pl.BlockSpec((1, tc, d), lambda bi, ti: (bi, ti, c))
    st_spec = pl.BlockSpec((1, N_HEADS, HEAD_W, HEAD_W), lambda bi, ti: (bi, 0, 0, 0))
    n_fac = 2 * (1 + len(_gla_levels(tc)))
    return pl.pallas_call(
        _gla_kernel,
        out_shape=(jax.ShapeDtypeStruct((b, t, d), BF16),
                   jax.ShapeDtypeStruct((b, N_HEADS, HEAD_W, HEAD_W), F32)),
        grid=(b, t // tc),
        in_specs=[col_spec(0), col_spec(1), col_spec(2), col_spec(3),
                  vec, vec, vec, vec, st_spec,
                  pl.BlockSpec((tc, tc), lambda bi, ti: (0, 0))],
        out_specs=(pl.BlockSpec((1, tc, d), lambda bi, ti: (bi, ti, 0)), st_spec),
        scratch_shapes=[pltpu.VMEM((N_HEADS, HEAD_W, HEAD_W), F32),
                        pltpu.VMEM((tc, d), F32),
                        pltpu.VMEM((n_fac, tc, d), BF16),
                        pltpu.VMEM((tc, d), BF16), pltpu.VMEM((tc, d), BF16),
                        pltpu.VMEM((tc, d), BF16),
                        pltpu.VMEM((N_HEADS, tc, tc), BF16),
                        pltpu.VMEM((tc, d), F32)],
        compiler_params=_cparams("parallel", "arbitrary"),
        name=name,
    )(proj, proj, proj, proj, log_lower.reshape(1, d), log1m_lower.reshape(1, d),
      one_m_lower.reshape(1, d), g_out.reshape(1, d), s0, jnp.asarray(_gla_level_table(tc)))


def _t5_bucket(rel):
    half = NUM_BUCKETS // 2
    ret = jnp.where(rel > 0, half, 0)
    n = jnp.abs(rel)
    max_exact = half // 2
    nf = jnp.maximum(n, 1).astype(F32)
    large = max_exact + (jnp.log(nf / max_exact) / math.log(MAX_DISTANCE / max_exact)
                         * (half - max_exact)).astype(jnp.int32)
    large = jnp.minimum(large, half - 1)
    return ret + jnp.where(n < max_exact, n, large)


def _toeplitz_kernel(w_ref, o_ref, *, tq):
    tk = o_ref.shape[2]
    p = w_ref.shape[3]
    x = jnp.broadcast_to(w_ref[0, 0], (tk, p))
    o_ref[0, 0] = pltpu.roll(x, 0, axis=1, stride=1, stride_axis=0)[:, :tq]


def _bias_tiles_t(rel_bias, d0_list, tq, tk):
    p = pl.cdiv(tq + tk, HEAD_W) * HEAD_W
    m = jnp.arange(p, dtype=jnp.int32)
    rel = jnp.asarray(d0_list, jnp.int32)[:, None] + jnp.where(m <= tq, -m, p - m)[None, :]
    onehot = (_t5_bucket(rel)[:, :, None] == jnp.arange(NUM_BUCKETS, dtype=jnp.int32)).astype(F32)
    w = jnp.einsum("dpn,nh->hdp", onehot, rel_bias.astype(F32),
                   precision=lax.Precision.HIGHEST)
    h, nd = w.shape[0], w.shape[1]
    return pl.pallas_call(
        functools.partial(_toeplitz_kernel, tq=tq),
        out_shape=jax.ShapeDtypeStruct((h, nd, tk, tq), F32),
        grid=(h, nd),
        in_specs=[pl.BlockSpec((1, 1, 1, p), lambda hi, di: (hi, di, 0, 0))],
        out_specs=pl.BlockSpec((1, 1, tk, tq), lambda hi, di: (hi, di, 0, 0)),
        compiler_params=_cparams("parallel", "parallel"),
        name="bias_tiles",
    )(w.reshape(h, nd, 1, p))


def _attn_kernel(*refs, segs, q_off, tq, hb):
    nseg = len(segs)
    q_ref, lam_ref, gn_ref = refs[0], refs[1], refs[2]
    seg_refs = [refs[3 + 3 * s: 6 + 3 * s] for s in range(nseg)]
    o_ref = refs[3 + 3 * nseg]
    m_ref, l_ref, acc_ref = refs[4 + 3 * nseg: 7 + 3 * nseg]
    kv_scratch = refs[7 + 3 * nseg:]
    seg_scr = [kv_scratch[4 * s: 4 * s + 4] for s in range(nseg)]

    i = pl.program_id(2)
    q0 = q_off + i * tq

    @pl.when(i == 0)
    def _():
        for (k_ref, v_ref, _), (kb_ref, vt_ref, _, _), (_, seg_len, tk, _, _) in zip(seg_refs, seg_scr, segs):
            kb_ref[...] = k_ref[0].astype(BF16)
            for kt in range(seg_len // tk):
                vt_ref[kt] = v_ref[0, kt * tk:(kt + 1) * tk, :].astype(F32).T.astype(BF16)

    q = q_ref[0]
    lane = lax.broadcasted_iota(jnp.int32, (tq, HEAD_W), 1)
    qz = []
    for hh in range(hb):
        qh = q[:, hh * HEAD_W:(hh + 1) * HEAD_W] * (DA_HEAD_DIM ** -0.5)
        qz.append([jnp.where(lane < DA_HEAD_DIM, qh, 0).astype(BF16),
                   jnp.where(lane >= DA_HEAD_DIM, qh, 0).astype(BF16)])
    m_ref[...] = jnp.full_like(m_ref, NEG_BIG)
    l_ref[...] = jnp.zeros_like(l_ref)
    acc_ref[...] = jnp.zeros_like(acc_ref)
    q_chunk = (q0 + lax.broadcasted_iota(jnp.int32, (1, tq), 1)) // CHUNK
    vis_end = ((q0 + tq - 1) // CHUNK + 1) * CHUNK
    full_end = (q0 // CHUNK + 1) * CHUNK
    nt = (((1,), (1,)), ((), ()))

    for (_, _, b_ref), (kb_ref, vt_ref, sa_ref, sb_ref), (pos0, seg_len, tk, d0_min, d0_step) in zip(
            seg_refs, seg_scr, segs):
        n_tiles = seg_len // tk
        n_vis = jnp.clip((vis_end - pos0 + tk - 1) // tk, 0, n_tiles)
        n_full = jnp.clip((full_end - pos0) // tk, 0, n_vis)

        def scores_into(kt, s_ref, kb_ref=kb_ref, tk=tk, n_tiles=n_tiles):
            r0 = pl.multiple_of(jnp.minimum(kt, n_tiles - 1) * tk, tk)
            for hh in range(hb):
                kb = kb_ref[pl.ds(r0, tk), hh * HEAD_W:(hh + 1) * HEAD_W]
                for c in range(2):
                    s_ref[2 * hh + c] = lax.dot_general(kb, qz[hh][c], nt, preferred_element_type=F32)

        def consume(kt, s_ref, masked, b_ref=b_ref, vt_ref=vt_ref, pos0=pos0, tk=tk,
                    d0_min=d0_min, d0_step=d0_step, n_tiles=n_tiles, n_vis=n_vis):
            ktc = jnp.minimum(kt, n_tiles - 1)
            r0 = ktc * tk
            d_idx = (pos0 + r0 - q0 - d0_min) // d0_step
            if masked:
                k0 = jnp.where(kt < n_vis, pos0 + r0, jnp.int32(1 << 28))
                k_chunk = (k0 + lax.broadcasted_iota(jnp.int32, (tk, 1), 0)) // CHUNK
                visible = k_chunk <= q_chunk
            for hh in range(hb):
                vt = vt_ref[ktc, hh * HEAD_W:(hh + 1) * HEAD_W, :]
                bias = b_ref[hh, d_idx]
                for c in range(2):
                    j = 2 * hh + c
                    s = s_ref[j] + bias
                    if masked:
                        s = jnp.where(visible, s, NEG_BIG)
                    m_prev = m_ref[j]
                    m_new = jnp.maximum(m_prev, jnp.max(s, axis=0, keepdims=True))
                    p = jnp.exp(s - m_new)
                    alpha = jnp.exp(m_prev - m_new)
                    l_ref[j] = alpha * l_ref[j] + jnp.sum(p, axis=0, keepdims=True)
                    acc_ref[j] = alpha * acc_ref[j] + jnp.dot(vt, p.astype(BF16),
                                                              preferred_element_type=F32)
                    m_ref[j] = m_new

        def pair(u, carry, masked, base, sa_ref=sa_ref, sb_ref=sb_ref):
            t0 = base + 2 * u
            scores_into(t0 + 1, sb_ref)
            consume(t0, sa_ref, masked)
            scores_into(t0 + 2, sa_ref)
            consume(t0 + 1, sb_ref, masked)
            return carry

        n_pairs_full = n_full // 2
        base = 2 * n_pairs_full
        scores_into(0, sa_ref)
        lax.fori_loop(0, n_pairs_full, functools.partial(pair, masked=False, base=0), 0)
        lax.fori_loop(0, (n_vis - base + 1) // 2, functools.partial(pair, masked=True, base=base), 0)

    lam = lam_ref[0:1, 0:1]
    for hh in range(hb):
        o_t = (acc_ref[2 * hh] * (1.0 / l_ref[2 * hh])
               - lam * (acc_ref[2 * hh + 1] * (1.0 / l_ref[2 * hh + 1])))
        cols = slice(hh * HEAD_W, (hh + 1) * HEAD_W)
        o_ref[0, :, cols] = (_rms(o_t.T, gn_ref[:, cols]) * lam_ref[0:1, 1:2]).astype(o_ref.dtype)


def _attention(q, segments, rel_bias, lam, out_scale, g_subln, q_off, name):
    b, t_real, d = q.shape
    tq_all = max(t_real, HEAD_W)
    if tq_all != t_real:
        q = jnp.pad(q, ((0, 0), (0, tq_all - t_real), (0, 0)))
    tq = _row_tile(tq_all, ATTN_TILE)
    nq = tq_all // tq
    hb = ATTN_HEADS_PER_STEP
    hw = hb * HEAD_W
    segs, args, in_specs, scratch = [], [], [], []
    for (k, v, pos0, tk) in segments:
        seg_len = k.shape[1]
        assert seg_len % tk == 0
        d0s = sorted({pos0 + kt * tk - (q_off + i * tq)
                      for i in range(nq) for kt in range(seg_len // tk)
                      if pos0 + kt * tk < ((q_off + (i + 1) * tq - 1) // CHUNK + 1) * CHUNK})
        step = math.gcd(tq, tk)
        d0s = list(range(d0s[0], d0s[-1] + 1, step))
        tiles = _bias_tiles_t(rel_bias, d0s, tq, tk)
        segs.append((pos0, seg_len, tk, d0s[0], step))
        args += [k, v, tiles]
        in_specs += [pl.BlockSpec((1, seg_len, hw), lambda bi, h, i: (bi, 0, h)),
                     pl.BlockSpec((1, seg_len, hw), lambda bi, h, i: (bi, 0, h)),
                     pl.BlockSpec((hb, len(d0s), tk, tq), lambda bi, h, i: (h, 0, 0, 0))]
        scratch += [pltpu.VMEM((seg_len, hw), BF16), pltpu.VMEM((seg_len // tk, hw, tk), BF16),
                    pltpu.VMEM((2 * hb, tk, tq), F32), pltpu.VMEM((2 * hb, tk, tq), F32)]
    scal = jnp.zeros((1, HEAD_W), F32).at[0, 0].set(lam).at[0, 1].set(out_scale)
    kern = functools.partial(_attn_kernel, segs=tuple(segs), q_off=q_off, tq=tq, hb=hb)
    out = pl.pallas_call(
        kern,
        out_shape=jax.ShapeDtypeStruct((b, tq_all, d), BF16),
        grid=(b, N_HEADS // hb, nq),
        in_specs=[pl.BlockSpec((1, tq, hw), lambda bi, h, i: (bi, i, h)),
                  pl.BlockSpec((1, HEAD_W), lambda bi, h, i: (0, 0)),
                  pl.BlockSpec((1, hw), lambda bi, h, i: (0, h))] + in_specs,
        out_specs=pl.BlockSpec((1, tq, hw), lambda bi, h, i: (bi, i, h)),
        scratch_shapes=[pltpu.VMEM((2 * hb, 1, tq), F32), pltpu.VMEM((2 * hb, 1, tq), F32),
                        pltpu.VMEM((2 * hb, HEAD_W, tq), F32)] + scratch,
        compiler_params=_cparams("parallel", "parallel", "arbitrary"),
        name=name,
    )(q, scal, g_subln.reshape(1, d), *args)
    return out[:, :t_real]


def _trunk(x, ple, q_off, past_k, past_v, hg_s0, conv_s0, wts, tag):
    (g_norms, w_in_a, lower, g_hg, w_out_a, g_kv, w_kv, rel_bias, w_q_b, lam_b,
     g_subln, w_out_b, w_ffn_in, conv_w, conv_b, w_ffn_out, w_ple, w_ple_gate) = wts
    b, t, d = x.shape
    depth = g_norms.shape[0]
    n_a = w_in_a.shape[0]
    m = b * t
    hg_out, conv_out = [], []
    k_new = v_new = None
    segments = None
    for i in range(depth):
        nm = f"{tag}{i}"
        x2 = x.reshape(m, d)
        if i < n_a:
            lw = lower[i]
            proj = _norm_matmul(x2, g_norms[i, 0], w_in_a[i], F32, nm + "_hg_in")
            o, s = _gla(proj.reshape(b, t, 4 * d), jnp.log(lw), jnp.log1p(-lw), 1.0 - lw,
                        g_hg[i], hg_s0[i], nm + "_gla")
            hg_out.append(s)
            w_mix = w_out_a[i]
        else:
            j = i - n_a
            q = _norm_matmul(x2, g_norms[i, 0], w_q_b[j], BF16, nm + "_q")
            lam_init = 0.8 - 0.6 * math.exp(-0.3 * i)
            lp = lam_b[j].astype(F32)
            lam = jnp.exp(jnp.sum(lp[0] * lp[1])) - jnp.exp(jnp.sum(lp[2] * lp[3])) + lam_init
            o = _attention(q.reshape(b, t, d), segments, rel_bias, lam, 1.0 - lam_init,
                           g_subln[j], q_off, nm + "_attn")
            w_mix = w_out_b[j]
        x, cbuf = _ffn(x, o.reshape(b, t, d), w_mix, g_norms[i, 1], g_norms[i, 2], w_ffn_in[i],
                       conv_w[i], conv_b[i], w_ffn_out[i], g_norms[i, 3], conv_s0[i],
                       ple[i], w_ple_gate[i], w_ple[i], nm + "_ffn")
        conv_out.append(cbuf)
        x2 = x.reshape(m, d)
        if i == n_a - 1:
            k_new, v_new = _norm_kv(x2, g_kv, w_kv, nm + "_kv")
            k_new = k_new.reshape(b, t, d)
            v_new = v_new.reshape(b, t, d)
            segments = []
            if past_k is not None:
                tp = past_k.shape[1]
                segments.append((past_k.reshape(b, tp, d), past_v.reshape(b, tp, d), 0, tp))
            segments.append((k_new, v_new, q_off, min(t, ATTN_KEY_TILE)))
    hd = (b, t, N_HEADS, HEAD_W)
    return x, k_new.reshape(hd), v_new.reshape(hd), jnp.stack(hg_out), jnp.stack(conv_out)


def kernel(x_prompt, x_sample, cache_k, cache_v, state_hgrn, state_conv, p_prompt, p_sample,
           g_norms, w_in_a, lb_raw, g_hg, w_out_a, g_kv, w_kv, rel_bias,
           w_q_b, lam_b, g_subln, w_out_b, w_ffn_in, conv_w, conv_b, w_ffn_out,
           w_ple, w_ple_gate):
    sm = jax.nn.softmax(lb_raw.astype(F32), axis=0)
    cs = jnp.cumsum(sm, axis=0)
    lower = cs - cs[0:1]
    bf = lambda w: w.astype(BF16)
    wts = (g_norms, bf(w_in_a), lower, g_hg, bf(w_out_a), g_kv, bf(w_kv), rel_bias,
           bf(w_q_b), lam_b, g_subln, bf(w_out_b), bf(w_ffn_in), conv_w, conv_b,
           bf(w_ffn_out), bf(w_ple), bf(w_ple_gate))
    bp, tp, _ = x_prompt.shape
    n_a, depth = w_in_a.shape[0], g_norms.shape[0]
    hg0 = jnp.zeros((n_a, bp) + state_hgrn.shape[2:], F32)
    cv0 = jnp.zeros((depth, bp) + state_conv.shape[2:], F32)
    y_p, k_p, v_p, hg_p, cv_p = _trunk(x_prompt, p_prompt, 0, None, None, hg0, cv0, wts, "p")
    y_s, k_s, v_s, hg_s, cv_s = _trunk(x_sample, p_sample, cache_k.shape[1], cache_k, cache_v,
                                       state_hgrn, state_conv, wts, "s")
    return (y_p, y_s, k_p, v_p, k_s, v_s, hg_p, hg_s, cv_p, cv_s)
```

```python
import functools
import math

import numpy as np
import jax
import jax.numpy as jnp
from jax import lax
from jax.experimental import pallas as pl
from jax.experimental.pallas import tpu as pltpu

F32 = jnp.float32
BF16 = jnp.bfloat16

D_MODEL = 1024
N_HEADS = 8
HEAD_W = D_MODEL // N_HEADS
DA_HEAD_DIM = HEAD_W // 2
CHUNK = 64
GLA_BLOCK = 16
NUM_BUCKETS = 32
MAX_DISTANCE = 256
D_FF = 2816
CONV_W = 3
EPS = 1e-6
NEG_BIG = -1e30
LOG2E = math.log2(math.e)
ATTN_Q_SCALE = DA_HEAD_DIM ** -0.5 * LOG2E

V7X_VMEM_LIMIT_BYTES = 56 * 1024 * 1024

ROW_TILE = 512
FFN_ROW_TILE = 1024
FFN_COL_TILE = 256
FFN_CONV_ROWS = 128
FFN_EDGE_ROWS = 256
GLA_TILE = 256
ATTN_TILE = 512
ATTN_KEY_TILE = 256
ATTN_HEADS_PER_STEP = 2


def _cparams(*sem):
    return pltpu.CompilerParams(dimension_semantics=sem,
                                vmem_limit_bytes=V7X_VMEM_LIMIT_BYTES)


def _rms(x, g):
    ms = jnp.mean(x * x, axis=-1, keepdims=True)
    return x * lax.rsqrt(ms + EPS) * g


def _sigmoid(z):
    return 1.0 / (1.0 + jnp.exp(-z))


def _row_tile(m, cap):
    t = min(m, cap)
    assert m % t == 0, (m, t)
    return t


def _norm_matmul_kernel(x_ref, g_ref, w_ref, o_ref, *, scale):
    h = _rms(x_ref[...], g_ref[...]).astype(BF16)
    n = w_ref.shape[1]
    tn = min(n, D_MODEL)
    for n0 in range(0, n, tn):
        y = jnp.dot(h, w_ref[:, n0:n0 + tn], preferred_element_type=F32)
        if scale != 1.0:
            y = y * scale
        o_ref[:, n0:n0 + tn] = y.astype(o_ref.dtype)


def _norm_matmul(x2d, g, w, out_dtype, name, scale=1.0):
    m, d = x2d.shape
    n = w.shape[1]
    tm = _row_tile(m, ROW_TILE)
    return pl.pallas_call(
        functools.partial(_norm_matmul_kernel, scale=scale),
        out_shape=jax.ShapeDtypeStruct((m, n), out_dtype),
        grid=(m // tm,),
        in_specs=[pl.BlockSpec((tm, d), lambda i: (i, 0)),
                  pl.BlockSpec((1, d), lambda i: (0, 0)),
                  pl.BlockSpec((d, n), lambda i: (0, 0))],
        out_specs=pl.BlockSpec((tm, n), lambda i: (i, 0)),
        compiler_params=_cparams("parallel"),
        name=name,
    )(x2d, g.reshape(1, d), w)


def _norm_kv_kernel(x_ref, g_ref, w_ref, k_ref, v_ref):
    d = k_ref.shape[1]
    h = _rms(x_ref[...], g_ref[...]).astype(BF16)
    k_ref[...] = jnp.dot(h, w_ref[:, :d], preferred_element_type=F32)
    v_ref[...] = jnp.dot(h, w_ref[:, d:], preferred_element_type=F32)


def _norm_kv(x2d, g, w, name):
    m, d = x2d.shape
    tm = _row_tile(m, ROW_TILE)
    out = jax.ShapeDtypeStruct((m, d), F32)
    return pl.pallas_call(
        _norm_kv_kernel,
        out_shape=(out, out),
        grid=(m // tm,),
        in_specs=[pl.BlockSpec((tm, d), lambda i: (i, 0)),
                  pl.BlockSpec((1, d), lambda i: (0, 0)),
                  pl.BlockSpec((d, 2 * d), lambda i: (0, 0))],
        out_specs=(pl.BlockSpec((tm, d), lambda i: (i, 0)),
                   pl.BlockSpec((tm, d), lambda i: (i, 0))),
        compiler_params=_cparams("parallel"),
        name=name,
    )(x2d, g.reshape(1, d), w)


def _ffn_kernel(x_ref, a_ref, wm_ref, gm_ref, gin_ref, wgu_ref, cw_ref, cb_ref, wo_ref, gout_ref,
                cs_ref, p_ref, wgate_ref, wple_ref,
                o_ref, co_ref, xm_ref, h_ref, acc_ref, gu_ref, act_ref):
    t = pl.program_id(1)
    j = pl.program_id(2)
    nb, tm, d = x_ref.shape
    rows = nb * tm
    rb = min(rows, FFN_EDGE_ROWS)

    def chunk(ref, r0):
        return ref[0, r0:r0 + rb, :] if nb == 1 else ref[...].reshape(rows, ref.shape[2])

    @pl.when(j == 0)
    def _():
        for r0 in range(0, rows, rb):
            m = jnp.dot(chunk(a_ref, r0), wm_ref[...], preferred_element_type=F32)
            xm = chunk(x_ref, r0) + _rms(m, gm_ref[...])
            xm_ref[r0:r0 + rb, :] = xm
            h_ref[r0:r0 + rb, :] = _rms(xm, gin_ref[...]).astype(BF16)
        acc_ref[...] = jnp.zeros_like(acc_ref)

    @pl.when((j == 0) & (t == 0))
    def _():
        co_ref[...] = cs_ref[...]

    gu_ref[...] = jnp.dot(h_ref[...], wgu_ref[0], preferred_element_type=F32)
    cw = cw_ref[...]
    cb = cb_ref[...]
    tf = cw.shape[1]
    r8 = lax.broadcasted_iota(jnp.int32, (8, tf), 0)
    rc = min(tm, FFN_CONV_ROWS)
    for bi in range(nb):
        base = bi * tm
        prev = co_ref[bi, j]
        co_ref[bi, j] = gu_ref[base + tm - (CONV_W - 1):base + tm, :tf]
        head = jnp.where(r8 == 6, prev[0:1, :], prev[1:2, :])
        for r0 in range(base, base + tm, rc):
            if r0 > base:
                head = gu_ref[r0 - 8:r0, :tf]
            gc = gu_ref[r0:r0 + rc, :tf]
            ext = jnp.concatenate([head, gc], axis=0)
            g1 = pltpu.roll(ext, 1, axis=0)[8:, :]
            g2 = pltpu.roll(ext, 2, axis=0)[8:, :]
            c = cb + cw[0:1, :] * g2 + cw[1:2, :] * g1 + cw[2:3, :] * gc
            act_ref[r0:r0 + rc, :] = (c * _sigmoid(c) * gu_ref[r0:r0 + rc, tf:]).astype(BF16)
    acc_ref[...] += jnp.dot(act_ref[...], wo_ref[...], preferred_element_type=F32)

    @pl.when(j == pl.num_programs(2) - 1)
    def _():
        for r0 in range(0, rows, rb):
            y = xm_ref[r0:r0 + rb, :] + _rms(acc_ref[r0:r0 + rb, :], gout_ref[...])
            z = jnp.dot(y.astype(BF16), wgate_ref[...], preferred_element_type=F32)
            e = jnp.dot(chunk(p_ref, r0).astype(BF16), wple_ref[...], preferred_element_type=F32)
            out = y + _sigmoid(z) * e
            if nb == 1:
                o_ref[0, r0:r0 + rb, :] = out
            else:
                o_ref[...] = out.reshape(nb, tm, d)


def _ffn(x, a, w_mix, g_mix, gin, w_in, cw, cb, w_out, gout, conv_state, p, w_gate, w_ple, name):
    b, t, d = x.shape
    f = w_out.shape[0]
    pd = p.shape[2]
    tm = _row_tile(t, FFN_ROW_TILE)
    nb = max(1, min(b, FFN_ROW_TILE // t))
    assert b % nb == 0
    rows = nb * tm
    assert rows % min(rows, FFN_EDGE_ROWS) == 0 and (nb == 1 or rows <= FFN_EDGE_ROWS)
    tf = FFN_COL_TILE
    nf = f // tf
    assert f % tf == 0 and t >= CONV_W - 1 and tm % 8 == 0
    cs = conv_state.reshape(b, CONV_W - 1, nf, tf).transpose(0, 2, 1, 3)
    cs_spec = pl.BlockSpec((nb, nf, CONV_W - 1, tf), lambda bi, ti, j: (bi, 0, 0, 0))
    w_tiles = w_in.reshape(d, 2, nf, tf).transpose(2, 0, 1, 3).reshape(nf, d, 2 * tf)
    row_spec = lambda width: pl.BlockSpec((nb, tm, width), lambda bi, ti, j: (bi, ti, 0))
    vec_spec = pl.BlockSpec((1, d), lambda bi, ti, j: (0, 0))
    const_spec = lambda shape: pl.BlockSpec(shape, lambda bi, ti, j: (0, 0),
                                            pipeline_mode=pl.Buffered(1))
    y, co = pl.pallas_call(
        _ffn_kernel,
        out_shape=(jax.ShapeDtypeStruct((b, t, d), F32),
                   jax.ShapeDtypeStruct(cs.shape, F32)),
        grid=(b // nb, t // tm, nf),
        in_specs=[row_spec(d), row_spec(d), const_spec((d, d)), vec_spec,
                  vec_spec,
                  pl.BlockSpec((1, d, 2 * tf), lambda bi, ti, j: (j, 0, 0)),
                  pl.BlockSpec((CONV_W, tf), lambda bi, ti, j: (0, j)),
                  pl.BlockSpec((1, tf), lambda bi, ti, j: (0, j)),
                  pl.BlockSpec((tf, d), lambda bi, ti, j: (j, 0)),
                  vec_spec,
                  cs_spec, row_spec(pd), const_spec((d, d)), const_spec((pd, d))],
        out_specs=(row_spec(d), cs_spec),
        scratch_shapes=[pltpu.VMEM((rows, d), F32),
                        pltpu.VMEM((rows, d), BF16),
                        pltpu.VMEM((rows, d), F32),
                        pltpu.VMEM((rows, 2 * tf), F32),
                        pltpu.VMEM((rows, tf), BF16)],
        compiler_params=_cparams("parallel", "arbitrary", "arbitrary"),
        name=name,
    )(x, a, w_mix, g_mix.reshape(1, d), gin.reshape(1, d), w_tiles, cw, cb.reshape(1, f), w_out,
      gout.reshape(1, d), cs, p, w_gate, w_ple)
    return y, co.transpose(0, 2, 1, 3).reshape(b, CONV_W - 1, f)


def _split3(x):
    hi = x.astype(BF16)
    r1 = x - hi.astype(F32)
    mid = r1.astype(BF16)
    lo = (r1 - mid.astype(F32)).astype(BF16)
    return hi, mid, lo


def _gla_levels(tc):
    return [s for s in (16, 32, 64, 128, 256, 512, 1024) if 2 * s <= tc]


def _gla_level_table(tc):
    r = np.arange(tc)[:, None]
    c = np.arange(tc)[None, :]
    table = np.full((tc, tc), -1, np.int32)
    table[(r // GLA_BLOCK == c // GLA_BLOCK) & (c <= r)] = 0
    for i, s in enumerate(_gla_levels(tc)):
        table[(r // (2 * s) == c // (2 * s)) & ((r // s) % 2 == 1) & ((c // s) % 2 == 0)] = i + 1
    return table


def _gla_kernel(q_ref, f_ref, i_ref, g_ref, ll_ref, l1m_ref, om_ref, gn_ref, s0_ref, lvl_ref,
                o_ref, so_ref, st_ref, bg_ref, fac_ref, qi_ref, ks_ref, v_ref, a_ref, oo_ref):
    t = pl.program_id(1)
    tc = q_ref.shape[1]
    d = q_ref.shape[2]
    levels = _gla_levels(tc)

    @pl.when(t == 0)
    def _():
        for h in range(N_HEADS):
            st_ref[h] = s0_ref[0, h].T

    qh = q_ref[0]
    fp = f_ref[0]
    q = qh * _sigmoid(qh)
    e = jnp.exp(-jnp.abs(fp))
    logsig = jnp.minimum(fp, 0.0) - jnp.log(1.0 + e)
    a = ll_ref[...]
    c = l1m_ref[...] + logsig
    logf = jnp.maximum(a, c) + jnp.log(1.0 + jnp.exp(-jnp.abs(a - c)))
    kk = om_ref[...] * (jnp.where(fp >= 0.0, e, 1.0) / (1.0 + e))

    row = lax.broadcasted_iota(jnp.int32, (tc, tc), 0)
    col = lax.broadcasted_iota(jnp.int32, (tc, tc), 1)
    m_incl = jnp.where(col <= row, 1.0, 0.0).astype(BF16)
    bg = sum(jnp.dot(m_incl, p, preferred_element_type=F32) for p in _split3(logf))
    bg_ref[...] = bg
    b_last = bg[tc - 1:tc, :]

    qi_ref[...] = (q * jnp.exp(bg)).astype(BF16)
    ks_ref[...] = (kk * jnp.exp(b_last - bg)).astype(BF16)
    v_ref[...] = i_ref[0].astype(BF16)

    def ref_rows(span, offset):
        pieces = []
        for a0 in range(0, tc, span):
            r = a0 + offset
            src = bg_ref[r:r + 1, :] if r >= 0 else jnp.zeros((1, d), F32)
            pieces.append(jnp.broadcast_to(src, (span, d)))
        return pieces[0] if len(pieces) == 1 else jnp.concatenate(pieces, axis=0)

    dl = bg - ref_rows(GLA_BLOCK, -1)
    fac_ref[0] = (q * jnp.exp(dl)).astype(BF16)
    fac_ref[1] = (kk * jnp.exp(-dl)).astype(BF16)
    rid = lax.broadcasted_iota(jnp.int32, (tc, 1), 0)
    for li, s in enumerate(levels):
        mid = ref_rows(2 * s, s - 1)
        right = ((rid // s) % 2) == 1
        x = jnp.exp(jnp.where(right, bg - mid, mid - bg))
        fac_ref[2 + 2 * li] = (q * x).astype(BF16)
        fac_ref[3 + 2 * li] = (kk * x).astype(BF16)

    nt = (((1,), (1,)), ((), ()))
    tn = (((0,), (0,)), ((), ()))
    lvl = lvl_ref[...]
    for h in range(N_HEADS):
        cols = slice(h * HEAD_W, (h + 1) * HEAD_W)
        att = jnp.zeros((tc, tc), F32)
        for ci in range(len(levels), -1, -1):
            prod = lax.dot_general(fac_ref[2 * ci, :, cols], fac_ref[2 * ci + 1, :, cols], nt,
                                   preferred_element_type=F32)
            att = jnp.where(lvl == ci, prod, att)
        a_ref[h] = att.astype(BF16)

    for h in range(N_HEADS):
        cols = slice(h * HEAD_W, (h + 1) * HEAD_W)
        vb = v_ref[:, cols]
        s_t = st_ref[h]
        o_inter = lax.dot_general(qi_ref[:, cols], s_t.astype(BF16), nt, preferred_element_type=F32)
        oo_ref[:, cols] = o_inter + jnp.dot(a_ref[h], vb, preferred_element_type=F32)
        upd = lax.dot_general(vb, ks_ref[:, cols], tn, preferred_element_type=F32)
        st_ref[h] = s_t * jnp.exp(b_last[:, cols]) + upd

    gh = g_ref[0]
    gate = gh * _sigmoid(gh)
    for h in range(N_HEADS):
        cols = slice(h * HEAD_W, (h + 1) * HEAD_W)
        oh = _rms(oo_ref[:, cols], gn_ref[:, cols])
        o_ref[0, :, cols] = (oh * gate[:, cols]).astype(o_ref.dtype)

    @pl.when(t == pl.num_programs(1) - 1)
    def _():
        for h in range(N_HEADS):
            so_ref[0, h] = st_ref[h].T


def _gla(proj, log_lower, log1m_lower, one_m_lower, g_out, s0, name):
    b, t, _ = proj.shape
    d = D_MODEL
    tc = _row_tile(t, GLA_TILE)
    assert tc % GLA_BLOCK == 0
    vec = pl.BlockSpec((1, d), lambda bi, ti: (0, 0))
    col_spec = lambda c: pl.BlockSpec((1, tc, d), lambda bi, ti: (bi, ti, c))
    st_spec = pl.BlockSpec((1, N_HEADS, HEAD_W, HEAD_W), lambda bi, ti: (bi, 0, 0, 0))
    n_fac = 2 * (1 + len(_gla_levels(tc)))
    return pl.pallas_call(
        _gla_kernel,
        out_shape=(jax.ShapeDtypeStruct((b, t, d), BF16),
                   jax.ShapeDtypeStruct((b, N_HEADS, HEAD_W, HEAD_W), F32)),
        grid=(b, t // tc),
        in_specs=[col_spec(0), col_spec(1), col_spec(2), col_spec(3),
                  vec, vec, vec, vec, st_spec,
                  pl.BlockSpec((tc, tc), lambda bi, ti: (0, 0))],
        out_specs=(pl.BlockSpec((1, tc, d), lambda bi, ti: (bi, ti, 0)), st_spec),
        scratch_shapes=[pltpu.VMEM((N_HEADS, HEAD_W, HEAD_W), F32),
                        pltpu.VMEM((tc, d), F32),
                        pltpu.VMEM((n_fac, tc, d), BF16),
                        pltpu.VMEM((tc, d), BF16), pltpu.VMEM((tc, d), BF16),
                        pltpu.VMEM((tc, d), BF16),
                        pltpu.VMEM((N_HEADS, tc, tc), BF16),
                        pltpu.VMEM((tc, d), F32)],
        compiler_params=_cparams("parallel", "arbitrary"),
        name=name,
    )(proj, proj, proj, proj, log_lower.reshape(1, d), log1m_lower.reshape(1, d),
      one_m_lower.reshape(1, d), g_out.reshape(1, d), s0, jnp.asarray(_gla_level_table(tc)))


def _t5_bucket(rel):
    half = NUM_BUCKETS // 2
    ret = jnp.where(rel > 0, half, 0)
    n = jnp.abs(rel)
    max_exact = half // 2
    nf = jnp.maximum(n, 1).astype(F32)
    large = max_exact + (jnp.log(nf / max_exact) / math.log(MAX_DISTANCE / max_exact)
                         * (half - max_exact)).astype(jnp.int32)
    large = jnp.minimum(large, half - 1)
    return ret + jnp.where(n < max_exact, n, large)


def _toeplitz_kernel(w_ref, o_ref, *, tq):
    tk = o_ref.shape[2]
    p = w_ref.shape[3]
    x = jnp.broadcast_to(w_ref[0, 0], (tk, p))
    o_ref[0, 0] = pltpu.roll(x, 0, axis=1, stride=1, stride_axis=0)[:, :tq]


def _bias_tiles_t(rel_bias, d0_list, tq, tk):
    p = pl.cdiv(tq + tk, HEAD_W) * HEAD_W
    m = jnp.arange(p, dtype=jnp.int32)
    rel = jnp.asarray(d0_list, jnp.int32)[:, None] + jnp.where(m <= tq, -m, p - m)[None, :]
    onehot = (_t5_bucket(rel)[:, :, None] == jnp.arange(NUM_BUCKETS, dtype=jnp.int32)).astype(F32)
    w = jnp.einsum("dpn,nh->hdp", onehot, rel_bias.astype(F32) * LOG2E,
                   precision=lax.Precision.HIGHEST)
    h, nd = w.shape[0], w.shape[1]
    return pl.pallas_call(
        functools.partial(_toeplitz_kernel, tq=tq),
        out_shape=jax.ShapeDtypeStruct((h, nd, tk, tq), F32),
        grid=(h, nd),
        in_specs=[pl.BlockSpec((1, 1, 1, p), lambda hi, di: (hi, di, 0, 0))],
        out_specs=pl.BlockSpec((1, 1, tk, tq), lambda hi, di: (hi, di, 0, 0)),
        compiler_params=_cparams("parallel", "parallel"),
        name="bias_tiles",
    )(w.reshape(h, nd, 1, p))


def _attn_kernel(*refs, segs, q_off, tq, hb):
    nseg = len(segs)
    q_ref, lam_ref, gn_ref = refs[0], refs[1], refs[2]
    seg_refs = [refs[3 + 3 * s: 6 + 3 * s] for s in range(nseg)]
    o_ref = refs[3 + 3 * nseg]
    m_ref, l_ref, acc_ref = refs[4 + 3 * nseg: 7 + 3 * nseg]
    kv_scratch = refs[7 + 3 * nseg:]
    seg_scr = [kv_scratch[4 * s: 4 * s + 4] for s in range(nseg)]

    i = pl.program_id(2)
    q0 = q_off + i * tq

    @pl.when(i == 0)
    def _():
        for (k_ref, v_ref, _), (kb_ref, vt_ref, _, _), (_, seg_len, tk, _, _) in zip(seg_refs, seg_scr, segs):
            kb_ref[...] = k_ref[0].astype(BF16)
            for kt in range(seg_len // tk):
                vt_ref[kt] = v_ref[0, kt * tk:(kt + 1) * tk, :].astype(F32).T.astype(BF16)

    q = q_ref[0]
    lane = lax.broadcasted_iota(jnp.int32, (tq, HEAD_W), 1)
    qz = []
    for hh in range(hb):
        qh = q[:, hh * HEAD_W:(hh + 1) * HEAD_W]
        qz.append([jnp.where(lane < DA_HEAD_DIM, qh, 0).astype(BF16),
                   jnp.where(lane >= DA_HEAD_DIM, qh, 0).astype(BF16)])
    m_ref[...] = jnp.full_like(m_ref, NEG_BIG)
    l_ref[...] = jnp.zeros_like(l_ref)
    acc_ref[...] = jnp.zeros_like(acc_ref)
    q_chunk = (q0 + lax.broadcasted_iota(jnp.int32, (1, tq), 1)) // CHUNK
    vis_end = ((q0 + tq - 1) // CHUNK + 1) * CHUNK
    full_end = (q0 // CHUNK + 1) * CHUNK
    nt = (((1,), (1,)), ((), ()))

    for (_, _, b_ref), (kb_ref, vt_ref, sa_ref, sb_ref), (pos0, seg_len, tk, d0_min, d0_step) in zip(
            seg_refs, seg_scr, segs):
        n_tiles = seg_len // tk
        n_vis = jnp.clip((vis_end - pos0 + tk - 1) // tk, 0, n_tiles)
        n_full = jnp.clip((full_end - pos0) // tk, 0, n_vis)

        def scores_into(kt, s_ref, kb_ref=kb_ref, tk=tk, n_tiles=n_tiles):
            r0 = pl.multiple_of(jnp.minimum(kt, n_tiles - 1) * tk, tk)
            for hh in range(hb):
                kb = kb_ref[pl.ds(r0, tk), hh * HEAD_W:(hh + 1) * HEAD_W]
                for c in range(2):
                    s_ref[2 * hh + c] = lax.dot_general(kb, qz[hh][c], nt, preferred_element_type=F32)

        def consume(kt, s_ref, masked, b_ref=b_ref, vt_ref=vt_ref, pos0=pos0, tk=tk,
                    d0_min=d0_min, d0_step=d0_step, n_tiles=n_tiles, n_vis=n_vis):
            ktc = jnp.minimum(kt, n_tiles - 1)
            r0 = ktc * tk
            d_idx = (pos0 + r0 - q0 - d0_min) // d0_step
            if masked:
                k0 = jnp.where(kt < n_vis, pos0 + r0, jnp.int32(1 << 28))
                k_chunk = (k0 + lax.broadcasted_iota(jnp.int32, (tk, 1), 0)) // CHUNK
                visible = k_chunk <= q_chunk
            for hh in range(hb):
                vt = vt_ref[ktc, hh * HEAD_W:(hh + 1) * HEAD_W, :]
                bias = b_ref[hh, d_idx]
                for c in range(2):
                    j = 2 * hh + c
                    s = s_ref[j] + bias
                    if masked:
                        s = jnp.where(visible, s, NEG_BIG)
                    m_prev = m_ref[j]
                    m_new = jnp.maximum(m_prev, jnp.max(s, axis=0, keepdims=True))
                    p = jnp.exp2(s - m_new)
                    alpha = jnp.exp2(m_prev - m_new)
                    l_ref[j] = alpha * l_ref[j] + jnp.sum(p, axis=0, keepdims=True)
                    acc_ref[j] = alpha * acc_ref[j] + jnp.dot(vt, p.astype(BF16),
                                                              preferred_element_type=F32)
                    m_ref[j] = m_new

        def pair(u, carry, masked, base, sa_ref=sa_ref, sb_ref=sb_ref):
            t0 = base + 2 * u
            scores_into(t0 + 1, sb_ref)
            consume(t0, sa_ref, masked)
            scores_into(t0 + 2, sa_ref)
            consume(t0 + 1, sb_ref, masked)
            return carry

        n_pairs_full = n_full // 2
        base = 2 * n_pairs_full
        scores_into(0, sa_ref)
        lax.fori_loop(0, n_pairs_full, functools.partial(pair, masked=False, base=0), 0)
        lax.fori_loop(0, (n_vis - base + 1) // 2, functools.partial(pair, masked=True, base=base), 0)

    lam = lam_ref[0:1, 0:1]
    for hh in range(hb):
        o_t = (acc_ref[2 * hh] * (1.0 / l_ref[2 * hh])
               - lam * (acc_ref[2 * hh + 1] * (1.0 / l_ref[2 * hh + 1])))
        cols = slice(hh * HEAD_W, (hh + 1) * HEAD_W)
        o_ref[0, :, cols] = (_rms(o_t.T, gn_ref[:, cols]) * lam_ref[0:1, 1:2]).astype(o_ref.dtype)


def _attention(q, segments, rel_bias, lam, out_scale, g_subln, q_off, name):
    b, t_real, d = q.shape
    tq_all = max(t_real, HEAD_W)
    if tq_all != t_real:
        q = jnp.pad(q, ((0, 0), (0, tq_all - t_real), (0, 0)))
    tq = _row_tile(tq_all, ATTN_TILE)
    nq = tq_all // tq
    hb = 2 * ATTN_HEADS_PER_STEP if nq == 1 else ATTN_HEADS_PER_STEP
    hw = hb * HEAD_W
    segs, args, in_specs, scratch = [], [], [], []
    for (k, v, pos0, tk) in segments:
        seg_len = k.shape[1]
        assert seg_len % tk == 0
        d0s = sorted({pos0 + kt * tk - (q_off + i * tq)
                      for i in range(nq) for kt in range(seg_len // tk)
                      if pos0 + kt * tk < ((q_off + (i + 1) * tq - 1) // CHUNK + 1) * CHUNK})
        step = math.gcd(tq, tk)
        d0s = list(range(d0s[0], d0s[-1] + 1, step))
        tiles = _bias_tiles_t(rel_bias, d0s, tq, tk)
        segs.append((pos0, seg_len, tk, d0s[0], step))
        args += [k, v, tiles]
        in_specs += [pl.BlockSpec((1, seg_len, hw), lambda bi, h, i: (bi, 0, h)),
                     pl.BlockSpec((1, seg_len, hw), lambda bi, h, i: (bi, 0, h)),
                     pl.BlockSpec((hb, len(d0s), tk, tq), lambda bi, h, i: (h, 0, 0, 0))]
        scratch += [pltpu.VMEM((seg_len, hw), BF16), pltpu.VMEM((seg_len // tk, hw, tk), BF16),
                    pltpu.VMEM((2 * hb, tk, tq), F32), pltpu.VMEM((2 * hb, tk, tq), F32)]
    scal = jnp.zeros((1, HEAD_W), F32).at[0, 0].set(lam).at[0, 1].set(out_scale)
    kern = functools.partial(_attn_kernel, segs=tuple(segs), q_off=q_off, tq=tq, hb=hb)
    out = pl.pallas_call(
        kern,
        out_shape=jax.ShapeDtypeStruct((b, tq_all, d), BF16),
        grid=(b, N_HEADS // hb, nq),
        in_specs=[pl.BlockSpec((1, tq, hw), lambda bi, h, i: (bi, i, h)),
                  pl.BlockSpec((1, HEAD_W), lambda bi, h, i: (0, 0)),
                  pl.BlockSpec((1, hw), lambda bi, h, i: (0, h))] + in_specs,
        out_specs=pl.BlockSpec((1, tq, hw), lambda bi, h, i: (bi, i, h)),
        scratch_shapes=[pltpu.VMEM((2 * hb, 1, tq), F32), pltpu.VMEM((2 * hb, 1, tq), F32),
                        pltpu.VMEM((2 * hb, HEAD_W, tq), F32)] + scratch,
        compiler_params=_cparams("parallel", "parallel", "arbitrary"),
        name=name,
    )(q, scal, g_subln.reshape(1, d), *args)
    return out[:, :t_real]


def _trunk(x, ple, q_off, past_k, past_v, hg_s0, conv_s0, wts, tag):
    (g_norms, w_in_a, lower, g_hg, w_out_a, g_kv, w_kv, rel_bias, w_q_b, lam_b,
     g_subln, w_out_b, w_ffn_in, conv_w, conv_b, w_ffn_out, w_ple, w_ple_gate) = wts
    b, t, d = x.shape
    depth = g_norms.shape[0]
    n_a = w_in_a.shape[0]
    m = b * t
    hg_out, conv_out = [], []
    k_new = v_new = None
    segments = None
    for i in range(depth):
        nm = f"{tag}{i}"
        x2 = x.reshape(m, d)
        if i < n_a:
            lw = lower[i]
            proj = _norm_matmul(x2, g_norms[i, 0], w_in_a[i], F32, nm + "_hg_in")
            o, s = _gla(proj.reshape(b, t, 4 * d), jnp.log(lw), jnp.log1p(-lw), 1.0 - lw,
                        g_hg[i], hg_s0[i], nm + "_gla")
            hg_out.append(s)
            w_mix = w_out_a[i]
        else:
            j = i - n_a
            q = _norm_matmul(x2, g_norms[i, 0], w_q_b[j], BF16, nm + "_q", scale=ATTN_Q_SCALE)
            lam_init = 0.8 - 0.6 * math.exp(-0.3 * i)
            lp = lam_b[j].astype(F32)
            lam = jnp.exp(jnp.sum(lp[0] * lp[1])) - jnp.exp(jnp.sum(lp[2] * lp[3])) + lam_init
            o = _attention(q.reshape(b, t, d), segments, rel_bias, lam, 1.0 - lam_init,
                           g_subln[j], q_off, nm + "_attn")
            w_mix = w_out_b[j]
        x, cbuf = _ffn(x, o.reshape(b, t, d), w_mix, g_norms[i, 1], g_norms[i, 2], w_ffn_in[i],
                       conv_w[i], conv_b[i], w_ffn_out[i], g_norms[i, 3], conv_s0[i],
                       ple[i], w_ple_gate[i], w_ple[i], nm + "_ffn")
        conv_out.append(cbuf)
        x2 = x.reshape(m, d)
        if i == n_a - 1:
            k_new, v_new = _norm_kv(x2, g_kv, w_kv, nm + "_kv")
            k_new = k_new.reshape(b, t, d)
            v_new = v_new.reshape(b, t, d)
            segments = []
            if past_k is not None:
                tp = past_k.shape[1]
                segments.append((past_k.reshape(b, tp, d), past_v.reshape(b, tp, d), 0, tp))
            segments.append((k_new, v_new, q_off, min(t, ATTN_KEY_TILE)))
    hd = (b, t, N_HEADS, HEAD_W)
    return x, k_new.reshape(hd), v_new.reshape(hd), jnp.stack(hg_out), jnp.stack(conv_out)


def kernel(x_prompt, x_sample, cache_k, cache_v, state_hgrn, state_conv, p_prompt, p_sample,
           g_norms, w_in_a, lb_raw, g_hg, w_out_a, g_kv, w_kv, rel_bias,
           w_q_b, lam_b, g_subln, w_out_b, w_ffn_in, conv_w, conv_b, w_ffn_out,
           w_ple, w_ple_gate):
    sm = jax.nn.softmax(lb_raw.astype(F32), axis=0)
    cs = jnp.cumsum(sm, axis=0)
    lower = cs - cs[0:1]
    bf = lambda w: w.astype(BF16)
    wts = (g_norms, bf(w_in_a), lower, g_hg, bf(w_out_a), g_kv, bf(w_kv), rel_bias,
           bf(w_q_b), lam_b, g_subln, bf(w_out_b), bf(w_ffn_in), conv_w, conv_b,
           bf(w_ffn_out), bf(w_ple), bf(w_ple_gate))
    bp, tp, _ = x_prompt.shape
    n_a, depth = w_in_a.shape[0], g_norms.shape[0]
    hg0 = jnp.zeros((n_a, bp) + state_hgrn.shape[2:], F32)
    cv0 = jnp.zeros((depth, bp) + state_conv.shape[2:], F32)
    y_p, k_p, v_p, hg_p, cv_p = _trunk(x_prompt, p_prompt, 0, None, None, hg0, cv0, wts, "p")
    y_s, k_s, v_s, hg_s, cv_s = _trunk(x_sample, p_sample, cache_k.shape[1], cache_k, cache_v,
                                       state_hgrn, state_conv, wts, "s")
    return (y_p, y_s, k_p, v_p, k_s, v_s, hg_p, hg_s, cv_p, cv_s)
```

```python
import functools
import math

import numpy as np
import jax
import jax.numpy as jnp
from jax import lax
from jax.experimental import pallas as pl
from jax.experimental.pallas import tpu as pltpu

F32 = jnp.float32
BF16 = jnp.bfloat16

D_MODEL = 1024
N_HEADS = 8
HEAD_W = D_MODEL // N_HEADS
DA_HEAD_DIM = HEAD_W // 2
CHUNK = 64
GLA_BLOCK = 16
NUM_BUCKETS = 32
MAX_DISTANCE = 256
D_FF = 2816
CONV_W = 3
EPS = 1e-6
NEG_BIG = -1e30
LOG2E = math.log2(math.e)
ATTN_Q_SCALE = DA_HEAD_DIM ** -0.5 * LOG2E

V7X_VMEM_LIMIT_BYTES = 56 * 1024 * 1024

ROW_TILE = 512
FFN_ROW_TILE = 1024
FFN_COL_TILE = 256
FFN_CONV_ROWS = 128
FFN_ROW_PARTS = 2
FFN_EDGE_ROWS = 256
GLA_TILE = 256
ATTN_TILE = 512
ATTN_KEY_TILE = 256
ATTN_HEADS_PER_STEP = 2


def _cparams(*sem):
    return pltpu.CompilerParams(dimension_semantics=sem,
                                vmem_limit_bytes=V7X_VMEM_LIMIT_BYTES)


def _rms(x, g):
    ms = jnp.mean(x * x, axis=-1, keepdims=True)
    return x * lax.rsqrt(ms + EPS) * g


def _sigmoid(z):
    return 0.5 + 0.5 * jnp.tanh(0.5 * z)


def _row_tile(m, cap):
    t = min(m, cap)
    assert m % t == 0, (m, t)
    return t


def _norm_matmul_kernel(x_ref, g_ref, w_ref, o_ref, *, scale):
    h = _rms(x_ref[...], g_ref[...]).astype(BF16)
    n = w_ref.shape[1]
    tn = min(n, D_MODEL)
    for n0 in range(0, n, tn):
        y = jnp.dot(h, w_ref[:, n0:n0 + tn], preferred_element_type=F32)
        if scale != 1.0:
            y = y * scale
        o_ref[:, n0:n0 + tn] = y.astype(o_ref.dtype)


def _norm_matmul(x2d, g, w, out_dtype, name, scale=1.0):
    m, d = x2d.shape
    n = w.shape[1]
    tm = _row_tile(m, ROW_TILE)
    return pl.pallas_call(
        functools.partial(_norm_matmul_kernel, scale=scale),
        out_shape=jax.ShapeDtypeStruct((m, n), out_dtype),
        grid=(m // tm,),
        in_specs=[pl.BlockSpec((tm, d), lambda i: (i, 0)),
                  pl.BlockSpec((1, d), lambda i: (0, 0)),
                  pl.BlockSpec((d, n), lambda i: (0, 0))],
        out_specs=pl.BlockSpec((tm, n), lambda i: (i, 0)),
        compiler_params=_cparams("parallel"),
        name=name,
    )(x2d, g.reshape(1, d), w)


def _norm_kv_kernel(x_ref, g_ref, w_ref, k_ref, v_ref):
    d = k_ref.shape[1]
    h = _rms(x_ref[...], g_ref[...]).astype(BF16)
    k_ref[...] = jnp.dot(h, w_ref[:, :d], preferred_element_type=F32)
    v_ref[...] = jnp.dot(h, w_ref[:, d:], preferred_element_type=F32)


def _norm_kv(x2d, g, w, name):
    m, d = x2d.shape
    tm = _row_tile(m, ROW_TILE)
    out = jax.ShapeDtypeStruct((m, d), F32)
    return pl.pallas_call(
        _norm_kv_kernel,
        out_shape=(out, out),
        grid=(m // tm,),
        in_specs=[pl.BlockSpec((tm, d), lambda i: (i, 0)),
                  pl.BlockSpec((1, d), lambda i: (0, 0)),
                  pl.BlockSpec((d, 2 * d), lambda i: (0, 0))],
        out_specs=(pl.BlockSpec((tm, d), lambda i: (i, 0)),
                   pl.BlockSpec((tm, d), lambda i: (i, 0))),
        compiler_params=_cparams("parallel"),
        name=name,
    )(x2d, g.reshape(1, d), w)


def _ffn_kernel(x_ref, a_ref, wm_ref, gm_ref, gin_ref, wgu_ref, cw_ref, cb_ref, wo_ref, gout_ref,
                cs_ref, p_ref, wgate_ref, wple_ref,
                o_ref, co_ref, xm_ref, h_ref, acc_ref, gu_ref, act_ref):
    t = pl.program_id(1)
    j = pl.program_id(2)
    nb, tm, d = x_ref.shape
    rows = nb * tm
    rb = min(rows, FFN_EDGE_ROWS)

    def chunk(ref, r0):
        return ref[0, r0:r0 + rb, :] if nb == 1 else ref[...].reshape(rows, ref.shape[2])

    @pl.when(j == 0)
    def _():
        for r0 in range(0, rows, rb):
            m = jnp.dot(chunk(a_ref, r0), wm_ref[...], preferred_element_type=F32)
            xm = chunk(x_ref, r0) + _rms(m, gm_ref[...])
            xm_ref[r0:r0 + rb, :] = xm
            h_ref[r0:r0 + rb, :] = _rms(xm, gin_ref[...]).astype(BF16)
        acc_ref[...] = jnp.zeros_like(acc_ref)

    @pl.when((j == 0) & (t == 0))
    def _():
        co_ref[...] = cs_ref[...]

    n_parts = FFN_ROW_PARTS if rows % (FFN_ROW_PARTS * (tm if nb > 1 else 8)) == 0 else 1
    pr = rows // n_parts
    for part in range(n_parts):
        rs = slice(part * pr, (part + 1) * pr)
        gu_ref[rs, :] = jnp.dot(h_ref[rs, :], wgu_ref[0], preferred_element_type=F32)
    cw = cw_ref[...]
    cb = cb_ref[...]
    tf = cw.shape[1]
    r8 = lax.broadcasted_iota(jnp.int32, (8, tf), 0)
    rc = min(tm, FFN_CONV_ROWS, pr)
    for part in range(n_parts):
        for r0 in range(part * pr, (part + 1) * pr, rc):
            if r0 % tm == 0:
                bi = r0 // tm
                prev = co_ref[bi, j]
                head = jnp.where(r8 == 6, prev[0:1, :], prev[1:2, :])
            else:
                head = gu_ref[r0 - 8:r0, :tf]
            gc = gu_ref[r0:r0 + rc, :tf]
            if (r0 + rc) % tm == 0:
                co_ref[r0 // tm, j] = gc[rc - (CONV_W - 1):, :]
            ext = jnp.concatenate([head, gc], axis=0)
            g1 = pltpu.roll(ext, 1, axis=0)[8:, :]
            g2 = pltpu.roll(ext, 2, axis=0)[8:, :]
            c = cb + cw[0:1, :] * g2 + cw[1:2, :] * g1 + cw[2:3, :] * gc
            act_ref[r0:r0 + rc, :] = (c * _sigmoid(c) * gu_ref[r0:r0 + rc, tf:]).astype(BF16)
        rs = slice(part * pr, (part + 1) * pr)
        acc_ref[rs, :] += jnp.dot(act_ref[rs, :], wo_ref[...], preferred_element_type=F32)

    @pl.when(j == pl.num_programs(2) - 1)
    def _():
        for r0 in range(0, rows, rb):
            y = xm_ref[r0:r0 + rb, :] + _rms(acc_ref[r0:r0 + rb, :], gout_ref[...])
            z = jnp.dot(y.astype(BF16), wgate_ref[...], preferred_element_type=F32)
            e = jnp.dot(chunk(p_ref, r0).astype(BF16), wple_ref[...], preferred_element_type=F32)
            out = y + _sigmoid(z) * e
            if nb == 1:
                o_ref[0, r0:r0 + rb, :] = out
            else:
                o_ref[...] = out.reshape(nb, tm, d)


def _ffn(x, a, w_mix, g_mix, gin, w_in, cw, cb, w_out, gout, conv_state, p, w_gate, w_ple, name):
    b, t, d = x.shape
    f = w_out.shape[0]
    pd = p.shape[2]
    tm = _row_tile(t, FFN_ROW_TILE)
    nb = max(1, min(b, FFN_ROW_TILE // t))
    assert b % nb == 0
    rows = nb * tm
    assert rows % min(rows, FFN_EDGE_ROWS) == 0 and (nb == 1 or rows <= FFN_EDGE_ROWS)
    tf = FFN_COL_TILE
    nf = f // tf
    assert f % tf == 0 and t >= CONV_W - 1 and tm % 8 == 0
    cs = conv_state.reshape(b, CONV_W - 1, nf, tf).transpose(0, 2, 1, 3)
    cs_spec = pl.BlockSpec((nb, nf, CONV_W - 1, tf), lambda bi, ti, j: (bi, 0, 0, 0))
    w_tiles = w_in.reshape(d, 2, nf, tf).transpose(2, 0, 1, 3).reshape(nf, d, 2 * tf)
    row_spec = lambda width: pl.BlockSpec((nb, tm, width), lambda bi, ti, j: (bi, ti, 0))
    vec_spec = pl.BlockSpec((1, d), lambda bi, ti, j: (0, 0))
    const_spec = lambda shape: pl.BlockSpec(shape, lambda bi, ti, j: (0, 0),
                                            pipeline_mode=pl.Buffered(1))
    y, co = pl.pallas_call(
        _ffn_kernel,
        out_shape=(jax.ShapeDtypeStruct((b, t, d), F32),
                   jax.ShapeDtypeStruct(cs.shape, F32)),
        grid=(b // nb, t // tm, nf),
        in_specs=[row_spec(d), row_spec(d), const_spec((d, d)), vec_spec,
                  vec_spec,
                  pl.BlockSpec((1, d, 2 * tf), lambda bi, ti, j: (j, 0, 0)),
                  pl.BlockSpec((CONV_W, tf), lambda bi, ti, j: (0, j)),
                  pl.BlockSpec((1, tf), lambda bi, ti, j: (0, j)),
                  pl.BlockSpec((tf, d), lambda bi, ti, j: (j, 0)),
                  vec_spec,
                  cs_spec, row_spec(pd), const_spec((d, d)), const_spec((pd, d))],
        out_specs=(row_spec(d), cs_spec),
        scratch_shapes=[pltpu.VMEM((rows, d), F32),
                        pltpu.VMEM((rows, d), BF16),
                        pltpu.VMEM((rows, d), F32),
                        pltpu.VMEM((rows, 2 * tf), F32),
                        pltpu.VMEM((rows, tf), BF16)],
        compiler_params=_cparams("parallel", "arbitrary", "arbitrary"),
        name=name,
    )(x, a, w_mix, g_mix.reshape(1, d), gin.reshape(1, d), w_tiles, cw, cb.reshape(1, f), w_out,
      gout.reshape(1, d), cs, p, w_gate, w_ple)
    return y, co.transpose(0, 2, 1, 3).reshape(b, CONV_W - 1, f)


def _split3(x):
    hi = x.astype(BF16)
    r1 = x - hi.astype(F32)
    mid = r1.astype(BF16)
    lo = (r1 - mid.astype(F32)).astype(BF16)
    return hi, mid, lo


def _gla_levels(tc):
    return [s for s in (16, 32, 64, 128, 256, 512, 1024) if 2 * s <= tc]


def _gla_level_table(tc):
    r = np.arange(tc)[:, None]
    c = np.arange(tc)[None, :]
    table = np.full((tc, tc), -1, np.int32)
    table[(r // GLA_BLOCK == c // GLA_BLOCK) & (c <= r)] = 0
    for i, s in enumerate(_gla_levels(tc)):
        table[(r // (2 * s) == c // (2 * s)) & ((r // s) % 2 == 1) & ((c // s) % 2 == 0)] = i + 1
    return table


def _gla_kernel(q_ref, f_ref, i_ref, g_ref, ll_ref, l1m_ref, om_ref, gn_ref, s0_ref, lvl_ref,
                o_ref, so_ref, st_ref, bg_ref, fac_ref, qi_ref, ks_ref, v_ref, a_ref, oo_ref):
    t = pl.program_id(1)
    tc = q_ref.shape[1]
    d = q_ref.shape[2]
    levels = _gla_levels(tc)

    @pl.when(t == 0)
    def _():
        for h in range(N_HEADS):
            st_ref[h] = s0_ref[0, h].T

    qh = q_ref[0]
    fp = f_ref[0]
    q = qh * _sigmoid(qh)
    e = jnp.exp(-jnp.abs(fp))
    logsig = jnp.minimum(fp, 0.0) - jnp.log(1.0 + e)
    a = ll_ref[...]
    c = l1m_ref[...] + logsig
    logf = jnp.maximum(a, c) + jnp.log(1.0 + jnp.exp(-jnp.abs(a - c)))
    kk = om_ref[...] * _sigmoid(-fp)

    row = lax.broadcasted_iota(jnp.int32, (tc, tc), 0)
    col = lax.broadcasted_iota(jnp.int32, (tc, tc), 1)
    m_incl = jnp.where(col <= row, 1.0, 0.0).astype(BF16)
    bg = sum(jnp.dot(m_incl, p, preferred_element_type=F32) for p in _split3(logf))
    bg_ref[...] = bg
    b_last = bg[tc - 1:tc, :]

    qi_ref[...] = (q * jnp.exp(bg)).astype(BF16)
    ks_ref[...] = (kk * jnp.exp(b_last - bg)).astype(BF16)
    v_ref[...] = i_ref[0].astype(BF16)

    def ref_rows(span, offset):
        pieces = []
        for a0 in range(0, tc, span):
            r = a0 + offset
            src = bg_ref[r:r + 1, :] if r >= 0 else jnp.zeros((1, d), F32)
            pieces.append(jnp.broadcast_to(src, (span, d)))
        return pieces[0] if len(pieces) == 1 else jnp.concatenate(pieces, axis=0)

    dl = bg - ref_rows(GLA_BLOCK, -1)
    fac_ref[0] = (q * jnp.exp(dl)).astype(BF16)
    fac_ref[1] = (kk * jnp.exp(-dl)).astype(BF16)
    rid = lax.broadcasted_iota(jnp.int32, (tc, 1), 0)
    for li, s in enumerate(levels):
        mid = ref_rows(2 * s, s - 1)
        right = ((rid // s) % 2) == 1
        x = jnp.exp(jnp.where(right, bg - mid, mid - bg))
        fac_ref[2 + 2 * li] = (q * x).astype(BF16)
        fac_ref[3 + 2 * li] = (kk * x).astype(BF16)

    nt = (((1,), (1,)), ((), ()))
    tn = (((0,), (0,)), ((), ()))
    lvl = lvl_ref[...]
    for h in range(N_HEADS):
        cols = slice(h * HEAD_W, (h + 1) * HEAD_W)
        att = jnp.zeros((tc, tc), F32)
        for ci in range(len(levels), -1, -1):
            prod = lax.dot_general(fac_ref[2 * ci, :, cols], fac_ref[2 * ci + 1, :, cols], nt,
                                   preferred_element_type=F32)
            att = jnp.where(lvl == ci, prod, att)
        a_ref[h] = att.astype(BF16)

    for h in range(N_HEADS):
        cols = slice(h * HEAD_W, (h + 1) * HEAD_W)
        vb = v_ref[:, cols]
        s_t = st_ref[h]
        o_inter = lax.dot_general(qi_ref[:, cols], s_t.astype(BF16), nt, preferred_element_type=F32)
        oo_ref[:, cols] = o_inter + jnp.dot(a_ref[h], vb, preferred_element_type=F32)
        upd = lax.dot_general(vb, ks_ref[:, cols], tn, preferred_element_type=F32)
        st_ref[h] = s_t * jnp.exp(b_last[:, cols]) + upd

    gh = g_ref[0]
    gate = gh * _sigmoid(gh)
    for h in range(N_HEADS):
        cols = slice(h * HEAD_W, (h + 1) * HEAD_W)
        oh = _rms(oo_ref[:, cols], gn_ref[:, cols])
        o_ref[0, :, cols] = (oh * gate[:, cols]).astype(o_ref.dtype)

    @pl.when(t == pl.num_programs(1) - 1)
    def _():
        for h in range(N_HEADS):
            so_ref[0, h] = st_ref[h].T


def _gla(proj, log_lower, log1m_lower, one_m_lower, g_out, s0, name):
    b, t, _ = proj.shape
    d = D_MODEL
    tc = _row_tile(t, GLA_TILE)
    assert tc % GLA_BLOCK == 0
    vec = pl.BlockSpec((1, d), lambda bi, ti: (0, 0))
    col_spec = lambda c: pl.BlockSpec((1, tc, d), lambda bi, ti: (bi, ti, c))
    st_spec = pl.BlockSpec((1, N_HEADS, HEAD_W, HEAD_W), lambda bi, ti: (bi, 0, 0, 0))
    n_fac = 2 * (1 + len(_gla_levels(tc)))
    return pl.pallas_call(
        _gla_kernel,
        out_shape=(jax.ShapeDtypeStruct((b, t, d), BF16),
                   jax.ShapeDtypeStruct((b, N_HEADS, HEAD_W, HEAD_W), F32)),
        grid=(b, t // tc),
        in_specs=[col_spec(0), col_spec(1), col_spec(2), col_spec(3),
                  vec, vec, vec, vec, st_spec,
                  pl.BlockSpec((tc, tc), lambda bi, ti: (0, 0))],
        out_specs=(pl.BlockSpec((1, tc, d), lambda bi, ti: (bi, ti, 0)), st_spec),
        scratch_shapes=[pltpu.VMEM((N_HEADS, HEAD_W, HEAD_W), F32),
                        pltpu.VMEM((tc, d), F32),
                        pltpu.VMEM((n_fac, tc, d), BF16),
                        pltpu.VMEM((tc, d), BF16), pltpu.VMEM((tc, d), BF16),
                        pltpu.VMEM((tc, d), BF16),
                        pltpu.VMEM((N_HEADS, tc, tc), BF16),
                        pltpu.VMEM((tc, d), F32)],
        compiler_params=_cparams("parallel", "arbitrary"),
        name=name,
    )(proj, proj, proj, proj, log_lower.reshape(1, d), log1m_lower.reshape(1, d),
      one_m_lower.reshape(1, d), g_out.reshape(1, d), s0, jnp.asarray(_gla_level_table(tc)))


def _t5_bucket(rel):
    half = NUM_BUCKETS // 2
    ret = jnp.where(rel > 0, half, 0)
    n = jnp.abs(rel)
    max_exact = half // 2
    nf = jnp.maximum(n, 1).astype(F32)
    large = max_exact + (jnp.log(nf / max_exact) / math.log(MAX_DISTANCE / max_exact)
                         * (half - max_exact)).astype(jnp.int32)
    large = jnp.minimum(large, half - 1)
    return ret + jnp.where(n < max_exact, n, large)


def _toeplitz_kernel(w_ref, o_ref, *, tq):
    tk = o_ref.shape[2]
    p = w_ref.shape[3]
    x = jnp.broadcast_to(w_ref[0, 0], (tk, p))
    o_ref[0, 0] = pltpu.roll(x, 0, axis=1, stride=1, stride_axis=0)[:, :tq]


def _bias_tiles_t(rel_bias, d0_list, tq, tk):
    p = pl.cdiv(tq + tk, HEAD_W) * HEAD_W
    m = jnp.arange(p, dtype=jnp.int32)
    rel = jnp.asarray(d0_list, jnp.int32)[:, None] + jnp.where(m <= tq, -m, p - m)[None, :]
    onehot = (_t5_bucket(rel)[:, :, None] == jnp.arange(NUM_BUCKETS, dtype=jnp.int32)).astype(F32)
    w = jnp.einsum("dpn,nh->hdp", onehot, rel_bias.astype(F32) * LOG2E,
                   precision=lax.Precision.HIGHEST)
    h, nd = w.shape[0], w.shape[1]
    return pl.pallas_call(
        functools.partial(_toeplitz_kernel, tq=tq),
        out_shape=jax.ShapeDtypeStruct((h, nd, tk, tq), F32),
        grid=(h, nd),
        in_specs=[pl.BlockSpec((1, 1, 1, p), lambda hi, di: (hi, di, 0, 0))],
        out_specs=pl.BlockSpec((1, 1, tk, tq), lambda hi, di: (hi, di, 0, 0)),
        compiler_params=_cparams("parallel", "parallel"),
        name="bias_tiles",
    )(w.reshape(h, nd, 1, p))


def _attn_kernel(*refs, segs, q_off, tq, hb):
    nseg = len(segs)
    q_ref, lam_ref, gn_ref = refs[0], refs[1], refs[2]
    seg_refs = [refs[3 + 3 * s: 6 + 3 * s] for s in range(nseg)]
    o_ref = refs[3 + 3 * nseg]
    m_ref, l_ref, acc_ref = refs[4 + 3 * nseg: 7 + 3 * nseg]
    kv_scratch = refs[7 + 3 * nseg:]
    seg_scr = [kv_scratch[4 * s: 4 * s + 4] for s in range(nseg)]

    i = pl.program_id(2)
    q0 = q_off + i * tq

    @pl.when(i == 0)
    def _():
        for (k_ref, v_ref, _), (kb_ref, vt_ref, _, _), (_, seg_len, tk, _, _) in zip(seg_refs, seg_scr, segs):
            kb_ref[...] = k_ref[0].astype(BF16)
            for kt in range(seg_len // tk):
                vt_ref[kt] = v_ref[0, kt * tk:(kt + 1) * tk, :].astype(F32).T.astype(BF16)

    q = q_ref[0]
    lane = lax.broadcasted_iota(jnp.int32, (tq, HEAD_W), 1)
    qz = []
    for hh in range(hb):
        qh = q[:, hh * HEAD_W:(hh + 1) * HEAD_W]
        qz.append([jnp.where(lane < DA_HEAD_DIM, qh, 0).astype(BF16),
                   jnp.where(lane >= DA_HEAD_DIM, qh, 0).astype(BF16)])
    m_ref[...] = jnp.full_like(m_ref, NEG_BIG)
    l_ref[...] = jnp.zeros_like(l_ref)
    acc_ref[...] = jnp.zeros_like(acc_ref)
    q_chunk = (q0 + lax.broadcasted_iota(jnp.int32, (1, tq), 1)) // CHUNK
    vis_end = ((q0 + tq - 1) // CHUNK + 1) * CHUNK
    full_end = (q0 // CHUNK + 1) * CHUNK
    nt = (((1,), (1,)), ((), ()))

    for (_, _, b_ref), (kb_ref, vt_ref, sa_ref, sb_ref), (pos0, seg_len, tk, d0_min, d0_step) in zip(
            seg_refs, seg_scr, segs):
        n_tiles = seg_len // tk
        n_vis = jnp.clip((vis_end - pos0 + tk - 1) // tk, 0, n_tiles)
        n_full = jnp.clip((full_end - pos0) // tk, 0, n_vis)

        def scores_into(kt, s_ref, kb_ref=kb_ref, tk=tk, n_tiles=n_tiles):
            r0 = pl.multiple_of(jnp.minimum(kt, n_tiles - 1) * tk, tk)
            for hh in range(hb):
                kb = kb_ref[pl.ds(r0, tk), hh * HEAD_W:(hh + 1) * HEAD_W]
                for c in range(2):
                    s_ref[2 * hh + c] = lax.dot_general(kb, qz[hh][c], nt, preferred_element_type=F32)

        def consume(kt, s_ref, masked, b_ref=b_ref, vt_ref=vt_ref, pos0=pos0, tk=tk,
                    d0_min=d0_min, d0_step=d0_step, n_tiles=n_tiles, n_vis=n_vis):
            ktc = jnp.minimum(kt, n_tiles - 1)
            r0 = ktc * tk
            d_idx = (pos0 + r0 - q0 - d0_min) // d0_step
            if masked:
                k0 = jnp.where(kt < n_vis, pos0 + r0, jnp.int32(1 << 28))
                k_chunk = (k0 + lax.broadcasted_iota(jnp.int32, (tk, 1), 0)) // CHUNK
                visible = k_chunk <= q_chunk
            for hh in range(hb):
                vt = vt_ref[ktc, hh * HEAD_W:(hh + 1) * HEAD_W, :]
                bias = b_ref[hh, d_idx]
                for c in range(2):
                    j = 2 * hh + c
                    s = s_ref[j] + bias
                    if masked:
                        s = jnp.where(visible, s, NEG_BIG)
                    m_prev = m_ref[j]
                    m_new = jnp.maximum(m_prev, jnp.max(s, axis=0, keepdims=True))
                    p = jnp.exp2(s - m_new)
                    alpha = jnp.exp2(m_prev - m_new)
                    l_ref[j] = alpha * l_ref[j] + jnp.sum(p, axis=0, keepdims=True)
                    acc_ref[j] = alpha * acc_ref[j] + jnp.dot(vt, p.astype(BF16),
                                                              preferred_element_type=F32)
                    m_ref[j] = m_new

        def pair(u, carry, masked, base, sa_ref=sa_ref, sb_ref=sb_ref):
            t0 = base + 2 * u
            scores_into(t0 + 1, sb_ref)
            consume(t0, sa_ref, masked)
            scores_into(t0 + 2, sa_ref)
            consume(t0 + 1, sb_ref, masked)
            return carry

        n_pairs_full = n_full // 2
        base = 2 * n_pairs_full
        scores_into(0, sa_ref)
        lax.fori_loop(0, n_pairs_full, functools.partial(pair, masked=False, base=0), 0)
        lax.fori_loop(0, (n_vis - base + 1) // 2, functools.partial(pair, masked=True, base=base), 0)

    lam = lam_ref[0:1, 0:1]
    for hh in range(hb):
        o_t = (acc_ref[2 * hh] * (1.0 / l_ref[2 * hh])
               - lam * (acc_ref[2 * hh + 1] * (1.0 / l_ref[2 * hh + 1])))
        cols = slice(hh * HEAD_W, (hh + 1) * HEAD_W)
        o_ref[0, :, cols] = (_rms(o_t.T, gn_ref[:, cols]) * lam_ref[0:1, 1:2]).astype(o_ref.dtype)


def _attention(q, segments, rel_bias, lam, out_scale, g_subln, q_off, name):
    b, t_real, d = q.shape
    tq_all = max(t_real, HEAD_W)
    if tq_all != t_real:
        q = jnp.pad(q, ((0, 0), (0, tq_all - t_real), (0, 0)))
    tq = _row_tile(tq_all, ATTN_TILE)
    nq = tq_all // tq
    hb = 2 * ATTN_HEADS_PER_STEP if nq == 1 else ATTN_HEADS_PER_STEP
    hw = hb * HEAD_W
    segs, args, in_specs, scratch = [], [], [], []
    for (k, v, pos0, tk) in segments:
        seg_len = k.shape[1]
        assert seg_len % tk == 0
        d0s = sorted({pos0 + kt * tk - (q_off + i * tq)
                      for i in range(nq) for kt in range(seg_len // tk)
                      if pos0 + kt * tk < ((q_off + (i + 1) * tq - 1) // CHUNK + 1) * CHUNK})
        step = math.gcd(tq, tk)
        d0s = list(range(d0s[0], d0s[-1] + 1, step))
        tiles = _bias_tiles_t(rel_bias, d0s, tq, tk)
        segs.append((pos0, seg_len, tk, d0s[0], step))
        args += [k, v, tiles]
        in_specs += [pl.BlockSpec((1, seg_len, hw), lambda bi, h, i: (bi, 0, h)),
                     pl.BlockSpec((1, seg_len, hw), lambda bi, h, i: (bi, 0, h)),
                     pl.BlockSpec((hb, len(d0s), tk, tq), lambda bi, h, i: (h, 0, 0, 0))]
        scratch += [pltpu.VMEM((seg_len, hw), BF16), pltpu.VMEM((seg_len // tk, hw, tk), BF16),
                    pltpu.VMEM((2 * hb, tk, tq), F32), pltpu.VMEM((2 * hb, tk, tq), F32)]
    scal = jnp.zeros((1, HEAD_W), F32).at[0, 0].set(lam).at[0, 1].set(out_scale)
    kern = functools.partial(_attn_kernel, segs=tuple(segs), q_off=q_off, tq=tq, hb=hb)
    out = pl.pallas_call(
        kern,
        out_shape=jax.ShapeDtypeStruct((b, tq_all, d), BF16),
        grid=(b, N_HEADS // hb, nq),
        in_specs=[pl.BlockSpec((1, tq, hw), lambda bi, h, i: (bi, i, h)),
                  pl.BlockSpec((1, HEAD_W), lambda bi, h, i: (0, 0)),
                  pl.BlockSpec((1, hw), lambda bi, h, i: (0, h))] + in_specs,
        out_specs=pl.BlockSpec((1, tq, hw), lambda bi, h, i: (bi, i, h)),
        scratch_shapes=[pltpu.VMEM((2 * hb, 1, tq), F32), pltpu.VMEM((2 * hb, 1, tq), F32),
                        pltpu.VMEM((2 * hb, HEAD_W, tq), F32)] + scratch,
        compiler_params=_cparams("parallel", "parallel", "arbitrary"),
        name=name,
    )(q, scal, g_subln.reshape(1, d), *args)
    return out[:, :t_real]


def _trunk(x, ple, q_off, past_k, past_v, hg_s0, conv_s0, wts, tag):
    (g_norms, w_in_a, lower, g_hg, w_out_a, g_kv, w_kv, rel_bias, w_q_b, lam_b,
     g_subln, w_out_b, w_ffn_in, conv_w, conv_b, w_ffn_out, w_ple, w_ple_gate) = wts
    b, t, d = x.shape
    depth = g_norms.shape[0]
    n_a = w_in_a.shape[0]
    m = b * t
    hg_out, conv_out = [], []
    k_new = v_new = None
    segments = None
    for i in range(depth):
        nm = f"{tag}{i}"
        x2 = x.reshape(m, d)
        if i < n_a:
            lw = lower[i]
            proj = _norm_matmul(x2, g_norms[i, 0], w_in_a[i], F32, nm + "_hg_in")
            o, s = _gla(proj.reshape(b, t, 4 * d), jnp.log(lw), jnp.log1p(-lw), 1.0 - lw,
                        g_hg[i], hg_s0[i], nm + "_gla")
            hg_out.append(s)
            w_mix = w_out_a[i]
        else:
            j = i - n_a
            q = _norm_matmul(x2, g_norms[i, 0], w_q_b[j], BF16, nm + "_q", scale=ATTN_Q_SCALE)
            lam_init = 0.8 - 0.6 * math.exp(-0.3 * i)
            lp = lam_b[j].astype(F32)
            lam = jnp.exp(jnp.sum(lp[0] * lp[1])) - jnp.exp(jnp.sum(lp[2] * lp[3])) + lam_init
            o = _attention(q.reshape(b, t, d), segments, rel_bias, lam, 1.0 - lam_init,
                           g_subln[j], q_off, nm + "_attn")
            w_mix = w_out_b[j]
        x, cbuf = _ffn(x, o.reshape(b, t, d), w_mix, g_norms[i, 1], g_norms[i, 2], w_ffn_in[i],
                       conv_w[i], conv_b[i], w_ffn_out[i], g_norms[i, 3], conv_s0[i],
                       ple[i], w_ple_gate[i], w_ple[i], nm + "_ffn")
        conv_out.append(cbuf)
        x2 = x.reshape(m, d)
        if i == n_a - 1:
            k_new, v_new = _norm_kv(x2, g_kv, w_kv, nm + "_kv")
            k_new = k_new.reshape(b, t, d)
            v_new = v_new.reshape(b, t, d)
            segments = []
            if past_k is not None:
                tp = past_k.shape[1]
                segments.append((past_k.reshape(b, tp, d), past_v.reshape(b, tp, d), 0, tp))
            segments.append((k_new, v_new, q_off, min(t, ATTN_KEY_TILE)))
    hd = (b, t, N_HEADS, HEAD_W)
    return x, k_new.reshape(hd), v_new.reshape(hd), jnp.stack(hg_out), jnp.stack(conv_out)


def kernel(x_prompt, x_sample, cache_k, cache_v, state_hgrn, state_conv, p_prompt, p_sample,
           g_norms, w_in_a, lb_raw, g_hg, w_out_a, g_kv, w_kv, rel_bias,
           w_q_b, lam_b, g_subln, w_out_b, w_ffn_in, conv_w, conv_b, w_ffn_out,
           w_ple, w_ple_gate):
    sm = jax.nn.softmax(lb_raw.astype(F32), axis=0)
    cs = jnp.cumsum(sm, axis=0)
    lower = cs - cs[0:1]
    bf = lambda w: w.astype(BF16)
    wts = (g_norms, bf(w_in_a), lower, g_hg, bf(w_out_a), g_kv, bf(w_kv), rel_bias,
           bf(w_q_b), lam_b, g_subln, bf(w_out_b), bf(w_ffn_in), conv_w, conv_b,
           bf(w_ffn_out), bf(w_ple), bf(w_ple_gate))
    bp, tp, _ = x_prompt.shape
    n_a, depth = w_in_a.shape[0], g_norms.shape[0]
    hg0 = jnp.zeros((n_a, bp) + state_hgrn.shape[2:], F32)
    cv0 = jnp.zeros((depth, bp) + state_conv.shape[2:], F32)
    y_p, k_p, v_p, hg_p, cv_p = _trunk(x_prompt, p_prompt, 0, None, None, hg0, cv0, wts, "p")
    y_s, k_s, v_s, hg_s, cv_s = _trunk(x_sample, p_sample, cache_k.shape[1], cache_k, cache_v,
                                       state_hgrn, state_conv, wts, "s")
    return (y_p, y_s, k_p, v_p, k_s, v_s, hg_p, hg_s, cv_p, cv_s)
```

```python
import functools
import math

import numpy as np
import jax
import jax.numpy as jnp
from jax import lax
from jax.experimental import pallas as pl
from jax.experimental.pallas import tpu as pltpu

F32 = jnp.float32
BF16 = jnp.bfloat16

D_MODEL = 1024
N_HEADS = 8
HEAD_W = D_MODEL // N_HEADS
DA_HEAD_DIM = HEAD_W // 2
CHUNK = 64
GLA_BLOCK = 16
NUM_BUCKETS = 32
MAX_DISTANCE = 256
D_FF = 2816
CONV_W = 3
EPS = 1e-6
NEG_BIG = -1e30
LOG2E = math.log2(math.e)
ATTN_Q_SCALE = DA_HEAD_DIM ** -0.5 * LOG2E

V7X_VMEM_LIMIT_BYTES = 56 * 1024 * 1024

ROW_TILE = 512
FFN_ROW_TILE = 1024
FFN_COL_TILE = 256
FFN_CONV_ROWS = 128
FFN_EDGE_ROWS = 256
GLA_TILE = 256
ATTN_TILE = 512
ATTN_KEY_TILE = 256
ATTN_HEADS_PER_STEP = 2


def _cparams(*sem):
    return pltpu.CompilerParams(dimension_semantics=sem,
                                vmem_limit_bytes=V7X_VMEM_LIMIT_BYTES)


def _rms(x, g):
    ms = jnp.mean(x * x, axis=-1, keepdims=True)
    return x * lax.rsqrt(ms + EPS) * g


def _sigmoid(z):
    return 1.0 / (1.0 + jnp.exp(-z))


def _row_tile(m, cap):
    t = min(m, cap)
    assert m % t == 0, (m, t)
    return t


def _norm_matmul_kernel(x_ref, g_ref, w_ref, o_ref, *, scale):
    h = _rms(x_ref[...], g_ref[...]).astype(BF16)
    n = w_ref.shape[1]
    tn = min(n, D_MODEL)
    for n0 in range(0, n, tn):
        y = jnp.dot(h, w_ref[:, n0:n0 + tn], preferred_element_type=F32)
        if scale != 1.0:
            y = y * scale
        o_ref[:, n0:n0 + tn] = y.astype(o_ref.dtype)


def _norm_matmul(x2d, g, w, out_dtype, name, scale=1.0):
    m, d = x2d.shape
    n = w.shape[1]
    tm = _row_tile(m, ROW_TILE)
    return pl.pallas_call(
        functools.partial(_norm_matmul_kernel, scale=scale),
        out_shape=jax.ShapeDtypeStruct((m, n), out_dtype),
        grid=(m // tm,),
        in_specs=[pl.BlockSpec((tm, d), lambda i: (i, 0)),
                  pl.BlockSpec((1, d), lambda i: (0, 0)),
                  pl.BlockSpec((d, n), lambda i: (0, 0))],
        out_specs=pl.BlockSpec((tm, n), lambda i: (i, 0)),
        compiler_params=_cparams("parallel"),
        name=name,
    )(x2d, g.reshape(1, d), w)


def _norm_kv_kernel(x_ref, g_ref, w_ref, k_ref, v_ref):
    d = k_ref.shape[1]
    h = _rms(x_ref[...], g_ref[...]).astype(BF16)
    k_ref[...] = jnp.dot(h, w_ref[:, :d], preferred_element_type=F32)
    v_ref[...] = jnp.dot(h, w_ref[:, d:], preferred_element_type=F32)


def _norm_kv(x2d, g, w, name):
    m, d = x2d.shape
    tm = _row_tile(m, ROW_TILE)
    out = jax.ShapeDtypeStruct((m, d), F32)
    return pl.pallas_call(
        _norm_kv_kernel,
        out_shape=(out, out),
        grid=(m // tm,),
        in_specs=[pl.BlockSpec((tm, d), lambda i: (i, 0)),
                  pl.BlockSpec((1, d), lambda i: (0, 0)),
                  pl.BlockSpec((d, 2 * d), lambda i: (0, 0))],
        out_specs=(pl.BlockSpec((tm, d), lambda i: (i, 0)),
                   pl.BlockSpec((tm, d), lambda i: (i, 0))),
        compiler_params=_cparams("parallel"),
        name=name,
    )(x2d, g.reshape(1, d), w)


def _ffn_kernel(x_ref, a_ref, wm_ref, gm_ref, gin_ref, wgu_ref, cw_ref, cb_ref, wo_ref, gout_ref,
                cs_ref, p_ref, wgate_ref, wple_ref,
                o_ref, co_ref, xm_ref, h_ref, acc_ref, gu_ref, act_ref):
    t = pl.program_id(1)
    j = pl.program_id(2)
    nb, tm, d = x_ref.shape
    rows = nb * tm
    rb = min(rows, FFN_EDGE_ROWS)

    def chunk(ref, r0):
        return ref[0, r0:r0 + rb, :] if nb == 1 else ref[...].reshape(rows, ref.shape[2])

    @pl.when(j == 0)
    def _():
        for r0 in range(0, rows, rb):
            m = jnp.dot(chunk(a_ref, r0), wm_ref[...], preferred_element_type=F32)
            xm = chunk(x_ref, r0) + _rms(m, gm_ref[...])
            xm_ref[r0:r0 + rb, :] = xm
            h_ref[r0:r0 + rb, :] = _rms(xm, gin_ref[...]).astype(BF16)
        acc_ref[...] = jnp.zeros_like(acc_ref)

    @pl.when((j == 0) & (t == 0))
    def _():
        co_ref[...] = cs_ref[...]

    gu_ref[...] = jnp.dot(h_ref[...], wgu_ref[0], preferred_element_type=F32)
    cw = cw_ref[...]
    cb = cb_ref[...]
    tf = cw.shape[1]
    r8 = lax.broadcasted_iota(jnp.int32, (8, tf), 0)
    rc = min(tm, FFN_CONV_ROWS)
    for bi in range(nb):
        base = bi * tm
        prev = co_ref[bi, j]
        co_ref[bi, j] = gu_ref[base + tm - (CONV_W - 1):base + tm, :tf]
        head = jnp.where(r8 == 6, prev[0:1, :], prev[1:2, :])
        for r0 in range(base, base + tm, rc):
            if r0 > base:
                head = gu_ref[r0 - 8:r0, :tf]
            gc = gu_ref[r0:r0 + rc, :tf]
            ext = jnp.concatenate([head, gc], axis=0)
            g1 = pltpu.roll(ext, 1, axis=0)[8:, :]
            g2 = pltpu.roll(ext, 2, axis=0)[8:, :]
            c = cb + cw[0:1, :] * g2 + cw[1:2, :] * g1 + cw[2:3, :] * gc
            act_ref[r0:r0 + rc, :] = (c * _sigmoid(c) * gu_ref[r0:r0 + rc, tf:]).astype(BF16)
    acc_ref[...] += jnp.dot(act_ref[...], wo_ref[...], preferred_element_type=F32)

    @pl.when(j == pl.num_programs(2) - 1)
    def _():
        for r0 in range(0, rows, rb):
            y = xm_ref[r0:r0 + rb, :] + _rms(acc_ref[r0:r0 + rb, :], gout_ref[...])
            z = jnp.dot(y.astype(BF16), wgate_ref[...], preferred_element_type=F32)
            e = jnp.dot(chunk(p_ref, r0).astype(BF16), wple_ref[...], preferred_element_type=F32)
            out = y + _sigmoid(z) * e
            if nb == 1:
                o_ref[0, r0:r0 + rb, :] = out
            else:
                o_ref[...] = out.reshape(nb, tm, d)


def _ffn(x, a, w_mix, g_mix, gin, w_in, cw, cb, w_out, gout, conv_state, p, w_gate, w_ple, name):
    b, t, d = x.shape
    f = w_out.shape[0]
    pd = p.shape[2]
    tm = _row_tile(t, FFN_ROW_TILE)
    nb = max(1, min(b, FFN_ROW_TILE // t))
    assert b % nb == 0
    rows = nb * tm
    assert rows % min(rows, FFN_EDGE_ROWS) == 0 and (nb == 1 or rows <= FFN_EDGE_ROWS)
    tf = FFN_COL_TILE
    nf = f // tf
    assert f % tf == 0 and t >= CONV_W - 1 and tm % 8 == 0
    cs = conv_state.reshape(b, CONV_W - 1, nf, tf).transpose(0, 2, 1, 3)
    cs_spec = pl.BlockSpec((nb, nf, CONV_W - 1, tf), lambda bi, ti, j: (bi, 0, 0, 0))
    w_tiles = w_in.reshape(d, 2, nf, tf).transpose(2, 0, 1, 3).reshape(nf, d, 2 * tf)
    row_spec = lambda width: pl.BlockSpec((nb, tm, width), lambda bi, ti, j: (bi, ti, 0))
    vec_spec = pl.BlockSpec((1, d), lambda bi, ti, j: (0, 0))
    const_spec = lambda shape: pl.BlockSpec(shape, lambda bi, ti, j: (0, 0),
                                            pipeline_mode=pl.Buffered(1))
    y, co = pl.pallas_call(
        _ffn_kernel,
        out_shape=(jax.ShapeDtypeStruct((b, t, d), F32),
                   jax.ShapeDtypeStruct(cs.shape, F32)),
        grid=(b // nb, t // tm, nf),
        in_specs=[row_spec(d), row_spec(d), const_spec((d, d)), vec_spec,
                  vec_spec,
                  pl.BlockSpec((1, d, 2 * tf), lambda bi, ti, j: (j, 0, 0)),
                  pl.BlockSpec((CONV_W, tf), lambda bi, ti, j: (0, j)),
                  pl.BlockSpec((1, tf), lambda bi, ti, j: (0, j)),
                  pl.BlockSpec((tf, d), lambda bi, ti, j: (j, 0)),
                  vec_spec,
                  cs_spec, row_spec(pd), const_spec((d, d)), const_spec((pd, d))],
        out_specs=(row_spec(d), cs_spec),
        scratch_shapes=[pltpu.VMEM((rows, d), F32),
                        pltpu.VMEM((rows, d), BF16),
                        pltpu.VMEM((rows, d), F32),
                        pltpu.VMEM((rows, 2 * tf), F32),
                        pltpu.VMEM((rows, tf), BF16)],
        compiler_params=_cparams("parallel", "arbitrary", "arbitrary"),
        name=name,
    )(x, a, w_mix, g_mix.reshape(1, d), gin.reshape(1, d), w_tiles, cw, cb.reshape(1, f), w_out,
      gout.reshape(1, d), cs, p, w_gate, w_ple)
    return y, co.transpose(0, 2, 1, 3).reshape(b, CONV_W - 1, f)


def _split3(x):
    hi = x.astype(BF16)
    r1 = x - hi.astype(F32)
    mid = r1.astype(BF16)
    lo = (r1 - mid.astype(F32)).astype(BF16)
    return hi, mid, lo


def _gla_levels(tc):
    return [s for s in (16, 32, 64, 128, 256, 512, 1024) if 2 * s <= tc]


def _gla_level_table(tc):
    r = np.arange(tc)[:, None]
    c = np.arange(tc)[None, :]
    table = np.full((tc, tc), -1, np.int32)
    table[(r // GLA_BLOCK == c // GLA_BLOCK) & (c <= r)] = 0
    for i, s in enumerate(_gla_levels(tc)):
        table[(r // (2 * s) == c // (2 * s)) & ((r // s) % 2 == 1) & ((c // s) % 2 == 0)] = i + 1
    return table


def _gla_kernel(q_ref, f_ref, i_ref, g_ref, ll_ref, l1m_ref, om_ref, gn_ref, s0_ref, lvl_ref,
                o_ref, so_ref, st_ref, bg_ref, fac_ref, qi_ref, ks_ref, v_ref, a_ref, oo_ref):
    t = pl.program_id(1)
    tc = q_ref.shape[1]
    d = q_ref.shape[2]
    levels = _gla_levels(tc)

    @pl.when(t == 0)
    def _():
        for h in range(N_HEADS):
            st_ref[h] = s0_ref[0, h].T

    qh = q_ref[0]
    fp = f_ref[0]
    q = qh * _sigmoid(qh)
    e = jnp.exp(-jnp.abs(fp))
    logsig = jnp.minimum(fp, 0.0) - jnp.log(1.0 + e)
    a = ll_ref[...]
    c = l1m_ref[...] + logsig
    logf = jnp.maximum(a, c) + jnp.log(1.0 + jnp.exp(-jnp.abs(a - c)))
    kk = om_ref[...] * (jnp.where(fp >= 0.0, e, 1.0) / (1.0 + e))

    row = lax.broadcasted_iota(jnp.int32, (tc, tc), 0)
    col = lax.broadcasted_iota(jnp.int32, (tc, tc), 1)
    m_incl = jnp.where(col <= row, 1.0, 0.0).astype(BF16)
    bg = sum(jnp.dot(m_incl, p, preferred_element_type=F32) for p in _split3(logf * LOG2E))
    bg_ref[...] = bg
    b_last = bg[tc - 1:tc, :]

    qi_ref[...] = (q * jnp.exp2(bg)).astype(BF16)
    ks_ref[...] = (kk * jnp.exp2(b_last - bg)).astype(BF16)
    v_ref[...] = i_ref[0].astype(BF16)

    def ref_rows(span, offset):
        pieces = []
        for a0 in range(0, tc, span):
            r = a0 + offset
            src = bg_ref[r:r + 1, :] if r >= 0 else jnp.zeros((1, d), F32)
            pieces.append(jnp.broadcast_to(src, (span, d)))
        return pieces[0] if len(pieces) == 1 else jnp.concatenate(pieces, axis=0)

    dl = bg - ref_rows(GLA_BLOCK, -1)
    fac_ref[0] = (q * jnp.exp2(dl)).astype(BF16)
    fac_ref[1] = (kk * jnp.exp2(-dl)).astype(BF16)
    rid = lax.broadcasted_iota(jnp.int32, (tc, 1), 0)
    for li, s in enumerate(levels):
        mid = ref_rows(2 * s, s - 1)
        right = ((rid // s) % 2) == 1
        x = jnp.exp2(jnp.where(right, bg - mid, mid - bg))
        fac_ref[2 + 2 * li] = (q * x).astype(BF16)
        fac_ref[3 + 2 * li] = (kk * x).astype(BF16)

    nt = (((1,), (1,)), ((), ()))
    tn = (((0,), (0,)), ((), ()))
    lvl = lvl_ref[...]
    for h in range(N_HEADS):
        cols = slice(h * HEAD_W, (h + 1) * HEAD_W)
        att = jnp.zeros((tc, tc), F32)
        for ci in range(len(levels), -1, -1):
            prod = lax.dot_general(fac_ref[2 * ci, :, cols], fac_ref[2 * ci + 1, :, cols], nt,
                                   preferred_element_type=F32)
            att = jnp.where(lvl == ci, prod, att)
        a_ref[h] = att.astype(BF16)

    for h in range(N_HEADS):
        cols = slice(h * HEAD_W, (h + 1) * HEAD_W)
        vb = v_ref[:, cols]
        s_t = st_ref[h]
        o_inter = lax.dot_general(qi_ref[:, cols], s_t.astype(BF16), nt, preferred_element_type=F32)
        oo_ref[:, cols] = o_inter + jnp.dot(a_ref[h], vb, preferred_element_type=F32)
        upd = lax.dot_general(vb, ks_ref[:, cols], tn, preferred_element_type=F32)
        st_ref[h] = s_t * jnp.exp2(b_last[:, cols]) + upd

    gh = g_ref[0]
    gate = gh * _sigmoid(gh)
    for h in range(N_HEADS):
        cols = slice(h * HEAD_W, (h + 1) * HEAD_W)
        oh = _rms(oo_ref[:, cols], gn_ref[:, cols])
        o_ref[0, :, cols] = (oh * gate[:, cols]).astype(o_ref.dtype)

    @pl.when(t == pl.num_programs(1) - 1)
    def _():
        for h in range(N_HEADS):
            so_ref[0, h] = st_ref[h].T


def _gla(proj, log_lower, log1m_lower, one_m_lower, g_out, s0, name):
    b, t, _ = proj.shape
    d = D_MODEL
    tc = _row_tile(t, GLA_TILE)
    assert tc % GLA_BLOCK == 0
    vec = pl.BlockSpec((1, d), lambda bi, ti: (0, 0))
    col_spec = lambda c: pl.BlockSpec((1, tc, d), lambda bi, ti: (bi, ti, c))
    st_spec = pl.BlockSpec((1, N_HEADS, HEAD_W, HEAD_W), lambda bi, ti: (bi, 0, 0, 0))
    n_fac = 2 * (1 + len(_gla_levels(tc)))
    return pl.pallas_call(
        _gla_kernel,
        out_shape=(jax.ShapeDtypeStruct((b, t, d), BF16),
                   jax.ShapeDtypeStruct((b, N_HEADS, HEAD_W, HEAD_W), F32)),
        grid=(b, t // tc),
        in_specs=[col_spec(0), col_spec(1), col_spec(2), col_spec(3),
                  vec, vec, vec, vec, st_spec,
                  pl.BlockSpec((tc, tc), lambda bi, ti: (0, 0))],
        out_specs=(pl.BlockSpec((1, tc, d), lambda bi, ti: (bi, ti, 0)), st_spec),
        scratch_shapes=[pltpu.VMEM((N_HEADS, HEAD_W, HEAD_W), F32),
                        pltpu.VMEM((tc, d), F32),
                        pltpu.VMEM((n_fac, tc, d), BF16),
                        pltpu.VMEM((tc, d), BF16), pltpu.VMEM((tc, d), BF16),
                        pltpu.VMEM((tc, d), BF16),
                        pltpu.VMEM((N_HEADS, tc, tc), BF16),
                        pltpu.VMEM((tc, d), F32)],
        compiler_params=_cparams("parallel", "arbitrary"),
        name=name,
    )(proj, proj, proj, proj, log_lower.reshape(1, d), log1m_lower.reshape(1, d),
      one_m_lower.reshape(1, d), g_out.reshape(1, d), s0, jnp.asarray(_gla_level_table(tc)))


def _t5_bucket(rel):
    half = NUM_BUCKETS // 2
    ret = jnp.where(rel > 0, half, 0)
    n = jnp.abs(rel)
    max_exact = half // 2
    nf = jnp.maximum(n, 1).astype(F32)
    large = max_exact + (jnp.log(nf / max_exact) / math.log(MAX_DISTANCE / max_exact)
                         * (half - max_exact)).astype(jnp.int32)
    large = jnp.minimum(large, half - 1)
    return ret + jnp.where(n < max_exact, n, large)


def _toeplitz_kernel(w_ref, o_ref, *, tq):
    tk = o_ref.shape[2]
    p = w_ref.shape[3]
    x = jnp.broadcast_to(w_ref[0, 0], (tk, p))
    o_ref[0, 0] = pltpu.roll(x, 0, axis=1, stride=1, stride_axis=0)[:, :tq]


def _bias_tiles_t(rel_bias, d0_list, tq, tk):
    p = pl.cdiv(tq + tk, HEAD_W) * HEAD_W
    m = jnp.arange(p, dtype=jnp.int32)
    rel = jnp.asarray(d0_list, jnp.int32)[:, None] + jnp.where(m <= tq, -m, p - m)[None, :]
    onehot = (_t5_bucket(rel)[:, :, None] == jnp.arange(NUM_BUCKETS, dtype=jnp.int32)).astype(F32)
    w = jnp.einsum("dpn,nh->hdp", onehot, rel_bias.astype(F32) * LOG2E,
                   precision=lax.Precision.HIGHEST)
    h, nd = w.shape[0], w.shape[1]
    return pl.pallas_call(
        functools.partial(_toeplitz_kernel, tq=tq),
        out_shape=jax.ShapeDtypeStruct((h, nd, tk, tq), F32),
        grid=(h, nd),
        in_specs=[pl.BlockSpec((1, 1, 1, p), lambda hi, di: (hi, di, 0, 0))],
        out_specs=pl.BlockSpec((1, 1, tk, tq), lambda hi, di: (hi, di, 0, 0)),
        compiler_params=_cparams("parallel", "parallel"),
        name="bias_tiles",
    )(w.reshape(h, nd, 1, p))


def _attn_kernel(*refs, segs, q_off, tq, hb):
    nseg = len(segs)
    q_ref, lam_ref, gn_ref = refs[0], refs[1], refs[2]
    seg_refs = [refs[3 + 3 * s: 6 + 3 * s] for s in range(nseg)]
    o_ref = refs[3 + 3 * nseg]
    m_ref, l_ref, acc_ref = refs[4 + 3 * nseg: 7 + 3 * nseg]
    kv_scratch = refs[7 + 3 * nseg:]
    seg_scr = [kv_scratch[4 * s: 4 * s + 4] for s in range(nseg)]

    i = pl.program_id(2)
    q0 = q_off + i * tq

    @pl.when(i == 0)
    def _():
        for (k_ref, v_ref, _), (kb_ref, vt_ref, _, _), (_, seg_len, tk, _, _) in zip(seg_refs, seg_scr, segs):
            kb_ref[...] = k_ref[0].astype(BF16)
            for kt in range(seg_len // tk):
                vt_ref[kt] = v_ref[0, kt * tk:(kt + 1) * tk, :].astype(F32).T.astype(BF16)

    q = q_ref[0]
    lane = lax.broadcasted_iota(jnp.int32, (tq, HEAD_W), 1)
    qz = []
    for hh in range(hb):
        qh = q[:, hh * HEAD_W:(hh + 1) * HEAD_W]
        qz.append([jnp.where(lane < DA_HEAD_DIM, qh, 0).astype(BF16),
                   jnp.where(lane >= DA_HEAD_DIM, qh, 0).astype(BF16)])
    m_ref[...] = jnp.full_like(m_ref, NEG_BIG)
    l_ref[...] = jnp.zeros_like(l_ref)
    acc_ref[...] = jnp.zeros_like(acc_ref)
    q_chunk = (q0 + lax.broadcasted_iota(jnp.int32, (1, tq), 1)) // CHUNK
    vis_end = ((q0 + tq - 1) // CHUNK + 1) * CHUNK
    full_end = (q0 // CHUNK + 1) * CHUNK
    nt = (((1,), (1,)), ((), ()))

    for (_, _, b_ref), (kb_ref, vt_ref, sa_ref, sb_ref), (pos0, seg_len, tk, d0_min, d0_step) in zip(
            seg_refs, seg_scr, segs):
        n_tiles = seg_len // tk
        n_vis = jnp.clip((vis_end - pos0 + tk - 1) // tk, 0, n_tiles)
        n_full = jnp.clip((full_end - pos0) // tk, 0, n_vis)

        def scores_into(kt, s_ref, c0=0, kb_ref=kb_ref, tk=tk, n_tiles=n_tiles):
            r0 = pl.multiple_of(jnp.minimum(kt, n_tiles - 1) * tk, tk)
            for hh in range(hb):
                kb = kb_ref[pl.ds(r0, tk), hh * HEAD_W:(hh + 1) * HEAD_W]
                for c in range(2):
                    s_ref[2 * hh + c, :, c0:] = lax.dot_general(kb, qz[hh][c][c0:, :], nt,
                                                                preferred_element_type=F32)

        def consume(kt, s_ref, masked, c0=0, b_ref=b_ref, vt_ref=vt_ref, pos0=pos0, tk=tk,
                    d0_min=d0_min, d0_step=d0_step, n_tiles=n_tiles, n_vis=n_vis):
            ktc = jnp.minimum(kt, n_tiles - 1)
            r0 = ktc * tk
            d_idx = (pos0 + r0 - q0 - d0_min) // d0_step
            if masked:
                k0 = jnp.where(kt < n_vis, pos0 + r0, jnp.int32(1 << 28))
                k_chunk = (k0 + lax.broadcasted_iota(jnp.int32, (tk, 1), 0)) // CHUNK
                visible = k_chunk <= q_chunk[:, c0:]
            for hh in range(hb):
                vt = vt_ref[ktc, hh * HEAD_W:(hh + 1) * HEAD_W, :]
                bias = b_ref[hh, d_idx, :, c0:]
                for c in range(2):
                    j = 2 * hh + c
                    s = s_ref[j, :, c0:] + bias
                    if masked:
                        s = jnp.where(visible, s, NEG_BIG)
                    m_prev = m_ref[j, :, c0:]
                    m_new = jnp.maximum(m_prev, jnp.max(s, axis=0, keepdims=True))
                    p = jnp.exp2(s - m_new)
                    alpha = jnp.exp2(m_prev - m_new)
                    l_ref[j, :, c0:] = alpha * l_ref[j, :, c0:] + jnp.sum(p, axis=0, keepdims=True)
                    acc_ref[j, :, c0:] = alpha * acc_ref[j, :, c0:] + jnp.dot(
                        vt, p.astype(BF16), preferred_element_type=F32)
                    m_ref[j, :, c0:] = m_new

        def pair(u, carry, masked, base, sa_ref=sa_ref, sb_ref=sb_ref):
            t0 = base + 2 * u
            scores_into(t0 + 1, sb_ref)
            consume(t0, sa_ref, masked)
            scores_into(t0 + 2, sa_ref)
            consume(t0 + 1, sb_ref, masked)
            return carry

        n_pairs_full = n_full // 2
        base = 2 * n_pairs_full
        scores_into(0, sa_ref)
        lax.fori_loop(0, n_pairs_full, functools.partial(pair, masked=False, base=0), 0)
        if pos0 == q_off and tq == 2 * tk and tk % CHUNK == 0 and q_off % CHUNK == 0:
            scores_into(base + 1, sb_ref, c0=tk)
            consume(base, sa_ref, True)
            consume(base + 1, sb_ref, True, c0=tk)
        else:
            lax.fori_loop(0, (n_vis - base + 1) // 2, functools.partial(pair, masked=True, base=base), 0)

    lam = lam_ref[0:1, 0:1]
    for hh in range(hb):
        o_t = (acc_ref[2 * hh] * (1.0 / l_ref[2 * hh])
               - lam * (acc_ref[2 * hh + 1] * (1.0 / l_ref[2 * hh + 1])))
        cols = slice(hh * HEAD_W, (hh + 1) * HEAD_W)
        ms = jnp.mean(o_t * o_t, axis=0, keepdims=True)
        o_n = (o_t * lax.rsqrt(ms + EPS)).T
        o_ref[0, :, cols] = (o_n * gn_ref[:, cols] * lam_ref[0:1, 1:2]).astype(o_ref.dtype)


def _attention(q, segments, rel_bias, lam, out_scale, g_subln, q_off, name):
    b, t_real, d = q.shape
    tq_all = max(t_real, HEAD_W)
    if tq_all != t_real:
        q = jnp.pad(q, ((0, 0), (0, tq_all - t_real), (0, 0)))
    tq = _row_tile(tq_all, ATTN_TILE)
    nq = tq_all // tq
    hb = 2 * ATTN_HEADS_PER_STEP if nq == 1 else ATTN_HEADS_PER_STEP
    hw = hb * HEAD_W
    segs, args, in_specs, scratch = [], [], [], []
    for (k, v, pos0, tk) in segments:
        seg_len = k.shape[1]
        assert seg_len % tk == 0
        d0s = sorted({pos0 + kt * tk - (q_off + i * tq)
                      for i in range(nq) for kt in range(seg_len // tk)
                      if pos0 + kt * tk < ((q_off + (i + 1) * tq - 1) // CHUNK + 1) * CHUNK})
        step = math.gcd(tq, tk)
        d0s = list(range(d0s[0], d0s[-1] + 1, step))
        tiles = _bias_tiles_t(rel_bias, d0s, tq, tk)
        segs.append((pos0, seg_len, tk, d0s[0], step))
        args += [k, v, tiles]
        in_specs += [pl.BlockSpec((1, seg_len, hw), lambda bi, h, i: (bi, 0, h)),
                     pl.BlockSpec((1, seg_len, hw), lambda bi, h, i: (bi, 0, h)),
                     pl.BlockSpec((hb, len(d0s), tk, tq), lambda bi, h, i: (h, 0, 0, 0))]
        scratch += [pltpu.VMEM((seg_len, hw), BF16), pltpu.VMEM((seg_len // tk, hw, tk), BF16),
                    pltpu.VMEM((2 * hb, tk, tq), F32), pltpu.VMEM((2 * hb, tk, tq), F32)]
    scal = jnp.zeros((1, HEAD_W), F32).at[0, 0].set(lam).at[0, 1].set(out_scale)
    kern = functools.partial(_attn_kernel, segs=tuple(segs), q_off=q_off, tq=tq, hb=hb)
    out = pl.pallas_call(
        kern,
        out_shape=jax.ShapeDtypeStruct((b, tq_all, d), BF16),
        grid=(b, N_HEADS // hb, nq),
        in_specs=[pl.BlockSpec((1, tq, hw), lambda bi, h, i: (bi, i, h)),
                  pl.BlockSpec((1, HEAD_W), lambda bi, h, i: (0, 0)),
                  pl.BlockSpec((1, hw), lambda bi, h, i: (0, h))] + in_specs,
        out_specs=pl.BlockSpec((1, tq, hw), lambda bi, h, i: (bi, i, h)),
        scratch_shapes=[pltpu.VMEM((2 * hb, 1, tq), F32), pltpu.VMEM((2 * hb, 1, tq), F32),
                        pltpu.VMEM((2 * hb, HEAD_W, tq), F32)] + scratch,
        compiler_params=_cparams("parallel", "parallel", "arbitrary"),
        name=name,
    )(q, scal, g_subln.reshape(1, d), *args)
    return out[:, :t_real]


def _trunk(x, ple, q_off, past_k, past_v, hg_s0, conv_s0, wts, tag):
    (g_norms, w_in_a, lower, g_hg, w_out_a, g_kv, w_kv, rel_bias, w_q_b, lam_b,
     g_subln, w_out_b, w_ffn_in, conv_w, conv_b, w_ffn_out, w_ple, w_ple_gate) = wts
    b, t, d = x.shape
    depth = g_norms.shape[0]
    n_a = w_in_a.shape[0]
    m = b * t
    hg_out, conv_out = [], []
    k_new = v_new = None
    segments = None
    for i in range(depth):
        nm = f"{tag}{i}"
        x2 = x.reshape(m, d)
        if i < n_a:
            lw = lower[i]
            proj = _norm_matmul(x2, g_norms[i, 0], w_in_a[i], F32, nm + "_hg_in")
            o, s = _gla(proj.reshape(b, t, 4 * d), jnp.log(lw), jnp.log1p(-lw), 1.0 - lw,
                        g_hg[i], hg_s0[i], nm + "_gla")
            hg_out.append(s)
            w_mix = w_out_a[i]
        else:
            j = i - n_a
            q = _norm_matmul(x2, g_norms[i, 0], w_q_b[j], BF16, nm + "_q", scale=ATTN_Q_SCALE)
            lam_init = 0.8 - 0.6 * math.exp(-0.3 * i)
            lp = lam_b[j].astype(F32)
            lam = jnp.exp(jnp.sum(lp[0] * lp[1])) - jnp.exp(jnp.sum(lp[2] * lp[3])) + lam_init
            o = _attention(q.reshape(b, t, d), segments, rel_bias, lam, 1.0 - lam_init,
                           g_subln[j], q_off, nm + "_attn")
            w_mix = w_out_b[j]
        x, cbuf = _ffn(x, o.reshape(b, t, d), w_mix, g_norms[i, 1], g_norms[i, 2], w_ffn_in[i],
                       conv_w[i], conv_b[i], w_ffn_out[i], g_norms[i, 3], conv_s0[i],
                       ple[i], w_ple_gate[i], w_ple[i], nm + "_ffn")
        conv_out.append(cbuf)
        x2 = x.reshape(m, d)
        if i == n_a - 1:
            k_new, v_new = _norm_kv(x2, g_kv, w_kv, nm + "_kv")
            k_new = k_new.reshape(b, t, d)
            v_new = v_new.reshape(b, t, d)
            segments = []
            if past_k is not None:
                tp = past_k.shape[1]
                segments.append((past_k.reshape(b, tp, d), past_v.reshape(b, tp, d), 0, tp))
            segments.append((k_new, v_new, q_off, min(t, ATTN_KEY_TILE)))
    hd = (b, t, N_HEADS, HEAD_W)
    return x, k_new.reshape(hd), v_new.reshape(hd), jnp.stack(hg_out), jnp.stack(conv_out)


def kernel(x_prompt, x_sample, cache_k, cache_v, state_hgrn, state_conv, p_prompt, p_sample,
           g_norms, w_in_a, lb_raw, g_hg, w_out_a, g_kv, w_kv, rel_bias,
           w_q_b, lam_b, g_subln, w_out_b, w_ffn_in, conv_w, conv_b, w_ffn_out,
           w_ple, w_ple_gate):
    sm = jax.nn.softmax(lb_raw.astype(F32), axis=0)
    cs = jnp.cumsum(sm, axis=0)
    lower = cs - cs[0:1]
    bf = lambda w: w.astype(BF16)
    wts = (g_norms, bf(w_in_a), lower, g_hg, bf(w_out_a), g_kv, bf(w_kv), rel_bias,
           bf(w_q_b), lam_b, g_subln, bf(w_out_b), bf(w_ffn_in), conv_w, conv_b,
           bf(w_ffn_out), bf(w_ple), bf(w_ple_gate))
    bp, tp, _ = x_prompt.shape
    n_a, depth = w_in_a.shape[0], g_norms.shape[0]
    hg0 = jnp.zeros((n_a, bp) + state_hgrn.shape[2:], F32)
    cv0 = jnp.zeros((depth, bp) + state_conv.shape[2:], F32)
    y_p, k_p, v_p, hg_p, cv_p = _trunk(x_prompt, p_prompt, 0, None, None, hg0, cv0, wts, "p")
    y_s, k_s, v_s, hg_s, cv_s = _trunk(x_sample, p_sample, cache_k.shape[1], cache_k, cache_v,
                                       state_hgrn, state_conv, wts, "s")
    return (y_p, y_s, k_p, v_p, k_s, v_s, hg_p, hg_s, cv_p, cv_s)
```

```python
import functools
import math

import numpy as np
import jax
import jax.numpy as jnp
from jax import lax
from jax.experimental import pallas as pl
from jax.experimental.pallas import tpu as pltpu

F32 = jnp.float32
BF16 = jnp.bfloat16

D_MODEL = 1024
N_HEADS = 8
HEAD_W = D_MODEL // N_HEADS
DA_HEAD_DIM = HEAD_W // 2
CHUNK = 64
GLA_BLOCK = 16
NUM_BUCKETS = 32
MAX_DISTANCE = 256
D_FF = 2816
CONV_W = 3
EPS = 1e-6
NEG_BIG = -1e30
LOG2E = math.log2(math.e)
ATTN_Q_SCALE = DA_HEAD_DIM ** -0.5 * LOG2E

V7X_VMEM_LIMIT_BYTES = 56 * 1024 * 1024

ROW_TILE = 512
FFN_ROW_TILE = 1024
FFN_COL_TILE = 256
FFN_CONV_ROWS = 128
FFN_EDGE_ROWS = 256
GLA_TILE = 256
ATTN_TILE = 512
ATTN_KEY_TILE = 256
ATTN_HEADS_PER_STEP = 2


def _cparams(*sem):
    return pltpu.CompilerParams(dimension_semantics=sem,
                                vmem_limit_bytes=V7X_VMEM_LIMIT_BYTES)


def _rms(x, g):
    ms = jnp.mean(x * x, axis=-1, keepdims=True)
    return x * lax.rsqrt(ms + EPS) * g


def _sigmoid(z):
    return 1.0 / (1.0 + jnp.exp(-z))


def _row_tile(m, cap):
    t = min(m, cap)
    assert m % t == 0, (m, t)
    return t


def _norm_matmul_kernel(x_ref, g_ref, w_ref, o_ref, *, scale):
    h = _rms(x_ref[...], g_ref[...]).astype(BF16)
    n = w_ref.shape[1]
    tn = min(n, D_MODEL)
    for n0 in range(0, n, tn):
        y = jnp.dot(h, w_ref[:, n0:n0 + tn], preferred_element_type=F32)
        if scale != 1.0:
            y = y * scale
        o_ref[:, n0:n0 + tn] = y.astype(o_ref.dtype)


def _norm_matmul(x2d, g, w, out_dtype, name, scale=1.0):
    m, d = x2d.shape
    n = w.shape[1]
    tm = _row_tile(m, ROW_TILE)
    return pl.pallas_call(
        functools.partial(_norm_matmul_kernel, scale=scale),
        out_shape=jax.ShapeDtypeStruct((m, n), out_dtype),
        grid=(m // tm,),
        in_specs=[pl.BlockSpec((tm, d), lambda i: (i, 0)),
                  pl.BlockSpec((1, d), lambda i: (0, 0)),
                  pl.BlockSpec((d, n), lambda i: (0, 0))],
        out_specs=pl.BlockSpec((tm, n), lambda i: (i, 0)),
        compiler_params=_cparams("parallel"),
        name=name,
    )(x2d, g.reshape(1, d), w)


def _norm_kv_kernel(x_ref, g_ref, w_ref, k_ref, v_ref):
    d = k_ref.shape[1]
    h = _rms(x_ref[...], g_ref[...]).astype(BF16)
    k_ref[...] = jnp.dot(h, w_ref[:, :d], preferred_element_type=F32)
    v_ref[...] = jnp.dot(h, w_ref[:, d:], preferred_element_type=F32)


def _norm_kv(x2d, g, w, name):
    m, d = x2d.shape
    tm = _row_tile(m, ROW_TILE)
    out = jax.ShapeDtypeStruct((m, d), F32)
    return pl.pallas_call(
        _norm_kv_kernel,
        out_shape=(out, out),
        grid=(m // tm,),
        in_specs=[pl.BlockSpec((tm, d), lambda i: (i, 0)),
                  pl.BlockSpec((1, d), lambda i: (0, 0)),
                  pl.BlockSpec((d, 2 * d), lambda i: (0, 0))],
        out_specs=(pl.BlockSpec((tm, d), lambda i: (i, 0)),
                   pl.BlockSpec((tm, d), lambda i: (i, 0))),
        compiler_params=_cparams("parallel"),
        name=name,
    )(x2d, g.reshape(1, d), w)


def _ffn_kernel(x_ref, a_ref, wm_ref, gm_ref, gin_ref, wg_ref, wu_ref, cw_ref, cb_ref, wo_ref, gout_ref,
                cs_ref, p_ref, wgate_ref, wple_ref,
                o_ref, co_ref, xm_ref, h_ref, acc_ref, gu_ref, act_ref):
    t = pl.program_id(1)
    j = pl.program_id(2)
    nb, tm, d = x_ref.shape
    rows = nb * tm
    rb = min(rows, FFN_EDGE_ROWS)

    def chunk(ref, r0):
        return ref[0, r0:r0 + rb, :] if nb == 1 else ref[...].reshape(rows, ref.shape[2])

    @pl.when(j == 0)
    def _():
        for r0 in range(0, rows, rb):
            m = jnp.dot(chunk(a_ref, r0), wm_ref[...], preferred_element_type=F32)
            xm = chunk(x_ref, r0) + _rms(m, gm_ref[...])
            xm_ref[r0:r0 + rb, :] = xm
            h_ref[r0:r0 + rb, :] = _rms(xm, gin_ref[...]).astype(BF16)
        acc_ref[...] = jnp.zeros_like(acc_ref)

    @pl.when((j == 0) & (t == 0))
    def _():
        co_ref[...] = cs_ref[...]

    cw = cw_ref[...]
    cb = cb_ref[...]
    tf = cw.shape[1]
    gu_ref[:, :tf] = jnp.dot(h_ref[...], wg_ref[...], preferred_element_type=F32)
    gu_ref[:, tf:] = jnp.dot(h_ref[...], wu_ref[...], preferred_element_type=F32)
    r8 = lax.broadcasted_iota(jnp.int32, (8, tf), 0)
    rc = min(tm, FFN_CONV_ROWS)
    for bi in range(nb):
        base = bi * tm
        prev = co_ref[bi, j]
        co_ref[bi, j] = gu_ref[base + tm - (CONV_W - 1):base + tm, :tf]
        head = jnp.where(r8 == 6, prev[0:1, :], prev[1:2, :])
        for r0 in range(base, base + tm, rc):
            if r0 > base:
                head = gu_ref[r0 - 8:r0, :tf]
            gc = gu_ref[r0:r0 + rc, :tf]
            ext = jnp.concatenate([head, gc], axis=0)
            g1 = pltpu.roll(ext, 1, axis=0)[8:, :]
            g2 = pltpu.roll(ext, 2, axis=0)[8:, :]
            c = cb + cw[0:1, :] * g2 + cw[1:2, :] * g1 + cw[2:3, :] * gc
            act_ref[r0:r0 + rc, :] = (c * _sigmoid(c) * gu_ref[r0:r0 + rc, tf:]).astype(BF16)
    acc_ref[...] += jnp.dot(act_ref[...], wo_ref[...], preferred_element_type=F32)

    @pl.when(j == pl.num_programs(2) - 1)
    def _():
        for r0 in range(0, rows, rb):
            y = xm_ref[r0:r0 + rb, :] + _rms(acc_ref[r0:r0 + rb, :], gout_ref[...])
            z = jnp.dot(y.astype(BF16), wgate_ref[...], preferred_element_type=F32)
            e = jnp.dot(chunk(p_ref, r0).astype(BF16), wple_ref[...], preferred_element_type=F32)
            out = y + _sigmoid(z) * e
            if nb == 1:
                o_ref[0, r0:r0 + rb, :] = out
            else:
                o_ref[...] = out.reshape(nb, tm, d)


def _ffn(x, a, w_mix, g_mix, gin, w_in, cw, cb, w_out, gout, conv_state, p, w_gate, w_ple, name):
    b, t, d = x.shape
    f = w_out.shape[0]
    pd = p.shape[2]
    tm = _row_tile(t, FFN_ROW_TILE)
    nb = max(1, min(b, FFN_ROW_TILE // t))
    assert b % nb == 0
    rows = nb * tm
    assert rows % min(rows, FFN_EDGE_ROWS) == 0 and (nb == 1 or rows <= FFN_EDGE_ROWS)
    tf = FFN_COL_TILE
    nf = f // tf
    assert f % tf == 0 and t >= CONV_W - 1 and tm % 8 == 0
    cs = conv_state.reshape(b, CONV_W - 1, nf, tf).transpose(0, 2, 1, 3)
    cs_spec = pl.BlockSpec((nb, nf, CONV_W - 1, tf), lambda bi, ti, j: (bi, 0, 0, 0))
    row_spec = lambda width: pl.BlockSpec((nb, tm, width), lambda bi, ti, j: (bi, ti, 0))
    vec_spec = pl.BlockSpec((1, d), lambda bi, ti, j: (0, 0))
    const_spec = lambda shape: pl.BlockSpec(shape, lambda bi, ti, j: (0, 0),
                                            pipeline_mode=pl.Buffered(1))
    y, co = pl.pallas_call(
        _ffn_kernel,
        out_shape=(jax.ShapeDtypeStruct((b, t, d), F32),
                   jax.ShapeDtypeStruct(cs.shape, F32)),
        grid=(b // nb, t // tm, nf),
        in_specs=[row_spec(d), row_spec(d), const_spec((d, d)), vec_spec,
                  vec_spec,
                  pl.BlockSpec((d, tf), lambda bi, ti, j: (0, j)),
                  pl.BlockSpec((d, tf), lambda bi, ti, j: (0, j + nf)),
                  pl.BlockSpec((CONV_W, tf), lambda bi, ti, j: (0, j)),
                  pl.BlockSpec((1, tf), lambda bi, ti, j: (0, j)),
                  pl.BlockSpec((tf, d), lambda bi, ti, j: (j, 0)),
                  vec_spec,
                  cs_spec, row_spec(pd), const_spec((d, d)), const_spec((pd, d))],
        out_specs=(row_spec(d), cs_spec),
        scratch_shapes=[pltpu.VMEM((rows, d), F32),
                        pltpu.VMEM((rows, d), BF16),
                        pltpu.VMEM((rows, d), F32),
                        pltpu.VMEM((rows, 2 * tf), F32),
                        pltpu.VMEM((rows, tf), BF16)],
        compiler_params=_cparams("parallel", "arbitrary", "arbitrary"),
        name=name,
    )(x, a, w_mix, g_mix.reshape(1, d), gin.reshape(1, d), w_in, w_in, cw, cb.reshape(1, f), w_out,
      gout.reshape(1, d), cs, p, w_gate, w_ple)
    return y, co.transpose(0, 2, 1, 3).reshape(b, CONV_W - 1, f)


def _split3(x):
    hi = x.astype(BF16)
    r1 = x - hi.astype(F32)
    mid = r1.astype(BF16)
    lo = (r1 - mid.astype(F32)).astype(BF16)
    return hi, mid, lo


def _gla_levels(tc):
    return [s for s in (16, 32, 64, 128, 256, 512, 1024) if 2 * s <= tc]


def _gla_level_table(tc):
    r = np.arange(tc)[:, None]
    c = np.arange(tc)[None, :]
    table = np.full((tc, tc), -1, np.int32)
    table[(r // GLA_BLOCK == c // GLA_BLOCK) & (c <= r)] = 0
    for i, s in enumerate(_gla_levels(tc)):
        table[(r // (2 * s) == c // (2 * s)) & ((r // s) % 2 == 1) & ((c // s) % 2 == 0)] = i + 1
    return table


def _gla_kernel(q_ref, f_ref, i_ref, g_ref, ll_ref, l1m_ref, om_ref, gn_ref, s0_ref, lvl_ref,
                o_ref, so_ref, st_ref, bg_ref, fac_ref, qi_ref, ks_ref, v_ref, a_ref, oo_ref):
    t = pl.program_id(1)
    tc = q_ref.shape[1]
    d = q_ref.shape[2]
    levels = _gla_levels(tc)

    @pl.when(t == 0)
    def _():
        for h in range(N_HEADS):
            st_ref[h] = s0_ref[0, h].T

    qh = q_ref[0]
    fp = f_ref[0]
    q = qh * _sigmoid(qh)
    e = jnp.exp(-jnp.abs(fp))
    logsig = jnp.minimum(fp, 0.0) - jnp.log(1.0 + e)
    a = ll_ref[...]
    c = l1m_ref[...] + logsig
    logf = jnp.maximum(a, c) + jnp.log(1.0 + jnp.exp(-jnp.abs(a - c)))
    kk = om_ref[...] * (jnp.where(fp >= 0.0, e, 1.0) / (1.0 + e))

    row = lax.broadcasted_iota(jnp.int32, (tc, tc), 0)
    col = lax.broadcasted_iota(jnp.int32, (tc, tc), 1)
    m_incl = jnp.where(col <= row, 1.0, 0.0).astype(BF16)
    bg = sum(jnp.dot(m_incl, p, preferred_element_type=F32) for p in _split3(logf * LOG2E))
    bg_ref[...] = bg
    b_last = bg[tc - 1:tc, :]

    qi_ref[...] = (q * jnp.exp2(bg)).astype(BF16)
    ks_ref[...] = (kk * jnp.exp2(b_last - bg)).astype(BF16)
    v_ref[...] = i_ref[0].astype(BF16)

    def ref_rows(span, offset):
        pieces = []
        for a0 in range(0, tc, span):
            r = a0 + offset
            src = bg_ref[r:r + 1, :] if r >= 0 else jnp.zeros((1, d), F32)
            pieces.append(jnp.broadcast_to(src, (span, d)))
        return pieces[0] if len(pieces) == 1 else jnp.concatenate(pieces, axis=0)

    dl = bg - ref_rows(GLA_BLOCK, -1)
    fac_ref[0] = (q * jnp.exp2(dl)).astype(BF16)
    fac_ref[1] = (kk * jnp.exp2(-dl)).astype(BF16)
    rid = lax.broadcasted_iota(jnp.int32, (tc, 1), 0)
    for li, s in enumerate(levels):
        mid = ref_rows(2 * s, s - 1)
        right = ((rid // s) % 2) == 1
        x = jnp.exp2(jnp.where(right, bg - mid, mid - bg))
        fac_ref[2 + 2 * li] = (q * x).astype(BF16)
        fac_ref[3 + 2 * li] = (kk * x).astype(BF16)

    nt = (((1,), (1,)), ((), ()))
    tn = (((0,), (0,)), ((), ()))
    lvl = lvl_ref[...]
    for h in range(N_HEADS):
        cols = slice(h * HEAD_W, (h + 1) * HEAD_W)
        att = jnp.zeros((tc, tc), F32)
        for ci in range(len(levels), -1, -1):
            prod = lax.dot_general(fac_ref[2 * ci, :, cols], fac_ref[2 * ci + 1, :, cols], nt,
                                   preferred_element_type=F32)
            att = jnp.where(lvl == ci, prod, att)
        a_ref[h] = att.astype(BF16)

    for h in range(N_HEADS):
        cols = slice(h * HEAD_W, (h + 1) * HEAD_W)
        vb = v_ref[:, cols]
        s_t = st_ref[h]
        o_inter = lax.dot_general(qi_ref[:, cols], s_t.astype(BF16), nt, preferred_element_type=F32)
        oo_ref[:, cols] = o_inter + jnp.dot(a_ref[h], vb, preferred_element_type=F32)
        upd = lax.dot_general(vb, ks_ref[:, cols], tn, preferred_element_type=F32)
        st_ref[h] = s_t * jnp.exp2(b_last[:, cols]) + upd

    gh = g_ref[0]
    gate = gh * _sigmoid(gh)
    for h in range(N_HEADS):
        cols = slice(h * HEAD_W, (h + 1) * HEAD_W)
        oh = _rms(oo_ref[:, cols], gn_ref[:, cols])
        o_ref[0, :, cols] = (oh * gate[:, cols]).astype(o_ref.dtype)

    @pl.when(t == pl.num_programs(1) - 1)
    def _():
        for h in range(N_HEADS):
            so_ref[0, h] = st_ref[h].T


def _gla(proj, log_lower, log1m_lower, one_m_lower, g_out, s0, name):
    b, t, _ = proj.shape
    d = D_MODEL
    tc = _row_tile(t, GLA_TILE)
    assert tc % GLA_BLOCK == 0
    vec = pl.BlockSpec((1, d), lambda bi, ti: (0, 0))
    col_spec = lambda c: pl.BlockSpec((1, tc, d), lambda bi, ti: (bi, ti, c))
    st_spec = pl.BlockSpec((1, N_HEADS, HEAD_W, HEAD_W), lambda bi, ti: (bi, 0, 0, 0))
    n_fac = 2 * (1 + len(_gla_levels(tc)))
    return pl.pallas_call(
        _gla_kernel,
        out_shape=(jax.ShapeDtypeStruct((b, t, d), BF16),
                   jax.ShapeDtypeStruct((b, N_HEADS, HEAD_W, HEAD_W), F32)),
        grid=(b, t // tc),
        in_specs=[col_spec(0), col_spec(1), col_spec(2), col_spec(3),
                  vec, vec, vec, vec, st_spec,
                  pl.BlockSpec((tc, tc), lambda bi, ti: (0, 0))],
        out_specs=(pl.BlockSpec((1, tc, d), lambda bi, ti: (bi, ti, 0)), st_spec),
        scratch_shapes=[pltpu.VMEM((N_HEADS, HEAD_W, HEAD_W), F32),
                        pltpu.VMEM((tc, d), F32),
                        pltpu.VMEM((n_fac, tc, d), BF16),
                        pltpu.VMEM((tc, d), BF16), pltpu.VMEM((tc, d), BF16),
                        pltpu.VMEM((tc, d), BF16),
                        pltpu.VMEM((N_HEADS, tc, tc), BF16),
                        pltpu.VMEM((tc, d), F32)],
        compiler_params=_cparams("parallel", "arbitrary"),
        name=name,
    )(proj, proj, proj, proj, log_lower.reshape(1, d), log1m_lower.reshape(1, d),
      one_m_lower.reshape(1, d), g_out.reshape(1, d), s0, jnp.asarray(_gla_level_table(tc)))


def _t5_bucket(rel):
    half = NUM_BUCKETS // 2
    ret = jnp.where(rel > 0, half, 0)
    n = jnp.abs(rel)
    max_exact = half // 2
    nf = jnp.maximum(n, 1).astype(F32)
    large = max_exact + (jnp.log(nf / max_exact) / math.log(MAX_DISTANCE / max_exact)
                         * (half - max_exact)).astype(jnp.int32)
    large = jnp.minimum(large, half - 1)
    return ret + jnp.where(n < max_exact, n, large)


def _toeplitz_kernel(w_ref, o_ref, *, tq):
    tk = o_ref.shape[2]
    p = w_ref.shape[3]
    x = jnp.broadcast_to(w_ref[0, 0], (tk, p))
    o_ref[0, 0] = pltpu.roll(x, 0, axis=1, stride=1, stride_axis=0)[:, :tq]


def _bias_tiles_t(rel_bias, d0_list, tq, tk):
    p = pl.cdiv(tq + tk, HEAD_W) * HEAD_W
    m = jnp.arange(p, dtype=jnp.int32)
    rel = jnp.asarray(d0_list, jnp.int32)[:, None] + jnp.where(m <= tq, -m, p - m)[None, :]
    onehot = (_t5_bucket(rel)[:, :, None] == jnp.arange(NUM_BUCKETS, dtype=jnp.int32)).astype(F32)
    w = jnp.einsum("dpn,nh->hdp", onehot, rel_bias.astype(F32) * LOG2E,
                   precision=lax.Precision.HIGHEST)
    h, nd = w.shape[0], w.shape[1]
    return pl.pallas_call(
        functools.partial(_toeplitz_kernel, tq=tq),
        out_shape=jax.ShapeDtypeStruct((h, nd, tk, tq), F32),
        grid=(h, nd),
        in_specs=[pl.BlockSpec((1, 1, 1, p), lambda hi, di: (hi, di, 0, 0))],
        out_specs=pl.BlockSpec((1, 1, tk, tq), lambda hi, di: (hi, di, 0, 0)),
        compiler_params=_cparams("parallel", "parallel"),
        name="bias_tiles",
    )(w.reshape(h, nd, 1, p))


def _attn_kernel(*refs, segs, q_off, tq, hb):
    nseg = len(segs)
    q_ref, lam_ref, gn_ref = refs[0], refs[1], refs[2]
    seg_refs = [refs[3 + 3 * s: 6 + 3 * s] for s in range(nseg)]
    o_ref = refs[3 + 3 * nseg]
    m_ref, l_ref, acc_ref = refs[4 + 3 * nseg: 7 + 3 * nseg]
    kv_scratch = refs[7 + 3 * nseg:]
    seg_scr = [kv_scratch[4 * s: 4 * s + 4] for s in range(nseg)]

    i = pl.program_id(2)
    q0 = q_off + i * tq

    @pl.when(i == 0)
    def _():
        for (k_ref, v_ref, _), (kb_ref, vt_ref, _, _), (_, seg_len, tk, _, _) in zip(seg_refs, seg_scr, segs):
            kb_ref[...] = k_ref[0].astype(BF16)
            for kt in range(seg_len // tk):
                vt_ref[kt] = v_ref[0, kt * tk:(kt + 1) * tk, :].astype(F32).T.astype(BF16)

    q = q_ref[0]
    lane = lax.broadcasted_iota(jnp.int32, (tq, HEAD_W), 1)
    qz = []
    for hh in range(hb):
        qh = q[:, hh * HEAD_W:(hh + 1) * HEAD_W]
        qz.append([jnp.where(lane < DA_HEAD_DIM, qh, 0).astype(BF16),
                   jnp.where(lane >= DA_HEAD_DIM, qh, 0).astype(BF16)])
    m_ref[...] = jnp.full_like(m_ref, NEG_BIG)
    l_ref[...] = jnp.zeros_like(l_ref)
    acc_ref[...] = jnp.zeros_like(acc_ref)
    q_chunk = (q0 + lax.broadcasted_iota(jnp.int32, (1, tq), 1)) // CHUNK
    vis_end = ((q0 + tq - 1) // CHUNK + 1) * CHUNK
    full_end = (q0 // CHUNK + 1) * CHUNK
    nt = (((1,), (1,)), ((), ()))

    for (_, _, b_ref), (kb_ref, vt_ref, sa_ref, sb_ref), (pos0, seg_len, tk, d0_min, d0_step) in zip(
            seg_refs, seg_scr, segs):
        n_tiles = seg_len // tk
        n_vis = jnp.clip((vis_end - pos0 + tk - 1) // tk, 0, n_tiles)
        n_full = jnp.clip((full_end - pos0) // tk, 0, n_vis)

        def scores_into(kt, s_ref, c0=0, kb_ref=kb_ref, tk=tk, n_tiles=n_tiles):
            r0 = pl.multiple_of(jnp.minimum(kt, n_tiles - 1) * tk, tk)
            for hh in range(hb):
                kb = kb_ref[pl.ds(r0, tk), hh * HEAD_W:(hh + 1) * HEAD_W]
                for c in range(2):
                    s_ref[2 * hh + c, :, c0:] = lax.dot_general(kb, qz[hh][c][c0:, :], nt,
                                                                preferred_element_type=F32)

        def consume(kt, s_ref, masked, c0=0, b_ref=b_ref, vt_ref=vt_ref, pos0=pos0, tk=tk,
                    d0_min=d0_min, d0_step=d0_step, n_tiles=n_tiles, n_vis=n_vis):
            ktc = jnp.minimum(kt, n_tiles - 1)
            r0 = ktc * tk
            d_idx = (pos0 + r0 - q0 - d0_min) // d0_step
            if masked:
                k0 = jnp.where(kt < n_vis, pos0 + r0, jnp.int32(1 << 28))
                k_chunk = (k0 + lax.broadcasted_iota(jnp.int32, (tk, 1), 0)) // CHUNK
                visible = k_chunk <= q_chunk[:, c0:]
            for hh in range(hb):
                vt = vt_ref[ktc, hh * HEAD_W:(hh + 1) * HEAD_W, :]
                bias = b_ref[hh, d_idx, :, c0:]
                for c in range(2):
                    j = 2 * hh + c
                    s = s_ref[j, :, c0:] + bias
                    if masked:
                        s = jnp.where(visible, s, NEG_BIG)
                    m_prev = m_ref[j, :, c0:]
                    m_new = jnp.maximum(m_prev, jnp.max(s, axis=0, keepdims=True))
                    p = jnp.exp2(s - m_new)
                    alpha = jnp.exp2(m_prev - m_new)
                    l_ref[j, :, c0:] = alpha * l_ref[j, :, c0:] + jnp.sum(p, axis=0, keepdims=True)
                    acc_ref[j, :, c0:] = alpha * acc_ref[j, :, c0:] + jnp.dot(
                        vt, p.astype(BF16), preferred_element_type=F32)
                    m_ref[j, :, c0:] = m_new

        def pair(u, carry, masked, base, sa_ref=sa_ref, sb_ref=sb_ref):
            t0 = base + 2 * u
            scores_into(t0 + 1, sb_ref)
            consume(t0, sa_ref, masked)
            scores_into(t0 + 2, sa_ref)
            consume(t0 + 1, sb_ref, masked)
            return carry

        n_pairs_full = n_full // 2
        base = 2 * n_pairs_full
        scores_into(0, sa_ref)
        if n_tiles == 1:
            consume(0, sa_ref, True)
            continue
        lax.fori_loop(0, n_pairs_full, functools.partial(pair, masked=False, base=0), 0)
        if pos0 == q_off and tq == 2 * tk and tk % CHUNK == 0 and q_off % CHUNK == 0:
            scores_into(base + 1, sb_ref, c0=tk)
            consume(base, sa_ref, True)
            consume(base + 1, sb_ref, True, c0=tk)
        else:
            lax.fori_loop(0, (n_vis - base + 1) // 2, functools.partial(pair, masked=True, base=base), 0)

    lam = lam_ref[0:1, 0:1]
    for hh in range(hb):
        o_t = (acc_ref[2 * hh] * (1.0 / l_ref[2 * hh])
               - lam * (acc_ref[2 * hh + 1] * (1.0 / l_ref[2 * hh + 1])))
        cols = slice(hh * HEAD_W, (hh + 1) * HEAD_W)
        ms = jnp.mean(o_t * o_t, axis=0, keepdims=True)
        o_n = (o_t * lax.rsqrt(ms + EPS)).T
        o_ref[0, :, cols] = (o_n * gn_ref[:, cols] * lam_ref[0:1, 1:2]).astype(o_ref.dtype)


def _attention(q, segments, rel_bias, lam, out_scale, g_subln, q_off, name):
    b, t_real, d = q.shape
    tq_all = max(t_real, HEAD_W)
    if tq_all != t_real:
        q = jnp.pad(q, ((0, 0), (0, tq_all - t_real), (0, 0)))
    tq = _row_tile(tq_all, ATTN_TILE)
    nq = tq_all // tq
    hb = 2 * ATTN_HEADS_PER_STEP if nq == 1 else ATTN_HEADS_PER_STEP
    hw = hb * HEAD_W
    segs, args, in_specs, scratch = [], [], [], []
    for (k, v, pos0, tk) in segments:
        seg_len = k.shape[1]
        assert seg_len % tk == 0
        d0s = sorted({pos0 + kt * tk - (q_off + i * tq)
                      for i in range(nq) for kt in range(seg_len // tk)
                      if pos0 + kt * tk < ((q_off + (i + 1) * tq - 1) // CHUNK + 1) * CHUNK})
        step = math.gcd(tq, tk)
        d0s = list(range(d0s[0], d0s[-1] + 1, step))
        tiles = _bias_tiles_t(rel_bias, d0s, tq, tk)
        segs.append((pos0, seg_len, tk, d0s[0], step))
        args += [k, v, tiles]
        in_specs += [pl.BlockSpec((1, seg_len, hw), lambda bi, h, i: (bi, 0, h)),
                     pl.BlockSpec((1, seg_len, hw), lambda bi, h, i: (bi, 0, h)),
                     pl.BlockSpec((hb, len(d0s), tk, tq), lambda bi, h, i: (h, 0, 0, 0))]
        scratch += [pltpu.VMEM((seg_len, hw), BF16), pltpu.VMEM((seg_len // tk, hw, tk), BF16),
                    pltpu.VMEM((2 * hb, tk, tq), F32), pltpu.VMEM((2 * hb, tk, tq), F32)]
    scal = jnp.zeros((1, HEAD_W), F32).at[0, 0].set(lam).at[0, 1].set(out_scale)
    kern = functools.partial(_attn_kernel, segs=tuple(segs), q_off=q_off, tq=tq, hb=hb)
    out = pl.pallas_call(
        kern,
        out_shape=jax.ShapeDtypeStruct((b, tq_all, d), BF16),
        grid=(b, N_HEADS // hb, nq),
        in_specs=[pl.BlockSpec((1, tq, hw), lambda bi, h, i: (bi, i, h)),
                  pl.BlockSpec((1, HEAD_W), lambda bi, h, i: (0, 0)),
                  pl.BlockSpec((1, hw), lambda bi, h, i: (0, h))] + in_specs,
        out_specs=pl.BlockSpec((1, tq, hw), lambda bi, h, i: (bi, i, h)),
        scratch_shapes=[pltpu.VMEM((2 * hb, 1, tq), F32), pltpu.VMEM((2 * hb, 1, tq), F32),
                        pltpu.VMEM((2 * hb, HEAD_W, tq), F32)] + scratch,
        compiler_params=_cparams("parallel", "parallel", "arbitrary"),
        name=name,
    )(q, scal, g_subln.reshape(1, d), *args)
    return out[:, :t_real]


def _trunk(x, ple, q_off, past_k, past_v, hg_s0, conv_s0, wts, tag):
    (g_norms, w_in_a, lower, g_hg, w_out_a, g_kv, w_kv, rel_bias, w_q_b, lam_b,
     g_subln, w_out_b, w_ffn_in, conv_w, conv_b, w_ffn_out, w_ple, w_ple_gate) = wts
    b, t, d = x.shape
    depth = g_norms.shape[0]
    n_a = w_in_a.shape[0]
    m = b * t
    hg_out, conv_out = [], []
    k_new = v_new = None
    segments = None
    for i in range(depth):
        nm = f"{tag}{i}"
        x2 = x.reshape(m, d)
        if i < n_a:
            lw = lower[i]
            proj = _norm_matmul(x2, g_norms[i, 0], w_in_a[i], F32, nm + "_hg_in")
            o, s = _gla(proj.reshape(b, t, 4 * d), jnp.log(lw), jnp.log1p(-lw), 1.0 - lw,
                        g_hg[i], hg_s0[i], nm + "_gla")
            hg_out.append(s)
            w_mix = w_out_a[i]
        else:
            j = i - n_a
            q = _norm_matmul(x2, g_norms[i, 0], w_q_b[j], BF16, nm + "_q", scale=ATTN_Q_SCALE)
            lam_init = 0.8 - 0.6 * math.exp(-0.3 * i)
            lp = lam_b[j].astype(F32)
            lam = jnp.exp(jnp.sum(lp[0] * lp[1])) - jnp.exp(jnp.sum(lp[2] * lp[3])) + lam_init
            o = _attention(q.reshape(b, t, d), segments, rel_bias, lam, 1.0 - lam_init,
                           g_subln[j], q_off, nm + "_attn")
            w_mix = w_out_b[j]
        x, cbuf = _ffn(x, o.reshape(b, t, d), w_mix, g_norms[i, 1], g_norms[i, 2], w_ffn_in[i],
                       conv_w[i], conv_b[i], w_ffn_out[i], g_norms[i, 3], conv_s0[i],
                       ple[i], w_ple_gate[i], w_ple[i], nm + "_ffn")
        conv_out.append(cbuf)
        x2 = x.reshape(m, d)
        if i == n_a - 1:
            k_new, v_new = _norm_kv(x2, g_kv, w_kv, nm + "_kv")
            k_new = k_new.reshape(b, t, d)
            v_new = v_new.reshape(b, t, d)
            segments = []
            if past_k is not None:
                tp = past_k.shape[1]
                segments.append((past_k.reshape(b, tp, d), past_v.reshape(b, tp, d), 0, tp))
            segments.append((k_new, v_new, q_off, min(t, ATTN_KEY_TILE)))
    hd = (b, t, N_HEADS, HEAD_W)
    return x, k_new.reshape(hd), v_new.reshape(hd), jnp.stack(hg_out), jnp.stack(conv_out)


def kernel(x_prompt, x_sample, cache_k, cache_v, state_hgrn, state_conv, p_prompt, p_sample,
           g_norms, w_in_a, lb_raw, g_hg, w_out_a, g_kv, w_kv, rel_bias,
           w_q_b, lam_b, g_subln, w_out_b, w_ffn_in, conv_w, conv_b, w_ffn_out,
           w_ple, w_ple_gate):
    sm = jax.nn.softmax(lb_raw.astype(F32), axis=0)
    cs = jnp.cumsum(sm, axis=0)
    lower = cs - cs[0:1]
    bf = lambda w: w.astype(BF16)
    wts = (g_norms, bf(w_in_a), lower, g_hg, bf(w_out_a), g_kv, bf(w_kv), rel_bias,
           bf(w_q_b), lam_b, g_subln, bf(w_out_b), bf(w_ffn_in), conv_w, conv_b,
           bf(w_ffn_out), bf(w_ple), bf(w_ple_gate))
    bp, tp, _ = x_prompt.shape
    n_a, depth = w_in_a.shape[0], g_norms.shape[0]
    hg0 = jnp.zeros((n_a, bp) + state_hgrn.shape[2:], F32)
    cv0 = jnp.zeros((depth, bp) + state_conv.shape[2:], F32)
    y_p, k_p, v_p, hg_p, cv_p = _trunk(x_prompt, p_prompt, 0, None, None, hg0, cv0, wts, "p")
    y_s, k_s, v_s, hg_s, cv_s = _trunk(x_sample, p_sample, cache_k.shape[1], cache_k, cache_v,
                                       state_hgrn, state_conv, wts, "s")
    return (y_p, y_s, k_p, v_p, k_s, v_s, hg_p, hg_s, cv_p, cv_s)
```

```python
import functools
import math

import numpy as np
import jax
import jax.numpy as jnp
from jax import lax
from jax.experimental import pallas as pl
from jax.experimental.pallas import tpu as pltpu

F32 = jnp.float32
BF16 = jnp.bfloat16

D_MODEL = 1024
N_HEADS = 8
HEAD_W = D_MODEL // N_HEADS
DA_HEAD_DIM = HEAD_W // 2
CHUNK = 64
GLA_BLOCK = 16
NUM_BUCKETS = 32
MAX_DISTANCE = 256
D_FF = 2816
CONV_W = 3
EPS = 1e-6
NEG_BIG = -1e30
LOG2E = math.log2(math.e)
ATTN_Q_SCALE = DA_HEAD_DIM ** -0.5 * LOG2E

V7X_VMEM_LIMIT_BYTES = 56 * 1024 * 1024

ROW_TILE = 512
FFN_ROW_TILE = 1024
FFN_COL_TILE = 256
FFN_CONV_ROWS = 128
FFN_EDGE_ROWS = 256
GLA_TILE = 256
GLA_TILES_PER_STEP = 2
ATTN_TILE = 512
ATTN_KEY_TILE = 256
ATTN_HEADS_PER_STEP = 2


def _cparams(*sem):
    return pltpu.CompilerParams(dimension_semantics=sem,
                                vmem_limit_bytes=V7X_VMEM_LIMIT_BYTES)


def _rms(x, g):
    ms = jnp.mean(x * x, axis=-1, keepdims=True)
    return x * lax.rsqrt(ms + EPS) * g


def _sigmoid(z):
    return 1.0 / (1.0 + jnp.exp(-z))


def _row_tile(m, cap):
    t = min(m, cap)
    assert m % t == 0, (m, t)
    return t


def _norm_matmul_kernel(x_ref, g_ref, w_ref, o_ref, *, scale):
    h = _rms(x_ref[...], g_ref[...]).astype(BF16)
    n = w_ref.shape[1]
    tn = min(n, D_MODEL)
    for n0 in range(0, n, tn):
        y = jnp.dot(h, w_ref[:, n0:n0 + tn], preferred_element_type=F32)
        if scale != 1.0:
            y = y * scale
        o_ref[:, n0:n0 + tn] = y.astype(o_ref.dtype)


def _norm_matmul(x2d, g, w, out_dtype, name, scale=1.0):
    m, d = x2d.shape
    n = w.shape[1]
    tm = _row_tile(m, ROW_TILE)
    return pl.pallas_call(
        functools.partial(_norm_matmul_kernel, scale=scale),
        out_shape=jax.ShapeDtypeStruct((m, n), out_dtype),
        grid=(m // tm,),
        in_specs=[pl.BlockSpec((tm, d), lambda i: (i, 0)),
                  pl.BlockSpec((1, d), lambda i: (0, 0)),
                  pl.BlockSpec((d, n), lambda i: (0, 0))],
        out_specs=pl.BlockSpec((tm, n), lambda i: (i, 0)),
        compiler_params=_cparams("parallel"),
        name=name,
    )(x2d, g.reshape(1, d), w)


def _norm_kv_kernel(x_ref, g_ref, w_ref, k_ref, v_ref):
    d = k_ref.shape[1]
    h = _rms(x_ref[...], g_ref[...]).astype(BF16)
    k_ref[...] = jnp.dot(h, w_ref[:, :d], preferred_element_type=F32)
    v_ref[...] = jnp.dot(h, w_ref[:, d:], preferred_element_type=F32)


def _norm_kv(x2d, g, w, name):
    m, d = x2d.shape
    tm = _row_tile(m, ROW_TILE)
    out = jax.ShapeDtypeStruct((m, d), F32)
    return pl.pallas_call(
        _norm_kv_kernel,
        out_shape=(out, out),
        grid=(m // tm,),
        in_specs=[pl.BlockSpec((tm, d), lambda i: (i, 0)),
                  pl.BlockSpec((1, d), lambda i: (0, 0)),
                  pl.BlockSpec((d, 2 * d), lambda i: (0, 0))],
        out_specs=(pl.BlockSpec((tm, d), lambda i: (i, 0)),
                   pl.BlockSpec((tm, d), lambda i: (i, 0))),
        compiler_params=_cparams("parallel"),
        name=name,
    )(x2d, g.reshape(1, d), w)


def _ffn_kernel(x_ref, a_ref, wm_ref, gm_ref, gin_ref, wg_ref, wu_ref, cw_ref, cb_ref, wo_ref, gout_ref,
                cs_ref, p_ref, wgate_ref, wple_ref,
                o_ref, co_ref, xm_ref, h_ref, acc_ref, gu_ref, act_ref):
    t = pl.program_id(1)
    j = pl.program_id(2)
    nb, tm, d = x_ref.shape
    rows = nb * tm
    rb = min(rows, FFN_EDGE_ROWS)

    def chunk(ref, r0):
        return ref[0, r0:r0 + rb, :] if nb == 1 else ref[...].reshape(rows, ref.shape[2])

    @pl.when(j == 0)
    def _():
        for r0 in range(0, rows, rb):
            m = jnp.dot(chunk(a_ref, r0), wm_ref[...], preferred_element_type=F32)
            xm = chunk(x_ref, r0) + _rms(m, gm_ref[...])
            xm_ref[r0:r0 + rb, :] = xm
            h_ref[r0:r0 + rb, :] = _rms(xm, gin_ref[...]).astype(BF16)
        acc_ref[...] = jnp.zeros_like(acc_ref)

    @pl.when((j == 0) & (t == 0))
    def _():
        co_ref[...] = cs_ref[...]

    cw = cw_ref[...]
    cb = cb_ref[...]
    tf = cw.shape[1]
    gu_ref[:, :tf] = jnp.dot(h_ref[...], wg_ref[...], preferred_element_type=F32)
    gu_ref[:, tf:] = jnp.dot(h_ref[...], wu_ref[...], preferred_element_type=F32)
    r8 = lax.broadcasted_iota(jnp.int32, (8, tf), 0)
    rc = min(tm, FFN_CONV_ROWS)
    for bi in range(nb):
        base = bi * tm
        prev = co_ref[bi, j]
        co_ref[bi, j] = gu_ref[base + tm - (CONV_W - 1):base + tm, :tf]
        head = jnp.where(r8 == 6, prev[0:1, :], prev[1:2, :])
        for r0 in range(base, base + tm, rc):
            if r0 > base:
                head = gu_ref[r0 - 8:r0, :tf]
            gc = gu_ref[r0:r0 + rc, :tf]
            ext = jnp.concatenate([head, gc], axis=0)
            g1 = pltpu.roll(ext, 1, axis=0)[8:, :]
            g2 = pltpu.roll(ext, 2, axis=0)[8:, :]
            c = cb + cw[0:1, :] * g2 + cw[1:2, :] * g1 + cw[2:3, :] * gc
            act_ref[r0:r0 + rc, :] = (c * _sigmoid(c) * gu_ref[r0:r0 + rc, tf:]).astype(BF16)
    acc_ref[...] += jnp.dot(act_ref[...], wo_ref[...], preferred_element_type=F32)

    @pl.when(j == pl.num_programs(2) - 1)
    def _():
        for r0 in range(0, rows, rb):
            y = xm_ref[r0:r0 + rb, :] + _rms(acc_ref[r0:r0 + rb, :], gout_ref[...])
            z = jnp.dot(y.astype(BF16), wgate_ref[...], preferred_element_type=F32)
            e = jnp.dot(chunk(p_ref, r0).astype(BF16), wple_ref[...], preferred_element_type=F32)
            out = y + _sigmoid(z) * e
            if nb == 1:
                o_ref[0, r0:r0 + rb, :] = out
            else:
                o_ref[...] = out.reshape(nb, tm, d)


def _ffn(x, a, w_mix, g_mix, gin, w_in, cw, cb, w_out, gout, conv_state, p, w_gate, w_ple, name):
    b, t, d = x.shape
    f = w_out.shape[0]
    pd = p.shape[2]
    tm = _row_tile(t, FFN_ROW_TILE)
    nb = max(1, min(b, FFN_ROW_TILE // t))
    assert b % nb == 0
    rows = nb * tm
    assert rows % min(rows, FFN_EDGE_ROWS) == 0 and (nb == 1 or rows <= FFN_EDGE_ROWS)
    tf = FFN_COL_TILE
    nf = f // tf
    assert f % tf == 0 and t >= CONV_W - 1 and tm % 8 == 0
    cs = conv_state.reshape(b, CONV_W - 1, nf, tf).transpose(0, 2, 1, 3)
    cs_spec = pl.BlockSpec((nb, nf, CONV_W - 1, tf), lambda bi, ti, j: (bi, 0, 0, 0))
    row_spec = lambda width: pl.BlockSpec((nb, tm, width), lambda bi, ti, j: (bi, ti, 0))
    vec_spec = pl.BlockSpec((1, d), lambda bi, ti, j: (0, 0))
    const_spec = lambda shape: pl.BlockSpec(shape, lambda bi, ti, j: (0, 0),
                                            pipeline_mode=pl.Buffered(1))
    y, co = pl.pallas_call(
        _ffn_kernel,
        out_shape=(jax.ShapeDtypeStruct((b, t, d), F32),
                   jax.ShapeDtypeStruct(cs.shape, F32)),
        grid=(b // nb, t // tm, nf),
        in_specs=[row_spec(d), row_spec(d), const_spec((d, d)), vec_spec,
                  vec_spec,
                  pl.BlockSpec((d, tf), lambda bi, ti, j: (0, j)),
                  pl.BlockSpec((d, tf), lambda bi, ti, j: (0, j + nf)),
                  pl.BlockSpec((CONV_W, tf), lambda bi, ti, j: (0, j)),
                  pl.BlockSpec((1, tf), lambda bi, ti, j: (0, j)),
                  pl.BlockSpec((tf, d), lambda bi, ti, j: (j, 0)),
                  vec_spec,
                  cs_spec, row_spec(pd), const_spec((d, d)), const_spec((pd, d))],
        out_specs=(row_spec(d), cs_spec),
        scratch_shapes=[pltpu.VMEM((rows, d), F32),
                        pltpu.VMEM((rows, d), BF16),
                        pltpu.VMEM((rows, d), F32),
                        pltpu.VMEM((rows, 2 * tf), F32),
                        pltpu.VMEM((rows, tf), BF16)],
        compiler_params=_cparams("parallel", "arbitrary", "arbitrary"),
        name=name,
    )(x, a, w_mix, g_mix.reshape(1, d), gin.reshape(1, d), w_in, w_in, cw, cb.reshape(1, f), w_out,
      gout.reshape(1, d), cs, p, w_gate, w_ple)
    return y, co.transpose(0, 2, 1, 3).reshape(b, CONV_W - 1, f)


def _split3(x):
    hi = x.astype(BF16)
    r1 = x - hi.astype(F32)
    mid = r1.astype(BF16)
    lo = (r1 - mid.astype(F32)).astype(BF16)
    return hi, mid, lo


def _gla_levels(tc):
    return [s for s in (16, 32, 64, 128, 256, 512, 1024) if 2 * s <= tc]


def _gla_level_table(tc):
    r = np.arange(tc)[:, None]
    c = np.arange(tc)[None, :]
    table = np.full((tc, tc), -1, np.int32)
    table[(r // GLA_BLOCK == c // GLA_BLOCK) & (c <= r)] = 0
    for i, s in enumerate(_gla_levels(tc)):
        table[(r // (2 * s) == c // (2 * s)) & ((r // s) % 2 == 1) & ((c // s) % 2 == 0)] = i + 1
    return table


def _gla_kernel(x_ref, gx_ref, w_ref, ll_ref, l1m_ref, om_ref, gn_ref, s0_ref, lvl_ref,
                o_ref, so_ref, st_ref, proj_ref, bg_ref, fac_ref, qi_ref, ks_ref, v_ref, a_ref, oo_ref,
                *, tc):
    t = pl.program_id(1)
    d = x_ref.shape[2]
    n_sub = x_ref.shape[1] // tc

    @pl.when(t == 0)
    def _():
        for h in range(N_HEADS):
            st_ref[h] = s0_ref[0, h].T

    def project(s):
        hx = _rms(x_ref[0, s * tc:(s + 1) * tc, :], gx_ref[...]).astype(BF16)
        for n0 in range(0, 4 * d, d):
            proj_ref[s, :, n0:n0 + d] = jnp.dot(hx, w_ref[:, n0:n0 + d], preferred_element_type=F32)

    project(0)
    for s in range(n_sub):
        if s + 1 < n_sub:
            project(s + 1)
        _gla_tile(proj_ref.at[s], ll_ref, l1m_ref, om_ref, gn_ref, lvl_ref, o_ref.at[0, s * tc:(s + 1) * tc, :],
                  st_ref, bg_ref, fac_ref, qi_ref, ks_ref, v_ref, a_ref, oo_ref)

    @pl.when(t == pl.num_programs(1) - 1)
    def _():
        for h in range(N_HEADS):
            so_ref[0, h] = st_ref[h].T


def _gla_tile(p_ref, ll_ref, l1m_ref, om_ref, gn_ref, lvl_ref, o_ref,
              st_ref, bg_ref, fac_ref, qi_ref, ks_ref, v_ref, a_ref, oo_ref):
    tc = p_ref.shape[0]
    d = p_ref.shape[1] // 4
    levels = _gla_levels(tc)

    qh = p_ref[:, 0:d]
    fp = p_ref[:, d:2 * d]
    q = qh * _sigmoid(qh)
    e = jnp.exp(-jnp.abs(fp))
    logsig = jnp.minimum(fp, 0.0) - jnp.log(1.0 + e)
    a = ll_ref[...]
    c = l1m_ref[...] + logsig
    logf = jnp.maximum(a, c) + jnp.log(1.0 + jnp.exp(-jnp.abs(a - c)))
    kk = om_ref[...] * (jnp.where(fp >= 0.0, e, 1.0) / (1.0 + e))

    row = lax.broadcasted_iota(jnp.int32, (tc, tc), 0)
    col = lax.broadcasted_iota(jnp.int32, (tc, tc), 1)
    m_incl = jnp.where(col <= row, 1.0, 0.0).astype(BF16)
    bg = sum(jnp.dot(m_incl, p, preferred_element_type=F32) for p in _split3(logf * LOG2E))
    bg_ref[...] = bg
    b_last = bg[tc - 1:tc, :]

    qi_ref[...] = (q * jnp.exp2(bg)).astype(BF16)
    ks_ref[...] = (kk * jnp.exp2(b_last - bg)).astype(BF16)
    v_ref[...] = p_ref[:, 2 * d:3 * d].astype(BF16)

    def ref_rows(span, offset):
        pieces = []
        for a0 in range(0, tc, span):
            r = a0 + offset
            src = bg_ref[r:r + 1, :] if r >= 0 else jnp.zeros((1, d), F32)
            pieces.append(jnp.broadcast_to(src, (span, d)))
        return pieces[0] if len(pieces) == 1 else jnp.concatenate(pieces, axis=0)

    dl = bg - ref_rows(GLA_BLOCK, -1)
    fac_ref[0] = (q * jnp.exp2(dl)).astype(BF16)
    fac_ref[1] = (kk * jnp.exp2(-dl)).astype(BF16)
    rid = lax.broadcasted_iota(jnp.int32, (tc, 1), 0)
    for li, s in enumerate(levels):
        mid = ref_rows(2 * s, s - 1)
        right = ((rid // s) % 2) == 1
        x = jnp.exp2(jnp.where(right, bg - mid, mid - bg))
        fac_ref[2 + 2 * li] = (q * x).astype(BF16)
        fac_ref[3 + 2 * li] = (kk * x).astype(BF16)

    nt = (((1,), (1,)), ((), ()))
    tn = (((0,), (0,)), ((), ()))
    lvl = lvl_ref[...]
    for h in range(N_HEADS):
        cols = slice(h * HEAD_W, (h + 1) * HEAD_W)
        att = jnp.zeros((tc, tc), F32)
        for ci in range(len(levels), -1, -1):
            prod = lax.dot_general(fac_ref[2 * ci, :, cols], fac_ref[2 * ci + 1, :, cols], nt,
                                   preferred_element_type=F32)
            att = jnp.where(lvl == ci, prod, att)
        a_ref[h] = att.astype(BF16)

    for h in range(N_HEADS):
        cols = slice(h * HEAD_W, (h + 1) * HEAD_W)
        vb = v_ref[:, cols]
        s_t = st_ref[h]
        o_inter = lax.dot_general(qi_ref[:, cols], s_t.astype(BF16), nt, preferred_element_type=F32)
        oo_ref[:, cols] = o_inter + jnp.dot(a_ref[h], vb, preferred_element_type=F32)
        upd = lax.dot_general(vb, ks_ref[:, cols], tn, preferred_element_type=F32)
        st_ref[h] = s_t * jnp.exp2(b_last[:, cols]) + upd

    gh = p_ref[:, 3 * d:4 * d]
    gate = gh * _sigmoid(gh)
    for h in range(N_HEADS):
        cols = slice(h * HEAD_W, (h + 1) * HEAD_W)
        oh = _rms(oo_ref[:, cols], gn_ref[:, cols])
        o_ref[:, cols] = (oh * gate[:, cols]).astype(o_ref.dtype)


def _gla(x, g_in, w_in, log_lower, log1m_lower, one_m_lower, g_out, s0, name):
    b, t, d = x.shape
    tc = _row_tile(t, GLA_TILE)
    assert tc % GLA_BLOCK == 0
    n_sub = GLA_TILES_PER_STEP if t % (GLA_TILES_PER_STEP * tc) == 0 else 1
    rows = n_sub * tc
    vec = pl.BlockSpec((1, d), lambda bi, ti: (0, 0))
    st_spec = pl.BlockSpec((1, N_HEADS, HEAD_W, HEAD_W), lambda bi, ti: (bi, 0, 0, 0))
    n_fac = 2 * (1 + len(_gla_levels(tc)))
    return pl.pallas_call(
        functools.partial(_gla_kernel, tc=tc),
        out_shape=(jax.ShapeDtypeStruct((b, t, d), BF16),
                   jax.ShapeDtypeStruct((b, N_HEADS, HEAD_W, HEAD_W), F32)),
        grid=(b, t // rows),
        in_specs=[pl.BlockSpec((1, rows, d), lambda bi, ti: (bi, ti, 0)),
                  vec,
                  pl.BlockSpec((d, 4 * d), lambda bi, ti: (0, 0), pipeline_mode=pl.Buffered(1)),
                  vec, vec, vec, vec, st_spec,
                  pl.BlockSpec((tc, tc), lambda bi, ti: (0, 0))],
        out_specs=(pl.BlockSpec((1, rows, d), lambda bi, ti: (bi, ti, 0)), st_spec),
        scratch_shapes=[pltpu.VMEM((N_HEADS, HEAD_W, HEAD_W), F32),
                        pltpu.VMEM((n_sub, tc, 4 * d), F32),
                        pltpu.VMEM((tc, d), F32),
                        pltpu.VMEM((n_fac, tc, d), BF16),
                        pltpu.VMEM((tc, d), BF16), pltpu.VMEM((tc, d), BF16),
                        pltpu.VMEM((tc, d), BF16),
                        pltpu.VMEM((N_HEADS, tc, tc), BF16),
                        pltpu.VMEM((tc, d), F32)],
        compiler_params=_cparams("parallel", "arbitrary"),
        name=name,
    )(x, g_in.reshape(1, d), w_in, log_lower.reshape(1, d), log1m_lower.reshape(1, d),
      one_m_lower.reshape(1, d), g_out.reshape(1, d), s0, jnp.asarray(_gla_level_table(tc)))


def _t5_bucket(rel):
    half = NUM_BUCKETS // 2
    ret = jnp.where(rel > 0, half, 0)
    n = jnp.abs(rel)
    max_exact = half // 2
    nf = jnp.maximum(n, 1).astype(F32)
    large = max_exact + (jnp.log(nf / max_exact) / math.log(MAX_DISTANCE / max_exact)
                         * (half - max_exact)).astype(jnp.int32)
    large = jnp.minimum(large, half - 1)
    return ret + jnp.where(n < max_exact, n, large)


def _toeplitz_kernel(w_ref, o_ref, *, tq):
    tk = o_ref.shape[2]
    p = w_ref.shape[3]
    x = jnp.broadcast_to(w_ref[0, 0], (tk, p))
    o_ref[0, 0] = pltpu.roll(x, 0, axis=1, stride=1, stride_axis=0)[:, :tq]


def _bias_tiles_t(rel_bias, d0_list, tq, tk):
    p = pl.cdiv(tq + tk, HEAD_W) * HEAD_W
    m = jnp.arange(p, dtype=jnp.int32)
    rel = jnp.asarray(d0_list, jnp.int32)[:, None] + jnp.where(m <= tq, -m, p - m)[None, :]
    onehot = (_t5_bucket(rel)[:, :, None] == jnp.arange(NUM_BUCKETS, dtype=jnp.int32)).astype(F32)
    w = jnp.einsum("dpn,nh->hdp", onehot, rel_bias.astype(F32) * LOG2E,
                   precision=lax.Precision.HIGHEST)
    h, nd = w.shape[0], w.shape[1]
    return pl.pallas_call(
        functools.partial(_toeplitz_kernel, tq=tq),
        out_shape=jax.ShapeDtypeStruct((h, nd, tk, tq), F32),
        grid=(h, nd),
        in_specs=[pl.BlockSpec((1, 1, 1, p), lambda hi, di: (hi, di, 0, 0))],
        out_specs=pl.BlockSpec((1, 1, tk, tq), lambda hi, di: (hi, di, 0, 0)),
        compiler_params=_cparams("parallel", "parallel"),
        name="bias_tiles",
    )(w.reshape(h, nd, 1, p))


def _attn_kernel(*refs, segs, q_off, tq, hb):
    nseg = len(segs)
    q_ref, lam_ref, gn_ref = refs[0], refs[1], refs[2]
    seg_refs = [refs[3 + 3 * s: 6 + 3 * s] for s in range(nseg)]
    o_ref = refs[3 + 3 * nseg]
    m_ref, l_ref, acc_ref = refs[4 + 3 * nseg: 7 + 3 * nseg]
    kv_scratch = refs[7 + 3 * nseg:]
    seg_scr = [kv_scratch[4 * s: 4 * s + 4] for s in range(nseg)]

    i = pl.program_id(2)
    q0 = q_off + i * tq

    @pl.when(i == 0)
    def _():
        for (k_ref, v_ref, _), (kb_ref, vt_ref, _, _), (_, seg_len, tk, _, _) in zip(seg_refs, seg_scr, segs):
            kb_ref[...] = k_ref[0].astype(BF16)
            for kt in range(seg_len // tk):
                vt_ref[kt] = v_ref[0, kt * tk:(kt + 1) * tk, :].astype(F32).T.astype(BF16)

    q = q_ref[0]
    lane = lax.broadcasted_iota(jnp.int32, (tq, HEAD_W), 1)
    qz = []
    for hh in range(hb):
        qh = q[:, hh * HEAD_W:(hh + 1) * HEAD_W]
        qz.append([jnp.where(lane < DA_HEAD_DIM, qh, 0).astype(BF16),
                   jnp.where(lane >= DA_HEAD_DIM, qh, 0).astype(BF16)])
    m_ref[...] = jnp.full_like(m_ref, NEG_BIG)
    l_ref[...] = jnp.zeros_like(l_ref)
    acc_ref[...] = jnp.zeros_like(acc_ref)
    q_chunk = (q0 + lax.broadcasted_iota(jnp.int32, (1, tq), 1)) // CHUNK
    vis_end = ((q0 + tq - 1) // CHUNK + 1) * CHUNK
    full_end = (q0 // CHUNK + 1) * CHUNK
    nt = (((1,), (1,)), ((), ()))

    for (_, _, b_ref), (kb_ref, vt_ref, sa_ref, sb_ref), (pos0, seg_len, tk, d0_min, d0_step) in zip(
            seg_refs, seg_scr, segs):
        n_tiles = seg_len // tk
        n_vis = jnp.clip((vis_end - pos0 + tk - 1) // tk, 0, n_tiles)
        n_full = jnp.clip((full_end - pos0) // tk, 0, n_vis)

        def scores_into(kt, s_ref, c0=0, kb_ref=kb_ref, tk=tk, n_tiles=n_tiles):
            r0 = pl.multiple_of(jnp.minimum(kt, n_tiles - 1) * tk, tk)
            for hh in range(hb):
                kb = kb_ref[pl.ds(r0, tk), hh * HEAD_W:(hh + 1) * HEAD_W]
                for c in range(2):
                    s_ref[2 * hh + c, :, c0:] = lax.dot_general(kb, qz[hh][c][c0:, :], nt,
                                                                preferred_element_type=F32)

        def consume(kt, s_ref, masked, c0=0, b_ref=b_ref, vt_ref=vt_ref, pos0=pos0, tk=tk,
                    d0_min=d0_min, d0_step=d0_step, n_tiles=n_tiles, n_vis=n_vis):
            ktc = jnp.minimum(kt, n_tiles - 1)
            r0 = ktc * tk
            d_idx = (pos0 + r0 - q0 - d0_min) // d0_step
            if masked:
                k0 = jnp.where(kt < n_vis, pos0 + r0, jnp.int32(1 << 28))
                k_chunk = (k0 + lax.broadcasted_iota(jnp.int32, (tk, 1), 0)) // CHUNK
                visible = k_chunk <= q_chunk[:, c0:]
            for hh in range(hb):
                vt = vt_ref[ktc, hh * HEAD_W:(hh + 1) * HEAD_W, :]
                bias = b_ref[hh, d_idx, :, c0:]
                for c in range(2):
                    j = 2 * hh + c
                    s = s_ref[j, :, c0:] + bias
                    if masked:
                        s = jnp.where(visible, s, NEG_BIG)
                    m_prev = m_ref[j, :, c0:]
                    m_new = jnp.maximum(m_prev, jnp.max(s, axis=0, keepdims=True))
                    p = jnp.exp2(s - m_new)
                    alpha = jnp.exp2(m_prev - m_new)
                    l_ref[j, :, c0:] = alpha * l_ref[j, :, c0:] + jnp.sum(p, axis=0, keepdims=True)
                    acc_ref[j, :, c0:] = alpha * acc_ref[j, :, c0:] + jnp.dot(
                        vt, p.astype(BF16), preferred_element_type=F32)
                    m_ref[j, :, c0:] = m_new

        def pair(u, carry, masked, base, sa_ref=sa_ref, sb_ref=sb_ref):
            t0 = base + 2 * u
            scores_into(t0 + 1, sb_ref)
            consume(t0, sa_ref, masked)
            scores_into(t0 + 2, sa_ref)
            consume(t0 + 1, sb_ref, masked)
            return carry

        n_pairs_full = n_full // 2
        base = 2 * n_pairs_full
        scores_into(0, sa_ref)
        if n_tiles == 1:
            consume(0, sa_ref, True)
            continue
        lax.fori_loop(0, n_pairs_full, functools.partial(pair, masked=False, base=0), 0)
        if pos0 == q_off and tq == 2 * tk and tk % CHUNK == 0 and q_off % CHUNK == 0:
            scores_into(base + 1, sb_ref, c0=tk)
            consume(base, sa_ref, True)
            consume(base + 1, sb_ref, True, c0=tk)
        else:
            lax.fori_loop(0, (n_vis - base + 1) // 2, functools.partial(pair, masked=True, base=base), 0)

    lam = lam_ref[0:1, 0:1]
    for hh in range(hb):
        o_t = (acc_ref[2 * hh] * (1.0 / l_ref[2 * hh])
               - lam * (acc_ref[2 * hh + 1] * (1.0 / l_ref[2 * hh + 1])))
        cols = slice(hh * HEAD_W, (hh + 1) * HEAD_W)
        ms = jnp.mean(o_t * o_t, axis=0, keepdims=True)
        o_n = (o_t * lax.rsqrt(ms + EPS)).T
        o_ref[0, :, cols] = (o_n * gn_ref[:, cols] * lam_ref[0:1, 1:2]).astype(o_ref.dtype)


def _attention(q, segments, rel_bias, lam, out_scale, g_subln, q_off, name):
    b, t_real, d = q.shape
    tq_all = max(t_real, HEAD_W)
    if tq_all != t_real:
        q = jnp.pad(q, ((0, 0), (0, tq_all - t_real), (0, 0)))
    tq = _row_tile(tq_all, ATTN_TILE)
    nq = tq_all // tq
    hb = 2 * ATTN_HEADS_PER_STEP if nq == 1 else ATTN_HEADS_PER_STEP
    hw = hb * HEAD_W
    segs, args, in_specs, scratch = [], [], [], []
    for (k, v, pos0, tk) in segments:
        seg_len = k.shape[1]
        assert seg_len % tk == 0
        d0s = sorted({pos0 + kt * tk - (q_off + i * tq)
                      for i in range(nq) for kt in range(seg_len // tk)
                      if pos0 + kt * tk < ((q_off + (i + 1) * tq - 1) // CHUNK + 1) * CHUNK})
        step = math.gcd(tq, tk)
        d0s = list(range(d0s[0], d0s[-1] + 1, step))
        tiles = _bias_tiles_t(rel_bias, d0s, tq, tk)
        segs.append((pos0, seg_len, tk, d0s[0], step))
        args += [k, v, tiles]
        in_specs += [pl.BlockSpec((1, seg_len, hw), lambda bi, h, i: (bi, 0, h)),
                     pl.BlockSpec((1, seg_len, hw), lambda bi, h, i: (bi, 0, h)),
                     pl.BlockSpec((hb, len(d0s), tk, tq), lambda bi, h, i: (h, 0, 0, 0))]
        scratch += [pltpu.VMEM((seg_len, hw), BF16), pltpu.VMEM((seg_len // tk, hw, tk), BF16),
                    pltpu.VMEM((2 * hb, tk, tq), F32), pltpu.VMEM((2 * hb, tk, tq), F32)]
    scal = jnp.zeros((1, HEAD_W), F32).at[0, 0].set(lam).at[0, 1].set(out_scale)
    kern = functools.partial(_attn_kernel, segs=tuple(segs), q_off=q_off, tq=tq, hb=hb)
    out = pl.pallas_call(
        kern,
        out_shape=jax.ShapeDtypeStruct((b, tq_all, d), BF16),
        grid=(b, N_HEADS // hb, nq),
        in_specs=[pl.BlockSpec((1, tq, hw), lambda bi, h, i: (bi, i, h)),
                  pl.BlockSpec((1, HEAD_W), lambda bi, h, i: (0, 0)),
                  pl.BlockSpec((1, hw), lambda bi, h, i: (0, h))] + in_specs,
        out_specs=pl.BlockSpec((1, tq, hw), lambda bi, h, i: (bi, i, h)),
        scratch_shapes=[pltpu.VMEM((2 * hb, 1, tq), F32), pltpu.VMEM((2 * hb, 1, tq), F32),
                        pltpu.VMEM((2 * hb, HEAD_W, tq), F32)] + scratch,
        compiler_params=_cparams("parallel", "parallel", "arbitrary"),
        name=name,
    )(q, scal, g_subln.reshape(1, d), *args)
    return out[:, :t_real]


def _trunk(x, ple, q_off, past_k, past_v, hg_s0, conv_s0, wts, tag):
    (g_norms, w_in_a, lower, g_hg, w_out_a, g_kv, w_kv, rel_bias, w_q_b, lam_b,
     g_subln, w_out_b, w_ffn_in, conv_w, conv_b, w_ffn_out, w_ple, w_ple_gate) = wts
    b, t, d = x.shape
    depth = g_norms.shape[0]
    n_a = w_in_a.shape[0]
    m = b * t
    hg_out, conv_out = [], []
    k_new = v_new = None
    segments = None
    for i in range(depth):
        nm = f"{tag}{i}"
        x2 = x.reshape(m, d)
        if i < n_a:
            lw = lower[i]
            o, s = _gla(x, g_norms[i, 0], w_in_a[i], jnp.log(lw), jnp.log1p(-lw), 1.0 - lw,
                        g_hg[i], hg_s0[i], nm + "_gla")
            hg_out.append(s)
            w_mix = w_out_a[i]
        else:
            j = i - n_a
            q = _norm_matmul(x2, g_norms[i, 0], w_q_b[j], BF16, nm + "_q", scale=ATTN_Q_SCALE)
            lam_init = 0.8 - 0.6 * math.exp(-0.3 * i)
            lp = lam_b[j].astype(F32)
            lam = jnp.exp(jnp.sum(lp[0] * lp[1])) - jnp.exp(jnp.sum(lp[2] * lp[3])) + lam_init
            o = _attention(q.reshape(b, t, d), segments, rel_bias, lam, 1.0 - lam_init,
                           g_subln[j], q_off, nm + "_attn")
            w_mix = w_out_b[j]
        x, cbuf = _ffn(x, o.reshape(b, t, d), w_mix, g_norms[i, 1], g_norms[i, 2], w_ffn_in[i],
                       conv_w[i], conv_b[i], w_ffn_out[i], g_norms[i, 3], conv_s0[i],
                       ple[i], w_ple_gate[i], w_ple[i], nm + "_ffn")
        conv_out.append(cbuf)
        x2 = x.reshape(m, d)
        if i == n_a - 1:
            k_new, v_new = _norm_kv(x2, g_kv, w_kv, nm + "_kv")
            k_new = k_new.reshape(b, t, d)
            v_new = v_new.reshape(b, t, d)
            segments = []
            if past_k is not None:
                tp = past_k.shape[1]
                segments.append((past_k.reshape(b, tp, d), past_v.reshape(b, tp, d), 0, tp))
            segments.append((k_new, v_new, q_off, min(t, ATTN_KEY_TILE)))
    hd = (b, t, N_HEADS, HEAD_W)
    return x, k_new.reshape(hd), v_new.reshape(hd), jnp.stack(hg_out), jnp.stack(conv_out)


def kernel(x_prompt, x_sample, cache_k, cache_v, state_hgrn, state_conv, p_prompt, p_sample,
           g_norms, w_in_a, lb_raw, g_hg, w_out_a, g_kv, w_kv, rel_bias,
           w_q_b, lam_b, g_subln, w_out_b, w_ffn_in, conv_w, conv_b, w_ffn_out,
           w_ple, w_ple_gate):
    sm = jax.nn.softmax(lb_raw.astype(F32), axis=0)
    cs = jnp.cumsum(sm, axis=0)
    lower = cs - cs[0:1]
    bf = lambda w: w.astype(BF16)
    wts = (g_norms, bf(w_in_a), lower, g_hg, bf(w_out_a), g_kv, bf(w_kv), rel_bias,
           bf(w_q_b), lam_b, g_subln, bf(w_out_b), bf(w_ffn_in), conv_w, conv_b,
           bf(w_ffn_out), bf(w_ple), bf(w_ple_gate))
    bp, tp, _ = x_prompt.shape
    n_a, depth = w_in_a.shape[0], g_norms.shape[0]
    hg0 = jnp.zeros((n_a, bp) + state_hgrn.shape[2:], F32)
    cv0 = jnp.zeros((depth, bp) + state_conv.shape[2:], F32)
    y_p, k_p, v_p, hg_p, cv_p = _trunk(x_prompt, p_prompt, 0, None, None, hg0, cv0, wts, "p")
    y_s, k_s, v_s, hg_s, cv_s = _trunk(x_sample, p_sample, cache_k.shape[1], cache_k, cache_v,
                                       state_hgrn, state_conv, wts, "s")
    return (y_p, y_s, k_p, v_p, k_s, v_s, hg_p, hg_s, cv_p, cv_s)
```

```python
import functools
import math

import numpy as np
import jax
import jax.numpy as jnp
from jax import lax
from jax.experimental import pallas as pl
from jax.experimental.pallas import tpu as pltpu

F32 = jnp.float32
BF16 = jnp.bfloat16

D_MODEL = 1024
N_HEADS = 8
HEAD_W = D_MODEL // N_HEADS
DA_HEAD_DIM = HEAD_W // 2
CHUNK = 64
GLA_BLOCK = 16
NUM_BUCKETS = 32
MAX_DISTANCE = 256
D_FF = 2816
CONV_W = 3
EPS = 1e-6
NEG_BIG = -1e30
LOG2E = math.log2(math.e)
ATTN_Q_SCALE = DA_HEAD_DIM ** -0.5 * LOG2E

V7X_VMEM_LIMIT_BYTES = 56 * 1024 * 1024

ROW_TILE = 512
FFN_ROW_TILE = 1024
FFN_COL_TILE = 256
FFN_CONV_ROWS = 128
FFN_EDGE_ROWS = 256
GLA_TILE = 256
GLA_TILES_PER_STEP = 4
ATTN_TILE = 512
ATTN_KEY_TILE = 256
ATTN_HEADS_PER_STEP = 2


def _cparams(*sem):
    return pltpu.CompilerParams(dimension_semantics=sem,
                                vmem_limit_bytes=V7X_VMEM_LIMIT_BYTES)


def _rms(x, g):
    ms = jnp.mean(x * x, axis=-1, keepdims=True)
    return x * lax.rsqrt(ms + EPS) * g


def _sigmoid(z):
    return 1.0 / (1.0 + jnp.exp(-z))


def _row_tile(m, cap):
    t = min(m, cap)
    assert m % t == 0, (m, t)
    return t


def _norm_matmul_kernel(x_ref, g_ref, w_ref, o_ref, *, scale):
    h = _rms(x_ref[...], g_ref[...]).astype(BF16)
    n = w_ref.shape[1]
    tn = min(n, D_MODEL)
    for n0 in range(0, n, tn):
        y = jnp.dot(h, w_ref[:, n0:n0 + tn], preferred_element_type=F32)
        if scale != 1.0:
            y = y * scale
        o_ref[:, n0:n0 + tn] = y.astype(o_ref.dtype)


def _norm_matmul(x2d, g, w, out_dtype, name, scale=1.0):
    m, d = x2d.shape
    n = w.shape[1]
    tm = _row_tile(m, ROW_TILE)
    return pl.pallas_call(
        functools.partial(_norm_matmul_kernel, scale=scale),
        out_shape=jax.ShapeDtypeStruct((m, n), out_dtype),
        grid=(m // tm,),
        in_specs=[pl.BlockSpec((tm, d), lambda i: (i, 0)),
                  pl.BlockSpec((1, d), lambda i: (0, 0)),
                  pl.BlockSpec((d, n), lambda i: (0, 0))],
        out_specs=pl.BlockSpec((tm, n), lambda i: (i, 0)),
        compiler_params=_cparams("parallel"),
        name=name,
    )(x2d, g.reshape(1, d), w)


def _norm_kv_kernel(x_ref, g_ref, w_ref, k_ref, v_ref):
    d = k_ref.shape[1]
    h = _rms(x_ref[...], g_ref[...]).astype(BF16)
    k_ref[...] = jnp.dot(h, w_ref[:, :d], preferred_element_type=F32)
    v_ref[...] = jnp.dot(h, w_ref[:, d:], preferred_element_type=F32)


def _norm_kv(x2d, g, w, name):
    m, d = x2d.shape
    tm = _row_tile(m, ROW_TILE)
    out = jax.ShapeDtypeStruct((m, d), F32)
    return pl.pallas_call(
        _norm_kv_kernel,
        out_shape=(out, out),
        grid=(m // tm,),
        in_specs=[pl.BlockSpec((tm, d), lambda i: (i, 0)),
                  pl.BlockSpec((1, d), lambda i: (0, 0)),
                  pl.BlockSpec((d, 2 * d), lambda i: (0, 0))],
        out_specs=(pl.BlockSpec((tm, d), lambda i: (i, 0)),
                   pl.BlockSpec((tm, d), lambda i: (i, 0))),
        compiler_params=_cparams("parallel"),
        name=name,
    )(x2d, g.reshape(1, d), w)


def _ffn_kernel(x_ref, a_ref, wm_ref, gm_ref, gin_ref, wg_ref, wu_ref, cw_ref, cb_ref, wo_ref, gout_ref,
                cs_ref, p_ref, wgate_ref, wple_ref,
                o_ref, co_ref, xm_ref, h_ref, acc_ref, gu_ref, act_ref):
    t = pl.program_id(1)
    j = pl.program_id(2)
    nb, tm, d = x_ref.shape
    rows = nb * tm
    rb = min(rows, FFN_EDGE_ROWS)

    def chunk(ref, r0):
        return ref[0, r0:r0 + rb, :] if nb == 1 else ref[...].reshape(rows, ref.shape[2])

    @pl.when(j == 0)
    def _():
        for r0 in range(0, rows, rb):
            m = jnp.dot(chunk(a_ref, r0), wm_ref[...], preferred_element_type=F32)
            xm = chunk(x_ref, r0) + _rms(m, gm_ref[...])
            xm_ref[r0:r0 + rb, :] = xm
            h_ref[r0:r0 + rb, :] = _rms(xm, gin_ref[...]).astype(BF16)
        acc_ref[...] = jnp.zeros_like(acc_ref)

    @pl.when((j == 0) & (t == 0))
    def _():
        co_ref[...] = cs_ref[...]

    cw = cw_ref[...]
    cb = cb_ref[...]
    tf = cw.shape[1]
    gu_ref[:, :tf] = jnp.dot(h_ref[...], wg_ref[...], preferred_element_type=F32)
    gu_ref[:, tf:] = jnp.dot(h_ref[...], wu_ref[...], preferred_element_type=F32)
    r8 = lax.broadcasted_iota(jnp.int32, (8, tf), 0)
    rc = min(tm, FFN_CONV_ROWS)
    for bi in range(nb):
        base = bi * tm
        prev = co_ref[bi, j]
        co_ref[bi, j] = gu_ref[base + tm - (CONV_W - 1):base + tm, :tf]
        head = jnp.where(r8 == 6, prev[0:1, :], prev[1:2, :])
        for r0 in range(base, base + tm, rc):
            if r0 > base:
                head = gu_ref[r0 - 8:r0, :tf]
            gc = gu_ref[r0:r0 + rc, :tf]
            ext = jnp.concatenate([head, gc], axis=0)
            g1 = pltpu.roll(ext, 1, axis=0)[8:, :]
            g2 = pltpu.roll(ext, 2, axis=0)[8:, :]
            c = cb + cw[0:1, :] * g2 + cw[1:2, :] * g1 + cw[2:3, :] * gc
            act_ref[r0:r0 + rc, :] = (c * _sigmoid(c) * gu_ref[r0:r0 + rc, tf:]).astype(BF16)
    acc_ref[...] += jnp.dot(act_ref[...], wo_ref[...], preferred_element_type=F32)

    @pl.when(j == pl.num_programs(2) - 1)
    def _():
        for r0 in range(0, rows, rb):
            y = xm_ref[r0:r0 + rb, :] + _rms(acc_ref[r0:r0 + rb, :], gout_ref[...])
            z = jnp.dot(y.astype(BF16), wgate_ref[...], preferred_element_type=F32)
            e = jnp.dot(chunk(p_ref, r0).astype(BF16), wple_ref[...], preferred_element_type=F32)
            out = y + _sigmoid(z) * e
            if nb == 1:
                o_ref[0, r0:r0 + rb, :] = out
            else:
                o_ref[...] = out.reshape(nb, tm, d)


def _ffn(x, a, w_mix, g_mix, gin, w_in, cw, cb, w_out, gout, conv_state, p, w_gate, w_ple, name):
    b, t, d = x.shape
    f = w_out.shape[0]
    pd = p.shape[2]
    tm = _row_tile(t, FFN_ROW_TILE)
    nb = max(1, min(b, FFN_ROW_TILE // t))
    assert b % nb == 0
    rows = nb * tm
    assert rows % min(rows, FFN_EDGE_ROWS) == 0 and (nb == 1 or rows <= FFN_EDGE_ROWS)
    tf = FFN_COL_TILE
    nf = f // tf
    assert f % tf == 0 and t >= CONV_W - 1 and tm % 8 == 0
    cs = conv_state.reshape(b, CONV_W - 1, nf, tf).transpose(0, 2, 1, 3)
    cs_spec = pl.BlockSpec((nb, nf, CONV_W - 1, tf), lambda bi, ti, j: (bi, 0, 0, 0))
    row_spec = lambda width: pl.BlockSpec((nb, tm, width), lambda bi, ti, j: (bi, ti, 0))
    vec_spec = pl.BlockSpec((1, d), lambda bi, ti, j: (0, 0))
    const_spec = lambda shape: pl.BlockSpec(shape, lambda bi, ti, j: (0, 0),
                                            pipeline_mode=pl.Buffered(1))
    y, co = pl.pallas_call(
        _ffn_kernel,
        out_shape=(jax.ShapeDtypeStruct((b, t, d), F32),
                   jax.ShapeDtypeStruct(cs.shape, F32)),
        grid=(b // nb, t // tm, nf),
        in_specs=[row_spec(d), row_spec(d), const_spec((d, d)), vec_spec,
                  vec_spec,
                  pl.BlockSpec((d, tf), lambda bi, ti, j: (0, j)),
                  pl.BlockSpec((d, tf), lambda bi, ti, j: (0, j + nf)),
                  pl.BlockSpec((CONV_W, tf), lambda bi, ti, j: (0, j)),
                  pl.BlockSpec((1, tf), lambda bi, ti, j: (0, j)),
                  pl.BlockSpec((tf, d), lambda bi, ti, j: (j, 0)),
                  vec_spec,
                  cs_spec, row_spec(pd), const_spec((d, d)), const_spec((pd, d))],
        out_specs=(row_spec(d), cs_spec),
        scratch_shapes=[pltpu.VMEM((rows, d), F32),
                        pltpu.VMEM((rows, d), BF16),
                        pltpu.VMEM((rows, d), F32),
                        pltpu.VMEM((rows, 2 * tf), F32),
                        pltpu.VMEM((rows, tf), BF16)],
        compiler_params=_cparams("parallel", "arbitrary", "arbitrary"),
        name=name,
    )(x, a, w_mix, g_mix.reshape(1, d), gin.reshape(1, d), w_in, w_in, cw, cb.reshape(1, f), w_out,
      gout.reshape(1, d), cs, p, w_gate, w_ple)
    return y, co.transpose(0, 2, 1, 3).reshape(b, CONV_W - 1, f)


def _split3(x):
    hi = x.astype(BF16)
    r1 = x - hi.astype(F32)
    mid = r1.astype(BF16)
    lo = (r1 - mid.astype(F32)).astype(BF16)
    return hi, mid, lo


def _gla_levels(tc):
    return [s for s in (16, 32, 64, 128, 256, 512, 1024) if 2 * s <= tc]


def _gla_level_table(tc):
    r = np.arange(tc)[:, None]
    c = np.arange(tc)[None, :]
    table = np.full((tc, tc), -1, np.int32)
    table[(r // GLA_BLOCK == c // GLA_BLOCK) & (c <= r)] = 0
    for i, s in enumerate(_gla_levels(tc)):
        table[(r // (2 * s) == c // (2 * s)) & ((r // s) % 2 == 1) & ((c // s) % 2 == 0)] = i + 1
    return table


def _gla_kernel(x_ref, gx_ref, w_ref, ll_ref, l1m_ref, om_ref, gn_ref, s0_ref, lvl_ref,
                o_ref, so_ref, st_ref, proj_ref, bg_ref, fac_ref, qi_ref, ks_ref, v_ref, a_ref, oo_ref,
                *, tc):
    t = pl.program_id(1)
    d = x_ref.shape[2]
    n_sub = x_ref.shape[1] // tc

    def project(x_tile, slot):
        hx = _rms(x_tile, gx_ref[...]).astype(BF16)
        for n0 in range(0, 4 * d, d):
            proj_ref[slot, :, n0:n0 + d] = jnp.dot(hx, w_ref[:, n0:n0 + d], preferred_element_type=F32)

    @pl.when(t == 0)
    def _():
        for h in range(N_HEADS):
            st_ref[h] = s0_ref[0, h].T

    project(x_ref[0, 0:tc, :], 0)
    for s in range(n_sub):
        if s + 1 < n_sub:
            project(x_ref[0, (s + 1) * tc:(s + 2) * tc, :], (s + 1) % 2)
        _gla_tile(proj_ref.at[s % 2], ll_ref, l1m_ref, om_ref, gn_ref, lvl_ref, o_ref.at[0, s * tc:(s + 1) * tc, :],
                  st_ref, bg_ref, fac_ref, qi_ref, ks_ref, v_ref, a_ref, oo_ref)

    @pl.when(t == pl.num_programs(1) - 1)
    def _():
        for h in range(N_HEADS):
            so_ref[0, h] = st_ref[h].T


def _gla_tile(p_ref, ll_ref, l1m_ref, om_ref, gn_ref, lvl_ref, o_ref,
              st_ref, bg_ref, fac_ref, qi_ref, ks_ref, v_ref, a_ref, oo_ref):
    tc = p_ref.shape[0]
    d = p_ref.shape[1] // 4
    levels = _gla_levels(tc)

    qh = p_ref[:, 0:d]
    fp = p_ref[:, d:2 * d]
    q = qh * _sigmoid(qh)
    e = jnp.exp(-jnp.abs(fp))
    logsig = jnp.minimum(fp, 0.0) - jnp.log(1.0 + e)
    a = ll_ref[...]
    c = l1m_ref[...] + logsig
    logf = jnp.maximum(a, c) + jnp.log(1.0 + jnp.exp(-jnp.abs(a - c)))
    kk = om_ref[...] * (jnp.where(fp >= 0.0, e, 1.0) / (1.0 + e))

    row = lax.broadcasted_iota(jnp.int32, (tc, tc), 0)
    col = lax.broadcasted_iota(jnp.int32, (tc, tc), 1)
    m_incl = jnp.where(col <= row, 1.0, 0.0).astype(BF16)
    bg = sum(jnp.dot(m_incl, p, preferred_element_type=F32) for p in _split3(logf * LOG2E))
    bg_ref[...] = bg
    b_last = bg[tc - 1:tc, :]

    qi_ref[...] = (q * jnp.exp2(bg)).astype(BF16)
    ks_ref[...] = (kk * jnp.exp2(b_last - bg)).astype(BF16)
    v_ref[...] = p_ref[:, 2 * d:3 * d].astype(BF16)

    def ref_rows(span, offset):
        pieces = []
        for a0 in range(0, tc, span):
            r = a0 + offset
            src = bg_ref[r:r + 1, :] if r >= 0 else jnp.zeros((1, d), F32)
            pieces.append(jnp.broadcast_to(src, (span, d)))
        return pieces[0] if len(pieces) == 1 else jnp.concatenate(pieces, axis=0)

    dl = bg - ref_rows(GLA_BLOCK, -1)
    fac_ref[0] = (q * jnp.exp2(dl)).astype(BF16)
    fac_ref[1] = (kk * jnp.exp2(-dl)).astype(BF16)
    rid = lax.broadcasted_iota(jnp.int32, (tc, 1), 0)
    for li, s in enumerate(levels):
        mid = ref_rows(2 * s, s - 1)
        right = ((rid // s) % 2) == 1
        x = jnp.exp2(jnp.where(right, bg - mid, mid - bg))
        fac_ref[2 + 2 * li] = (q * x).astype(BF16)
        fac_ref[3 + 2 * li] = (kk * x).astype(BF16)

    nt = (((1,), (1,)), ((), ()))
    tn = (((0,), (0,)), ((), ()))
    lvl = lvl_ref[...]
    for h in range(N_HEADS):
        cols = slice(h * HEAD_W, (h + 1) * HEAD_W)
        att = jnp.zeros((tc, tc), F32)
        for ci in range(len(levels), -1, -1):
            prod = lax.dot_general(fac_ref[2 * ci, :, cols], fac_ref[2 * ci + 1, :, cols], nt,
                                   preferred_element_type=F32)
            att = jnp.where(lvl == ci, prod, att)
        a_ref[h] = att.astype(BF16)

    for h in range(N_HEADS):
        cols = slice(h * HEAD_W, (h + 1) * HEAD_W)
        vb = v_ref[:, cols]
        s_t = st_ref[h]
        o_inter = lax.dot_general(qi_ref[:, cols], s_t.astype(BF16), nt, preferred_element_type=F32)
        oo_ref[:, cols] = o_inter + jnp.dot(a_ref[h], vb, preferred_element_type=F32)
        upd = lax.dot_general(vb, ks_ref[:, cols], tn, preferred_element_type=F32)
        st_ref[h] = s_t * jnp.exp2(b_last[:, cols]) + upd

    gh = p_ref[:, 3 * d:4 * d]
    gate = gh * _sigmoid(gh)
    for h in range(N_HEADS):
        cols = slice(h * HEAD_W, (h + 1) * HEAD_W)
        oh = _rms(oo_ref[:, cols], gn_ref[:, cols])
        o_ref[:, cols] = (oh * gate[:, cols]).astype(o_ref.dtype)


def _gla(x, g_in, w_in, log_lower, log1m_lower, one_m_lower, g_out, s0, name):
    b, t, d = x.shape
    tc = _row_tile(t, GLA_TILE)
    assert tc % GLA_BLOCK == 0
    n_sub = GLA_TILES_PER_STEP if t % (GLA_TILES_PER_STEP * tc) == 0 else 1
    rows = n_sub * tc
    vec = pl.BlockSpec((1, d), lambda bi, ti: (0, 0))
    st_spec = pl.BlockSpec((1, N_HEADS, HEAD_W, HEAD_W), lambda bi, ti: (bi, 0, 0, 0))
    n_fac = 2 * (1 + len(_gla_levels(tc)))
    return pl.pallas_call(
        functools.partial(_gla_kernel, tc=tc),
        out_shape=(jax.ShapeDtypeStruct((b, t, d), BF16),
                   jax.ShapeDtypeStruct((b, N_HEADS, HEAD_W, HEAD_W), F32)),
        grid=(b, t // rows),
        in_specs=[pl.BlockSpec((1, rows, d), lambda bi, ti: (bi, ti, 0)),
                  vec,
                  pl.BlockSpec((d, 4 * d), lambda bi, ti: (0, 0), pipeline_mode=pl.Buffered(1)),
                  vec, vec, vec, vec, st_spec,
                  pl.BlockSpec((tc, tc), lambda bi, ti: (0, 0))],
        out_specs=(pl.BlockSpec((1, rows, d), lambda bi, ti: (bi, ti, 0)), st_spec),
        scratch_shapes=[pltpu.VMEM((N_HEADS, HEAD_W, HEAD_W), F32),
                        pltpu.VMEM((2, tc, 4 * d), F32),
                        pltpu.VMEM((tc, d), F32),
                        pltpu.VMEM((n_fac, tc, d), BF16),
                        pltpu.VMEM((tc, d), BF16), pltpu.VMEM((tc, d), BF16),
                        pltpu.VMEM((tc, d), BF16),
                        pltpu.VMEM((N_HEADS, tc, tc), BF16),
                        pltpu.VMEM((tc, d), F32)],
        compiler_params=_cparams("parallel", "arbitrary"),
        name=name,
    )(x, g_in.reshape(1, d), w_in, log_lower.reshape(1, d), log1m_lower.reshape(1, d),
      one_m_lower.reshape(1, d), g_out.reshape(1, d), s0, jnp.asarray(_gla_level_table(tc)))


def _t5_bucket(rel):
    half = NUM_BUCKETS // 2
    ret = jnp.where(rel > 0, half, 0)
    n = jnp.abs(rel)
    max_exact = half // 2
    nf = jnp.maximum(n, 1).astype(F32)
    large = max_exact + (jnp.log(nf / max_exact) / math.log(MAX_DISTANCE / max_exact)
                         * (half - max_exact)).astype(jnp.int32)
    large = jnp.minimum(large, half - 1)
    return ret + jnp.where(n < max_exact, n, large)


def _toeplitz_kernel(w_ref, o_ref, *, tq):
    tk = o_ref.shape[2]
    p = w_ref.shape[3]
    x = jnp.broadcast_to(w_ref[0, 0], (tk, p))
    o_ref[0, 0] = pltpu.roll(x, 0, axis=1, stride=1, stride_axis=0)[:, :tq]


def _bias_tiles_t(rel_bias, d0_list, tq, tk):
    p = pl.cdiv(tq + tk, HEAD_W) * HEAD_W
    m = jnp.arange(p, dtype=jnp.int32)
    rel = jnp.asarray(d0_list, jnp.int32)[:, None] + jnp.where(m <= tq, -m, p - m)[None, :]
    onehot = (_t5_bucket(rel)[:, :, None] == jnp.arange(NUM_BUCKETS, dtype=jnp.int32)).astype(F32)
    w = jnp.einsum("dpn,nh->hdp", onehot, rel_bias.astype(F32) * LOG2E,
                   precision=lax.Precision.HIGHEST)
    h, nd = w.shape[0], w.shape[1]
    return pl.pallas_call(
        functools.partial(_toeplitz_kernel, tq=tq),
        out_shape=jax.ShapeDtypeStruct((h, nd, tk, tq), F32),
        grid=(h, nd),
        in_specs=[pl.BlockSpec((1, 1, 1, p), lambda hi, di: (hi, di, 0, 0))],
        out_specs=pl.BlockSpec((1, 1, tk, tq), lambda hi, di: (hi, di, 0, 0)),
        compiler_params=_cparams("parallel", "parallel"),
        name="bias_tiles",
    )(w.reshape(h, nd, 1, p))


def _attn_kernel(*refs, segs, q_off, tq, hb):
    nseg = len(segs)
    q_ref, lam_ref, gn_ref = refs[0], refs[1], refs[2]
    seg_refs = [refs[3 + 3 * s: 6 + 3 * s] for s in range(nseg)]
    o_ref = refs[3 + 3 * nseg]
    m_ref, l_ref, acc_ref = refs[4 + 3 * nseg: 7 + 3 * nseg]
    kv_scratch = refs[7 + 3 * nseg:]
    seg_scr = [kv_scratch[4 * s: 4 * s + 4] for s in range(nseg)]

    i = pl.program_id(2)
    q0 = q_off + i * tq

    @pl.when(i == 0)
    def _():
        for (k_ref, v_ref, _), (kb_ref, vt_ref, _, _), (_, seg_len, tk, _, _) in zip(seg_refs, seg_scr, segs):
            kb_ref[...] = k_ref[0].astype(BF16)
            for kt in range(seg_len // tk):
                vt_ref[kt] = v_ref[0, kt * tk:(kt + 1) * tk, :].astype(F32).T.astype(BF16)

    q = q_ref[0]
    lane = lax.broadcasted_iota(jnp.int32, (tq, HEAD_W), 1)
    qz = []
    for hh in range(hb):
        qh = q[:, hh * HEAD_W:(hh + 1) * HEAD_W]
        qz.append([jnp.where(lane < DA_HEAD_DIM, qh, 0).astype(BF16),
                   jnp.where(lane >= DA_HEAD_DIM, qh, 0).astype(BF16)])
    m_ref[...] = jnp.full_like(m_ref, NEG_BIG)
    l_ref[...] = jnp.zeros_like(l_ref)
    acc_ref[...] = jnp.zeros_like(acc_ref)
    q_chunk = (q0 + lax.broadcasted_iota(jnp.int32, (1, tq), 1)) // CHUNK
    vis_end = ((q0 + tq - 1) // CHUNK + 1) * CHUNK
    full_end = (q0 // CHUNK + 1) * CHUNK
    nt = (((1,), (1,)), ((), ()))

    for (_, _, b_ref), (kb_ref, vt_ref, sa_ref, sb_ref), (pos0, seg_len, tk, d0_min, d0_step) in zip(
            seg_refs, seg_scr, segs):
        n_tiles = seg_len // tk
        n_vis = jnp.clip((vis_end - pos0 + tk - 1) // tk, 0, n_tiles)
        n_full = jnp.clip((full_end - pos0) // tk, 0, n_vis)

        def scores_into(kt, s_ref, c0=0, kb_ref=kb_ref, tk=tk, n_tiles=n_tiles):
            r0 = pl.multiple_of(jnp.minimum(kt, n_tiles - 1) * tk, tk)
            for hh in range(hb):
                kb = kb_ref[pl.ds(r0, tk), hh * HEAD_W:(hh + 1) * HEAD_W]
                for c in range(2):
                    s_ref[2 * hh + c, :, c0:] = lax.dot_general(kb, qz[hh][c][c0:, :], nt,
                                                                preferred_element_type=F32)

        def consume(kt, s_ref, masked, c0=0, b_ref=b_ref, vt_ref=vt_ref, pos0=pos0, tk=tk,
                    d0_min=d0_min, d0_step=d0_step, n_tiles=n_tiles, n_vis=n_vis):
            ktc = jnp.minimum(kt, n_tiles - 1)
            r0 = ktc * tk
            d_idx = (pos0 + r0 - q0 - d0_min) // d0_step
            if masked:
                k0 = jnp.where(kt < n_vis, pos0 + r0, jnp.int32(1 << 28))
                k_chunk = (k0 + lax.broadcasted_iota(jnp.int32, (tk, 1), 0)) // CHUNK
                visible = k_chunk <= q_chunk[:, c0:]
            for hh in range(hb):
                vt = vt_ref[ktc, hh * HEAD_W:(hh + 1) * HEAD_W, :]
                bias = b_ref[hh, d_idx, :, c0:]
                for c in range(2):
                    j = 2 * hh + c
                    s = s_ref[j, :, c0:] + bias
                    if masked:
                        s = jnp.where(visible, s, NEG_BIG)
                    m_prev = m_ref[j, :, c0:]
                    m_new = jnp.maximum(m_prev, jnp.max(s, axis=0, keepdims=True))
                    p = jnp.exp2(s - m_new)
                    alpha = jnp.exp2(m_prev - m_new)
                    l_ref[j, :, c0:] = alpha * l_ref[j, :, c0:] + jnp.sum(p, axis=0, keepdims=True)
                    acc_ref[j, :, c0:] = alpha * acc_ref[j, :, c0:] + jnp.dot(
                        vt, p.astype(BF16), preferred_element_type=F32)
                    m_ref[j, :, c0:] = m_new

        def pair(u, carry, masked, base, sa_ref=sa_ref, sb_ref=sb_ref):
            t0 = base + 2 * u
            scores_into(t0 + 1, sb_ref)
            consume(t0, sa_ref, masked)
            scores_into(t0 + 2, sa_ref)
            consume(t0 + 1, sb_ref, masked)
            return carry

        n_pairs_full = n_full // 2
        base = 2 * n_pairs_full
        scores_into(0, sa_ref)
        if n_tiles == 1:
            consume(0, sa_ref, True)
            continue
        lax.fori_loop(0, n_pairs_full, functools.partial(pair, masked=False, base=0), 0)
        if pos0 == q_off and tq == 2 * tk and tk % CHUNK == 0 and q_off % CHUNK == 0:
            scores_into(base + 1, sb_ref, c0=tk)
            consume(base, sa_ref, True)
            consume(base + 1, sb_ref, True, c0=tk)
        else:
            lax.fori_loop(0, (n_vis - base + 1) // 2, functools.partial(pair, masked=True, base=base), 0)

    lam = lam_ref[0:1, 0:1]
    for hh in range(hb):
        o_t = (acc_ref[2 * hh] * (1.0 / l_ref[2 * hh])
               - lam * (acc_ref[2 * hh + 1] * (1.0 / l_ref[2 * hh + 1])))
        cols = slice(hh * HEAD_W, (hh + 1) * HEAD_W)
        ms = jnp.mean(o_t * o_t, axis=0, keepdims=True)
        o_n = (o_t * lax.rsqrt(ms + EPS)).T
        o_ref[0, :, cols] = (o_n * gn_ref[:, cols] * lam_ref[0:1, 1:2]).astype(o_ref.dtype)


def _attention(q, segments, rel_bias, lam, out_scale, g_subln, q_off, name):
    b, t_real, d = q.shape
    tq_all = max(t_real, HEAD_W)
    if tq_all != t_real:
        q = jnp.pad(q, ((0, 0), (0, tq_all - t_real), (0, 0)))
    tq = _row_tile(tq_all, ATTN_TILE)
    nq = tq_all // tq
    hb = 2 * ATTN_HEADS_PER_STEP if nq == 1 else ATTN_HEADS_PER_STEP
    hw = hb * HEAD_W
    segs, args, in_specs, scratch = [], [], [], []
    for (k, v, pos0, tk) in segments:
        seg_len = k.shape[1]
        assert seg_len % tk == 0
        d0s = sorted({pos0 + kt * tk - (q_off + i * tq)
                      for i in range(nq) for kt in range(seg_len // tk)
                      if pos0 + kt * tk < ((q_off + (i + 1) * tq - 1) // CHUNK + 1) * CHUNK})
        step = math.gcd(tq, tk)
        d0s = list(range(d0s[0], d0s[-1] + 1, step))
        tiles = _bias_tiles_t(rel_bias, d0s, tq, tk)
        segs.append((pos0, seg_len, tk, d0s[0], step))
        args += [k, v, tiles]
        in_specs += [pl.BlockSpec((1, seg_len, hw), lambda bi, h, i: (bi, 0, h)),
                     pl.BlockSpec((1, seg_len, hw), lambda bi, h, i: (bi, 0, h)),
                     pl.BlockSpec((hb, len(d0s), tk, tq), lambda bi, h, i: (h, 0, 0, 0))]
        scratch += [pltpu.VMEM((seg_len, hw), BF16), pltpu.VMEM((seg_len // tk, hw, tk), BF16),
                    pltpu.VMEM((2 * hb, tk, tq), F32), pltpu.VMEM((2 * hb, tk, tq), F32)]
    scal = jnp.zeros((1, HEAD_W), F32).at[0, 0].set(lam).at[0, 1].set(out_scale)
    kern = functools.partial(_attn_kernel, segs=tuple(segs), q_off=q_off, tq=tq, hb=hb)
    out = pl.pallas_call(
        kern,
        out_shape=jax.ShapeDtypeStruct((b, tq_all, d), BF16),
        grid=(b, N_HEADS // hb, nq),
        in_specs=[pl.BlockSpec((1, tq, hw), lambda bi, h, i: (bi, i, h)),
                  pl.BlockSpec((1, HEAD_W), lambda bi, h, i: (0, 0)),
                  pl.BlockSpec((1, hw), lambda bi, h, i: (0, h))] + in_specs,
        out_specs=pl.BlockSpec((1, tq, hw), lambda bi, h, i: (bi, i, h)),
        scratch_shapes=[pltpu.VMEM((2 * hb, 1, tq), F32), pltpu.VMEM((2 * hb, 1, tq), F32),
                        pltpu.VMEM((2 * hb, HEAD_W, tq), F32)] + scratch,
        compiler_params=_cparams("parallel", "parallel", "arbitrary"),
        name=name,
    )(q, scal, g_subln.reshape(1, d), *args)
    return out[:, :t_real]


def _trunk(x, ple, q_off, past_k, past_v, hg_s0, conv_s0, wts, tag):
    (g_norms, w_in_a, lower, g_hg, w_out_a, g_kv, w_kv, rel_bias, w_q_b, lam_b,
     g_subln, w_out_b, w_ffn_in, conv_w, conv_b, w_ffn_out, w_ple, w_ple_gate) = wts
    b, t, d = x.shape
    depth = g_norms.shape[0]
    n_a = w_in_a.shape[0]
    m = b * t
    hg_out, conv_out = [], []
    k_new = v_new = None
    segments = None
    for i in range(depth):
        nm = f"{tag}{i}"
        x2 = x.reshape(m, d)
        if i < n_a:
            lw = lower[i]
            o, s = _gla(x, g_norms[i, 0], w_in_a[i], jnp.log(lw), jnp.log1p(-lw), 1.0 - lw,
                        g_hg[i], hg_s0[i], nm + "_gla")
            hg_out.append(s)
            w_mix = w_out_a[i]
        else:
            j = i - n_a
            q = _norm_matmul(x2, g_norms[i, 0], w_q_b[j], BF16, nm + "_q", scale=ATTN_Q_SCALE)
            lam_init = 0.8 - 0.6 * math.exp(-0.3 * i)
            lp = lam_b[j].astype(F32)
            lam = jnp.exp(jnp.sum(lp[0] * lp[1])) - jnp.exp(jnp.sum(lp[2] * lp[3])) + lam_init
            o = _attention(q.reshape(b, t, d), segments, rel_bias, lam, 1.0 - lam_init,
                           g_subln[j], q_off, nm + "_attn")
            w_mix = w_out_b[j]
        x, cbuf = _ffn(x, o.reshape(b, t, d), w_mix, g_norms[i, 1], g_norms[i, 2], w_ffn_in[i],
                       conv_w[i], conv_b[i], w_ffn_out[i], g_norms[i, 3], conv_s0[i],
                       ple[i], w_ple_gate[i], w_ple[i], nm + "_ffn")
        conv_out.append(cbuf)
        x2 = x.reshape(m, d)
        if i == n_a - 1:
            k_new, v_new = _norm_kv(x2, g_kv, w_kv, nm + "_kv")
            k_new = k_new.reshape(b, t, d)
            v_new = v_new.reshape(b, t, d)
            segments = []
            if past_k is not None:
                tp = past_k.shape[1]
                segments.append((past_k.reshape(b, tp, d), past_v.reshape(b, tp, d), 0, tp))
            segments.append((k_new, v_new, q_off, min(t, ATTN_KEY_TILE)))
    hd = (b, t, N_HEADS, HEAD_W)
    return x, k_new.reshape(hd), v_new.reshape(hd), jnp.stack(hg_out), jnp.stack(conv_out)


def kernel(x_prompt, x_sample, cache_k, cache_v, state_hgrn, state_conv, p_prompt, p_sample,
           g_norms, w_in_a, lb_raw, g_hg, w_out_a, g_kv, w_kv, rel_bias,
           w_q_b, lam_b, g_subln, w_out_b, w_ffn_in, conv_w, conv_b, w_ffn_out,
           w_ple, w_ple_gate):
    sm = jax.nn.softmax(lb_raw.astype(F32), axis=0)
    cs = jnp.cumsum(sm, axis=0)
    lower = cs - cs[0:1]
    bf = lambda w: w.astype(BF16)
    wts = (g_norms, bf(w_in_a), lower, g_hg, bf(w_out_a), g_kv, bf(w_kv), rel_bias,
           bf(w_q_b), lam_b, g_subln, bf(w_out_b), bf(w_ffn_in), conv_w, conv_b,
           bf(w_ffn_out), bf(w_ple), bf(w_ple_gate))
    bp, tp, _ = x_prompt.shape
    n_a, depth = w_in_a.shape[0], g_norms.shape[0]
    hg0 = jnp.zeros((n_a, bp) + state_hgrn.shape[2:], F32)
    cv0 = jnp.zeros((depth, bp) + state_conv.shape[2:], F32)
    y_p, k_p, v_p, hg_p, cv_p = _trunk(x_prompt, p_prompt, 0, None, None, hg0, cv0, wts, "p")
    y_s, k_s, v_s, hg_s, cv_s = _trunk(x_sample, p_sample, cache_k.shape[1], cache_k, cache_v,
                                       state_hgrn, state_conv, wts, "s")
    return (y_p, y_s, k_p, v_p, k_s, v_s, hg_p, hg_s, cv_p, cv_s)
```

```python
import functools
import math

import numpy as np
import jax
import jax.numpy as jnp
from jax import lax
from jax.experimental import pallas as pl
from jax.experimental.pallas import tpu as pltpu

F32 = jnp.float32
BF16 = jnp.bfloat16

D_MODEL = 1024
N_HEADS = 8
HEAD_W = D_MODEL // N_HEADS
DA_HEAD_DIM = HEAD_W // 2
CHUNK = 64
GLA_BLOCK = 16
NUM_BUCKETS = 32
MAX_DISTANCE = 256
CONV_W = 3
EPS = 1e-6
NEG_BIG = -1e30
LOG2E = math.log2(math.e)
ATTN_Q_SCALE = DA_HEAD_DIM ** -0.5 * LOG2E

V7X_VMEM_LIMIT_BYTES = 56 * 1024 * 1024

ROW_TILE = 512
FFN_ROW_TILE = 1024
FFN_COL_TILE = 256
FFN_CONV_ROWS = 128
FFN_EDGE_ROWS = 256
GLA_TILE = 256
GLA_TILES_PER_STEP = 2
ATTN_TILE = 512
ATTN_KEY_TILE = 256
ATTN_HEADS_PER_STEP = 2


def _cparams(*sem):
    return pltpu.CompilerParams(dimension_semantics=sem,
                                vmem_limit_bytes=V7X_VMEM_LIMIT_BYTES)


def _rms(x, g):
    ms = jnp.mean(x * x, axis=-1, keepdims=True)
    return x * lax.rsqrt(ms + EPS) * g


def _sigmoid(z):
    return 1.0 / (1.0 + jnp.exp(-z))


def _row_tile(m, cap):
    t = min(m, cap)
    assert m % t == 0, (m, t)
    return t


def _norm_matmul_kernel(x_ref, g_ref, w_ref, o_ref, *, scale):
    h = _rms(x_ref[...], g_ref[...]).astype(BF16)
    n = w_ref.shape[1]
    tn = min(n, D_MODEL)
    for n0 in range(0, n, tn):
        y = jnp.dot(h, w_ref[:, n0:n0 + tn], preferred_element_type=F32)
        if scale != 1.0:
            y = y * scale
        o_ref[:, n0:n0 + tn] = y.astype(o_ref.dtype)


def _norm_matmul(x2d, g, w, out_dtype, name, scale=1.0):
    m, d = x2d.shape
    n = w.shape[1]
    tm = _row_tile(m, ROW_TILE)
    return pl.pallas_call(
        functools.partial(_norm_matmul_kernel, scale=scale),
        out_shape=jax.ShapeDtypeStruct((m, n), out_dtype),
        grid=(m // tm,),
        in_specs=[pl.BlockSpec((tm, d), lambda i: (i, 0)),
                  pl.BlockSpec((1, d), lambda i: (0, 0)),
                  pl.BlockSpec((d, n), lambda i: (0, 0))],
        out_specs=pl.BlockSpec((tm, n), lambda i: (i, 0)),
        compiler_params=_cparams("parallel"),
        name=name,
    )(x2d, g.reshape(1, d), w)


def _norm_kv_kernel(x_ref, g_ref, w_ref, k_ref, v_ref):
    d = k_ref.shape[1]
    h = _rms(x_ref[...], g_ref[...]).astype(BF16)
    k_ref[...] = jnp.dot(h, w_ref[:, :d], preferred_element_type=F32)
    v_ref[...] = jnp.dot(h, w_ref[:, d:], preferred_element_type=F32)


def _norm_kv(x2d, g, w, name):
    m, d = x2d.shape
    tm = _row_tile(m, ROW_TILE)
    out = jax.ShapeDtypeStruct((m, d), F32)
    return pl.pallas_call(
        _norm_kv_kernel,
        out_shape=(out, out),
        grid=(m // tm,),
        in_specs=[pl.BlockSpec((tm, d), lambda i: (i, 0)),
                  pl.BlockSpec((1, d), lambda i: (0, 0)),
                  pl.BlockSpec((d, 2 * d), lambda i: (0, 0))],
        out_specs=(pl.BlockSpec((tm, d), lambda i: (i, 0)),
                   pl.BlockSpec((tm, d), lambda i: (i, 0))),
        compiler_params=_cparams("parallel"),
        name=name,
    )(x2d, g.reshape(1, d), w)


def _ffn_kernel(x_ref, a_ref, wm_ref, gm_ref, gin_ref, wg_ref, wu_ref, cw_ref, cb_ref, wo_ref, gout_ref,
                cs_ref, p_ref, wgate_ref, wple_ref,
                o_ref, co_ref, xm_ref, h_ref, acc_ref, gu_ref, act_ref):
    t = pl.program_id(1)
    j = pl.program_id(2)
    nb, tm, d = x_ref.shape
    rows = nb * tm
    rb = min(rows, FFN_EDGE_ROWS)

    def chunk(ref, r0):
        return ref[0, r0:r0 + rb, :] if nb == 1 else ref[...].reshape(rows, ref.shape[2])

    @pl.when(j == 0)
    def _():
        for r0 in range(0, rows, rb):
            m = jnp.dot(chunk(a_ref, r0), wm_ref[...], preferred_element_type=F32)
            xm = chunk(x_ref, r0) + _rms(m, gm_ref[...])
            xm_ref[r0:r0 + rb, :] = xm
            h_ref[r0:r0 + rb, :] = _rms(xm, gin_ref[...]).astype(BF16)
        acc_ref[...] = jnp.zeros_like(acc_ref)

    @pl.when((j == 0) & (t == 0))
    def _():
        co_ref[...] = cs_ref[...]

    cw = cw_ref[...]
    cb = cb_ref[...]
    tf = cw.shape[1]
    gu_ref[:, :tf] = jnp.dot(h_ref[...], wg_ref[...], preferred_element_type=F32)
    gu_ref[:, tf:] = jnp.dot(h_ref[...], wu_ref[...], preferred_element_type=F32)
    r8 = lax.broadcasted_iota(jnp.int32, (8, tf), 0)
    rc = min(tm, FFN_CONV_ROWS)
    for bi in range(nb):
        base = bi * tm
        prev = co_ref[bi, j]
        co_ref[bi, j] = gu_ref[base + tm - (CONV_W - 1):base + tm, :tf]
        head = jnp.where(r8 == 6, prev[0:1, :], prev[1:2, :])
        for r0 in range(base, base + tm, rc):
            if r0 > base:
                head = gu_ref[r0 - 8:r0, :tf]
            gc = gu_ref[r0:r0 + rc, :tf]
            ext = jnp.concatenate([head, gc], axis=0)
            g1 = pltpu.roll(ext, 1, axis=0)[8:, :]
            g2 = pltpu.roll(ext, 2, axis=0)[8:, :]
            c = cb + cw[0:1, :] * g2 + cw[1:2, :] * g1 + cw[2:3, :] * gc
            act_ref[r0:r0 + rc, :] = (c * _sigmoid(c) * gu_ref[r0:r0 + rc, tf:]).astype(BF16)
    acc_ref[...] += jnp.dot(act_ref[...], wo_ref[...], preferred_element_type=F32)

    @pl.when(j == pl.num_programs(2) - 1)
    def _():
        for r0 in range(0, rows, rb):
            y = xm_ref[r0:r0 + rb, :] + _rms(acc_ref[r0:r0 + rb, :], gout_ref[...])
            z = jnp.dot(y.astype(BF16), wgate_ref[...], preferred_element_type=F32)
            e = jnp.dot(chunk(p_ref, r0).astype(BF16), wple_ref[...], preferred_element_type=F32)
            out = y + _sigmoid(z) * e
            if nb == 1:
                o_ref[0, r0:r0 + rb, :] = out
            else:
                o_ref[...] = out.reshape(nb, tm, d)


def _ffn(x, a, w_mix, g_mix, gin, w_in, cw, cb, w_out, gout, conv_state, p, w_gate, w_ple, name):
    b, t, d = x.shape
    f = w_out.shape[0]
    pd = p.shape[2]
    tm = _row_tile(t, FFN_ROW_TILE)
    nb = max(1, min(b, FFN_ROW_TILE // t))
    assert b % nb == 0
    rows = nb * tm
    assert rows % min(rows, FFN_EDGE_ROWS) == 0 and (nb == 1 or rows <= FFN_EDGE_ROWS)
    tf = FFN_COL_TILE
    nf = f // tf
    assert f % tf == 0 and t >= CONV_W - 1 and tm % 8 == 0
    cs = conv_state.reshape(b, CONV_W - 1, nf, tf).transpose(0, 2, 1, 3)
    cs_spec = pl.BlockSpec((nb, nf, CONV_W - 1, tf), lambda bi, ti, j: (bi, 0, 0, 0))
    row_spec = lambda width: pl.BlockSpec((nb, tm, width), lambda bi, ti, j: (bi, ti, 0))
    vec_spec = pl.BlockSpec((1, d), lambda bi, ti, j: (0, 0))
    const_spec = lambda shape: pl.BlockSpec(shape, lambda bi, ti, j: (0, 0),
                                            pipeline_mode=pl.Buffered(1))
    y, co = pl.pallas_call(
        _ffn_kernel,
        out_shape=(jax.ShapeDtypeStruct((b, t, d), F32),
                   jax.ShapeDtypeStruct(cs.shape, F32)),
        grid=(b // nb, t // tm, nf),
        in_specs=[row_spec(d), row_spec(d), const_spec((d, d)), vec_spec,
                  vec_spec,
                  pl.BlockSpec((d, tf), lambda bi, ti, j: (0, j)),
                  pl.BlockSpec((d, tf), lambda bi, ti, j: (0, j + nf)),
                  pl.BlockSpec((CONV_W, tf), lambda bi, ti, j: (0, j)),
                  pl.BlockSpec((1, tf), lambda bi, ti, j: (0, j)),
                  pl.BlockSpec((tf, d), lambda bi, ti, j: (j, 0)),
                  vec_spec,
                  cs_spec, row_spec(pd), const_spec((d, d)), const_spec((pd, d))],
        out_specs=(row_spec(d), cs_spec),
        scratch_shapes=[pltpu.VMEM((rows, d), F32),
                        pltpu.VMEM((rows, d), BF16),
                        pltpu.VMEM((rows, d), F32),
                        pltpu.VMEM((rows, 2 * tf), F32),
                        pltpu.VMEM((rows, tf), BF16)],
        compiler_params=_cparams("parallel", "arbitrary", "arbitrary"),
        name=name,
    )(x, a, w_mix, g_mix.reshape(1, d), gin.reshape(1, d), w_in, w_in, cw, cb.reshape(1, f), w_out,
      gout.reshape(1, d), cs, p, w_gate, w_ple)
    return y, co.transpose(0, 2, 1, 3).reshape(b, CONV_W - 1, f)


def _split3(x):
    hi = x.astype(BF16)
    r1 = x - hi.astype(F32)
    mid = r1.astype(BF16)
    lo = (r1 - mid.astype(F32)).astype(BF16)
    return hi, mid, lo


def _gla_levels(tc):
    return [s for s in (16, 32, 64, 128, 256, 512, 1024) if 2 * s <= tc]


def _gla_level_table(tc):
    r = np.arange(tc)[:, None]
    c = np.arange(tc)[None, :]
    table = np.full((tc, tc), -1, np.int32)
    table[(r // GLA_BLOCK == c // GLA_BLOCK) & (c <= r)] = 0
    for i, s in enumerate(_gla_levels(tc)):
        table[(r // (2 * s) == c // (2 * s)) & ((r // s) % 2 == 1) & ((c // s) % 2 == 0)] = i + 1
    return table


def _gla_kernel(x_ref, gx_ref, w_ref, ll_ref, l1m_ref, om_ref, gn_ref, s0_ref, lvl_ref,
                o_ref, so_ref, st_ref, proj_ref, bg_ref, fac_ref, qi_ref, ks_ref, v_ref, a_ref, oo_ref,
                *, tc):
    t = pl.program_id(1)
    d = x_ref.shape[2]
    n_sub = x_ref.shape[1] // tc

    def project(x_tile, slot):
        hx = _rms(x_tile, gx_ref[...]).astype(BF16)
        for n0 in range(0, 4 * d, d):
            proj_ref[slot, :, n0:n0 + d] = jnp.dot(hx, w_ref[:, n0:n0 + d], preferred_element_type=F32)

    @pl.when(t == 0)
    def _():
        for h in range(N_HEADS):
            st_ref[h] = s0_ref[0, h].T

    project(x_ref[0, 0:tc, :], 0)
    for s in range(n_sub):
        if s + 1 < n_sub:
            project(x_ref[0, (s + 1) * tc:(s + 2) * tc, :], (s + 1) % 2)
        _gla_tile(proj_ref.at[s % 2], ll_ref, l1m_ref, om_ref, gn_ref, lvl_ref, o_ref.at[0, s * tc:(s + 1) * tc, :],
                  st_ref, bg_ref, fac_ref, qi_ref, ks_ref, v_ref, a_ref, oo_ref)

    @pl.when(t == pl.num_programs(1) - 1)
    def _():
        for h in range(N_HEADS):
            so_ref[0, h] = st_ref[h].T


def _gla_tile(p_ref, ll_ref, l1m_ref, om_ref, gn_ref, lvl_ref, o_ref,
              st_ref, bg_ref, fac_ref, qi_ref, ks_ref, v_ref, a_ref, oo_ref):
    tc = p_ref.shape[0]
    d = p_ref.shape[1] // 4
    levels = _gla_levels(tc)

    qh = p_ref[:, 0:d]
    fp = p_ref[:, d:2 * d]
    q = qh * _sigmoid(qh)
    e = jnp.exp(-jnp.abs(fp))
    logsig = jnp.minimum(fp, 0.0) - jnp.log(1.0 + e)
    a = ll_ref[...]
    c = l1m_ref[...] + logsig
    logf = jnp.maximum(a, c) + jnp.log(1.0 + jnp.exp(-jnp.abs(a - c)))
    kk = om_ref[...] * (jnp.where(fp >= 0.0, e, 1.0) / (1.0 + e))

    row = lax.broadcasted_iota(jnp.int32, (tc, tc), 0)
    col = lax.broadcasted_iota(jnp.int32, (tc, tc), 1)
    m_incl = jnp.where(col <= row, 1.0, 0.0).astype(BF16)
    bg = sum(jnp.dot(m_incl, p, preferred_element_type=F32) for p in _split3(logf * LOG2E))
    bg_ref[...] = bg
    b_last = bg[tc - 1:tc, :]

    qi_ref[...] = (q * jnp.exp2(bg)).astype(BF16)
    ks_ref[...] = (kk * jnp.exp2(b_last - bg)).astype(BF16)
    v_ref[...] = p_ref[:, 2 * d:3 * d].astype(BF16)

    def ref_rows(span, offset):
        pieces = []
        for a0 in range(0, tc, span):
            r = a0 + offset
            src = bg_ref[r:r + 1, :] if r >= 0 else jnp.zeros((1, d), F32)
            pieces.append(jnp.broadcast_to(src, (span, d)))
        return pieces[0] if len(pieces) == 1 else jnp.concatenate(pieces, axis=0)

    dl = bg - ref_rows(GLA_BLOCK, -1)
    fac_ref[0] = (q * jnp.exp2(dl)).astype(BF16)
    fac_ref[1] = (kk * jnp.exp2(-dl)).astype(BF16)
    rid = lax.broadcasted_iota(jnp.int32, (tc, 1), 0)
    for li, s in enumerate(levels):
        mid = ref_rows(2 * s, s - 1)
        right = ((rid // s) % 2) == 1
        x = jnp.exp2(jnp.where(right, bg - mid, mid - bg))
        fac_ref[2 + 2 * li] = (q * x).astype(BF16)
        fac_ref[3 + 2 * li] = (kk * x).astype(BF16)

    nt = (((1,), (1,)), ((), ()))
    tn = (((0,), (0,)), ((), ()))
    lvl = lvl_ref[...]
    for h in range(N_HEADS):
        cols = slice(h * HEAD_W, (h + 1) * HEAD_W)
        att = jnp.zeros((tc, tc), F32)
        for ci in range(len(levels), -1, -1):
            prod = lax.dot_general(fac_ref[2 * ci, :, cols], fac_ref[2 * ci + 1, :, cols], nt,
                                   preferred_element_type=F32)
            att = jnp.where(lvl == ci, prod, att)
        a_ref[h] = att.astype(BF16)

    for h in range(N_HEADS):
        cols = slice(h * HEAD_W, (h + 1) * HEAD_W)
        vb = v_ref[:, cols]
        s_t = st_ref[h]
        o_inter = lax.dot_general(qi_ref[:, cols], s_t.astype(BF16), nt, preferred_element_type=F32)
        oo_ref[:, cols] = o_inter + jnp.dot(a_ref[h], vb, preferred_element_type=F32)
        upd = lax.dot_general(vb, ks_ref[:, cols], tn, preferred_element_type=F32)
        st_ref[h] = s_t * jnp.exp2(b_last[:, cols]) + upd

    gh = p_ref[:, 3 * d:4 * d]
    gate = gh * _sigmoid(gh)
    for h in range(N_HEADS):
        cols = slice(h * HEAD_W, (h + 1) * HEAD_W)
        oh = _rms(oo_ref[:, cols], gn_ref[:, cols])
        o_ref[:, cols] = (oh * gate[:, cols]).astype(o_ref.dtype)


def _gla(x, g_in, w_in, log_lower, log1m_lower, one_m_lower, g_out, s0, name):
    b, t, d = x.shape
    tc = _row_tile(t, GLA_TILE)
    assert tc % GLA_BLOCK == 0
    n_sub = GLA_TILES_PER_STEP if t % (GLA_TILES_PER_STEP * tc) == 0 else 1
    rows = n_sub * tc
    vec = pl.BlockSpec((1, d), lambda bi, ti: (0, 0))
    st_spec = pl.BlockSpec((1, N_HEADS, HEAD_W, HEAD_W), lambda bi, ti: (bi, 0, 0, 0))
    n_fac = 2 * (1 + len(_gla_levels(tc)))
    return pl.pallas_call(
        functools.partial(_gla_kernel, tc=tc),
        out_shape=(jax.ShapeDtypeStruct((b, t, d), BF16),
                   jax.ShapeDtypeStruct((b, N_HEADS, HEAD_W, HEAD_W), F32)),
        grid=(b, t // rows),
        in_specs=[pl.BlockSpec((1, rows, d), lambda bi, ti: (bi, ti, 0)),
                  vec,
                  pl.BlockSpec((d, 4 * d), lambda bi, ti: (0, 0), pipeline_mode=pl.Buffered(1)),
                  vec, vec, vec, vec, st_spec,
                  pl.BlockSpec((tc, tc), lambda bi, ti: (0, 0))],
        out_specs=(pl.BlockSpec((1, rows, d), lambda bi, ti: (bi, ti, 0)), st_spec),
        scratch_shapes=[pltpu.VMEM((N_HEADS, HEAD_W, HEAD_W), F32),
                        pltpu.VMEM((2, tc, 4 * d), F32),
                        pltpu.VMEM((tc, d), F32),
                        pltpu.VMEM((n_fac, tc, d), BF16),
                        pltpu.VMEM((tc, d), BF16), pltpu.VMEM((tc, d), BF16),
                        pltpu.VMEM((tc, d), BF16),
                        pltpu.VMEM((N_HEADS, tc, tc), BF16),
                        pltpu.VMEM((tc, d), F32)],
        compiler_params=_cparams("parallel", "arbitrary"),
        name=name,
    )(x, g_in.reshape(1, d), w_in, log_lower.reshape(1, d), log1m_lower.reshape(1, d),
      one_m_lower.reshape(1, d), g_out.reshape(1, d), s0, jnp.asarray(_gla_level_table(tc)))


def _t5_bucket(rel):
    half = NUM_BUCKETS // 2
    ret = jnp.where(rel > 0, half, 0)
    n = jnp.abs(rel)
    max_exact = half // 2
    nf = jnp.maximum(n, 1).astype(F32)
    large = max_exact + (jnp.log(nf / max_exact) / math.log(MAX_DISTANCE / max_exact)
                         * (half - max_exact)).astype(jnp.int32)
    large = jnp.minimum(large, half - 1)
    return ret + jnp.where(n < max_exact, n, large)


def _toeplitz_kernel(w_ref, o_ref, *, tq):
    tk = o_ref.shape[2]
    p = w_ref.shape[3]
    x = jnp.broadcast_to(w_ref[0, 0], (tk, p))
    o_ref[0, 0] = pltpu.roll(x, 0, axis=1, stride=1, stride_axis=0)[:, :tq]


def _bias_tiles_t(rel_bias, d0_list, tq, tk):
    p = pl.cdiv(tq + tk, HEAD_W) * HEAD_W
    m = jnp.arange(p, dtype=jnp.int32)
    rel = jnp.asarray(d0_list, jnp.int32)[:, None] + jnp.where(m <= tq, -m, p - m)[None, :]
    onehot = (_t5_bucket(rel)[:, :, None] == jnp.arange(NUM_BUCKETS, dtype=jnp.int32)).astype(F32)
    w = jnp.einsum("dpn,nh->hdp", onehot, rel_bias.astype(F32) * LOG2E,
                   precision=lax.Precision.HIGHEST)
    h, nd = w.shape[0], w.shape[1]
    return pl.pallas_call(
        functools.partial(_toeplitz_kernel, tq=tq),
        out_shape=jax.ShapeDtypeStruct((h, nd, tk, tq), F32),
        grid=(h, nd),
        in_specs=[pl.BlockSpec((1, 1, 1, p), lambda hi, di: (hi, di, 0, 0))],
        out_specs=pl.BlockSpec((1, 1, tk, tq), lambda hi, di: (hi, di, 0, 0)),
        compiler_params=_cparams("parallel", "parallel"),
        name="bias_tiles",
    )(w.reshape(h, nd, 1, p))


def _attn_kernel(*refs, segs, q_off, tq, hb):
    nseg = len(segs)
    q_ref, lam_ref, gn_ref = refs[0], refs[1], refs[2]
    seg_refs = [refs[3 + 3 * s: 6 + 3 * s] for s in range(nseg)]
    o_ref = refs[3 + 3 * nseg]
    m_ref, l_ref, acc_ref = refs[4 + 3 * nseg: 7 + 3 * nseg]
    kv_scratch = refs[7 + 3 * nseg:]
    seg_scr = [kv_scratch[4 * s: 4 * s + 4] for s in range(nseg)]

    i = pl.program_id(2)
    q0 = q_off + i * tq

    @pl.when(i == 0)
    def _():
        for (k_ref, v_ref, _), (kb_ref, vt_ref, _, _), (_, seg_len, tk, _, _) in zip(seg_refs, seg_scr, segs):
            kb_ref[...] = k_ref[0].astype(BF16)
            for kt in range(seg_len // tk):
                vt_ref[kt] = v_ref[0, kt * tk:(kt + 1) * tk, :].astype(F32).T.astype(BF16)

    q = q_ref[0]
    lane = lax.broadcasted_iota(jnp.int32, (tq, HEAD_W), 1)
    qz = []
    for hh in range(hb):
        qh = q[:, hh * HEAD_W:(hh + 1) * HEAD_W]
        qz.append([jnp.where(lane < DA_HEAD_DIM, qh, 0).astype(BF16),
                   jnp.where(lane >= DA_HEAD_DIM, qh, 0).astype(BF16)])
    m_ref[...] = jnp.full_like(m_ref, NEG_BIG)
    l_ref[...] = jnp.zeros_like(l_ref)
    acc_ref[...] = jnp.zeros_like(acc_ref)
    q_chunk = (q0 + lax.broadcasted_iota(jnp.int32, (1, tq), 1)) // CHUNK
    vis_end = ((q0 + tq - 1) // CHUNK + 1) * CHUNK
    full_end = (q0 // CHUNK + 1) * CHUNK
    nt = (((1,), (1,)), ((), ()))

    for (_, _, b_ref), (kb_ref, vt_ref, sa_ref, sb_ref), (pos0, seg_len, tk, d0_min, d0_step) in zip(
            seg_refs, seg_scr, segs):
        n_tiles = seg_len // tk
        n_vis = jnp.clip((vis_end - pos0 + tk - 1) // tk, 0, n_tiles)
        n_full = jnp.clip((full_end - pos0) // tk, 0, n_vis)

        def scores_into(kt, s_ref, c0=0, kb_ref=kb_ref, tk=tk, n_tiles=n_tiles):
            r0 = pl.multiple_of(jnp.minimum(kt, n_tiles - 1) * tk, tk)
            for hh in range(hb):
                kb = kb_ref[pl.ds(r0, tk), hh * HEAD_W:(hh + 1) * HEAD_W]
                for c in range(2):
                    s_ref[2 * hh + c, :, c0:] = lax.dot_general(kb, qz[hh][c][c0:, :], nt,
                                                                preferred_element_type=F32)

        def consume(kt, s_ref, masked, c0=0, b_ref=b_ref, vt_ref=vt_ref, pos0=pos0, tk=tk,
                    d0_min=d0_min, d0_step=d0_step, n_tiles=n_tiles, n_vis=n_vis):
            ktc = jnp.minimum(kt, n_tiles - 1)
            r0 = ktc * tk
            d_idx = (pos0 + r0 - q0 - d0_min) // d0_step
            if masked:
                k0 = jnp.where(kt < n_vis, pos0 + r0, jnp.int32(1 << 28))
                k_chunk = (k0 + lax.broadcasted_iota(jnp.int32, (tk, 1), 0)) // CHUNK
                visible = k_chunk <= q_chunk[:, c0:]
            for hh in range(hb):
                vt = vt_ref[ktc, hh * HEAD_W:(hh + 1) * HEAD_W, :]
                bias = b_ref[hh, d_idx, :, c0:]
                for c in range(2):
                    j = 2 * hh + c
                    s = s_ref[j, :, c0:] + bias
                    if masked:
                        s = jnp.where(visible, s, NEG_BIG)
                    m_prev = m_ref[j, :, c0:]
                    m_new = jnp.maximum(m_prev, jnp.max(s, axis=0, keepdims=True))
                    p = jnp.exp2(s - m_new)
                    alpha = jnp.exp2(m_prev - m_new)
                    l_ref[j, :, c0:] = alpha * l_ref[j, :, c0:] + jnp.sum(p, axis=0, keepdims=True)
                    acc_ref[j, :, c0:] = alpha * acc_ref[j, :, c0:] + jnp.dot(
                        vt, p.astype(BF16), preferred_element_type=F32)
                    m_ref[j, :, c0:] = m_new

        def pair(u, carry, masked, base, sa_ref=sa_ref, sb_ref=sb_ref):
            t0 = base + 2 * u
            scores_into(t0 + 1, sb_ref)
            consume(t0, sa_ref, masked)
            scores_into(t0 + 2, sa_ref)
            consume(t0 + 1, sb_ref, masked)
            return carry

        n_pairs_full = n_full // 2
        base = 2 * n_pairs_full
        scores_into(0, sa_ref)
        if n_tiles == 1:
            consume(0, sa_ref, True)
            continue
        lax.fori_loop(0, n_pairs_full, functools.partial(pair, masked=False, base=0), 0)
        if pos0 == q_off and tq == 2 * tk and tk % CHUNK == 0 and q_off % CHUNK == 0:
            scores_into(base + 1, sb_ref, c0=tk)
            consume(base, sa_ref, True)
            consume(base + 1, sb_ref, True, c0=tk)
        else:
            lax.fori_loop(0, (n_vis - base + 1) // 2, functools.partial(pair, masked=True, base=base), 0)

    lam = lam_ref[0:1, 0:1]
    for hh in range(hb):
        o_t = (acc_ref[2 * hh] * (1.0 / l_ref[2 * hh])
               - lam * (acc_ref[2 * hh + 1] * (1.0 / l_ref[2 * hh + 1])))
        cols = slice(hh * HEAD_W, (hh + 1) * HEAD_W)
        ms = jnp.mean(o_t * o_t, axis=0, keepdims=True)
        o_n = (o_t * lax.rsqrt(ms + EPS)).T
        o_ref[0, :, cols] = (o_n * gn_ref[:, cols] * lam_ref[0:1, 1:2]).astype(o_ref.dtype)


def _attention(q, segments, rel_bias, lam, out_scale, g_subln, q_off, name):
    b, t_real, d = q.shape
    tq_all = max(t_real, HEAD_W)
    if tq_all != t_real:
        q = jnp.pad(q, ((0, 0), (0, tq_all - t_real), (0, 0)))
    tq = _row_tile(tq_all, ATTN_TILE)
    nq = tq_all // tq
    hb = 2 * ATTN_HEADS_PER_STEP if nq == 1 else ATTN_HEADS_PER_STEP
    hw = hb * HEAD_W
    segs, args, in_specs, scratch = [], [], [], []
    for (k, v, pos0, tk) in segments:
        seg_len = k.shape[1]
        assert seg_len % tk == 0
        d0s = sorted({pos0 + kt * tk - (q_off + i * tq)
                      for i in range(nq) for kt in range(seg_len // tk)
                      if pos0 + kt * tk < ((q_off + (i + 1) * tq - 1) // CHUNK + 1) * CHUNK})
        step = math.gcd(tq, tk)
        d0s = list(range(d0s[0], d0s[-1] + 1, step))
        tiles = _bias_tiles_t(rel_bias, d0s, tq, tk)
        segs.append((pos0, seg_len, tk, d0s[0], step))
        args += [k, v, tiles]
        in_specs += [pl.BlockSpec((1, seg_len, hw), lambda bi, h, i: (bi, 0, h)),
                     pl.BlockSpec((1, seg_len, hw), lambda bi, h, i: (bi, 0, h)),
                     pl.BlockSpec((hb, len(d0s), tk, tq), lambda bi, h, i: (h, 0, 0, 0))]
        scratch += [pltpu.VMEM((seg_len, hw), BF16), pltpu.VMEM((seg_len // tk, hw, tk), BF16),
                    pltpu.VMEM((2 * hb, tk, tq), F32), pltpu.VMEM((2 * hb, tk, tq), F32)]
    scal = jnp.zeros((1, HEAD_W), F32).at[0, 0].set(lam).at[0, 1].set(out_scale)
    kern = functools.partial(_attn_kernel, segs=tuple(segs), q_off=q_off, tq=tq, hb=hb)
    out = pl.pallas_call(
        kern,
        out_shape=jax.ShapeDtypeStruct((b, tq_all, d), BF16),
        grid=(b, N_HEADS // hb, nq),
        in_specs=[pl.BlockSpec((1, tq, hw), lambda bi, h, i: (bi, i, h)),
                  pl.BlockSpec((1, HEAD_W), lambda bi, h, i: (0, 0)),
                  pl.BlockSpec((1, hw), lambda bi, h, i: (0, h))] + in_specs,
        out_specs=pl.BlockSpec((1, tq, hw), lambda bi, h, i: (bi, i, h)),
        scratch_shapes=[pltpu.VMEM((2 * hb, 1, tq), F32), pltpu.VMEM((2 * hb, 1, tq), F32),
                        pltpu.VMEM((2 * hb, HEAD_W, tq), F32)] + scratch,
        compiler_params=_cparams("parallel", "parallel", "arbitrary"),
        name=name,
    )(q, scal, g_subln.reshape(1, d), *args)
    return out[:, :t_real]


def _trunk(x, ple, q_off, past_k, past_v, hg_s0, conv_s0, wts, tag):
    (g_norms, w_in_a, lower, g_hg, w_out_a, g_kv, w_kv, rel_bias, w_q_b, lam_b,
     g_subln, w_out_b, w_ffn_in, conv_w, conv_b, w_ffn_out, w_ple, w_ple_gate) = wts
    b, t, d = x.shape
    depth = g_norms.shape[0]
    n_a = w_in_a.shape[0]
    m = b * t
    hg_out, conv_out = [], []
    k_new = v_new = None
    segments = None
    for i in range(depth):
        nm = f"{tag}{i}"
        x2 = x.reshape(m, d)
        if i < n_a:
            lw = lower[i]
            o, s = _gla(x, g_norms[i, 0], w_in_a[i], jnp.log(lw), jnp.log1p(-lw), 1.0 - lw,
                        g_hg[i], hg_s0[i], nm + "_gla")
            hg_out.append(s)
            w_mix = w_out_a[i]
        else:
            j = i - n_a
            q = _norm_matmul(x2, g_norms[i, 0], w_q_b[j], BF16, nm + "_q", scale=ATTN_Q_SCALE)
            lam_init = 0.8 - 0.6 * math.exp(-0.3 * i)
            lp = lam_b[j].astype(F32)
            lam = jnp.exp(jnp.sum(lp[0] * lp[1])) - jnp.exp(jnp.sum(lp[2] * lp[3])) + lam_init
            o = _attention(q.reshape(b, t, d), segments, rel_bias, lam, 1.0 - lam_init,
                           g_subln[j], q_off, nm + "_attn")
            w_mix = w_out_b[j]
        x, cbuf = _ffn(x, o.reshape(b, t, d), w_mix, g_norms[i, 1], g_norms[i, 2], w_ffn_in[i],
                       conv_w[i], conv_b[i], w_ffn_out[i], g_norms[i, 3], conv_s0[i],
                       ple[i], w_ple_gate[i], w_ple[i], nm + "_ffn")
        conv_out.append(cbuf)
        x2 = x.reshape(m, d)
        if i == n_a - 1:
            k_new, v_new = _norm_kv(x2, g_kv, w_kv, nm + "_kv")
            k_new = k_new.reshape(b, t, d)
            v_new = v_new.reshape(b, t, d)
            segments = []
            if past_k is not None:
                tp = past_k.shape[1]
                segments.append((past_k.reshape(b, tp, d), past_v.reshape(b, tp, d), 0, tp))
            segments.append((k_new, v_new, q_off, min(t, ATTN_KEY_TILE)))
    hd = (b, t, N_HEADS, HEAD_W)
    return x, k_new.reshape(hd), v_new.reshape(hd), jnp.stack(hg_out), jnp.stack(conv_out)


def kernel(x_prompt, x_sample, cache_k, cache_v, state_hgrn, state_conv, p_prompt, p_sample,
           g_norms, w_in_a, lb_raw, g_hg, w_out_a, g_kv, w_kv, rel_bias,
           w_q_b, lam_b, g_subln, w_out_b, w_ffn_in, conv_w, conv_b, w_ffn_out,
           w_ple, w_ple_gate):
    sm = jax.nn.softmax(lb_raw.astype(F32), axis=0)
    cs = jnp.cumsum(sm, axis=0)
    lower = cs - cs[0:1]
    bf = lambda w: w.astype(BF16)
    wts = (g_norms, bf(w_in_a), lower, g_hg, bf(w_out_a), g_kv, bf(w_kv), rel_bias,
           bf(w_q_b), lam_b, g_subln, bf(w_out_b), bf(w_ffn_in), conv_w, conv_b,
           bf(w_ffn_out), bf(w_ple), bf(w_ple_gate))
    bp, tp, _ = x_prompt.shape
    n_a, depth = w_in_a.shape[0], g_norms.shape[0]
    hg0 = jnp.zeros((n_a, bp) + state_hgrn.shape[2:], F32)
    cv0 = jnp.zeros((depth, bp) + state_conv.shape[2:], F32)
    y_p, k_p, v_p, hg_p, cv_p = _trunk(x_prompt, p_prompt, 0, None, None, hg0, cv0, wts, "p")
    y_s, k_s, v_s, hg_s, cv_s = _trunk(x_sample, p_sample, cache_k.shape[1], cache_k, cache_v,
                                       state_hgrn, state_conv, wts, "s")
    return (y_p, y_s, k_p, v_p, k_s, v_s, hg_p, hg_s, cv_p, cv_s)
```

```python
import functools
import math

import numpy as np
import jax
import jax.numpy as jnp
from jax import lax
from jax.experimental import pallas as pl
from jax.experimental.pallas import tpu as pltpu

F32 = jnp.float32
BF16 = jnp.bfloat16

D_MODEL = 1024
N_HEADS = 8
HEAD_W = D_MODEL // N_HEADS
DA_HEAD_DIM = HEAD_W // 2
CHUNK = 64
GLA_BLOCK = 16
NUM_BUCKETS = 32
MAX_DISTANCE = 256
CONV_W = 3
EPS = 1e-6
NEG_BIG = -1e30
LOG2E = math.log2(math.e)
ATTN_Q_SCALE = DA_HEAD_DIM ** -0.5 * LOG2E

V7X_VMEM_LIMIT_BYTES = 56 * 1024 * 1024

ROW_TILE = 1024
FFN_ROW_TILE = 1024
FFN_COL_TILE = 256
FFN_CONV_ROWS = 128
FFN_EDGE_ROWS = 256
GLA_TILE = 256
GLA_TILES_PER_STEP = 2
ATTN_TILE = 512
ATTN_KEY_TILE = 256
ATTN_HEADS_PER_STEP = 2


def _cparams(*sem):
    return pltpu.CompilerParams(dimension_semantics=sem,
                                vmem_limit_bytes=V7X_VMEM_LIMIT_BYTES)


def _rms(x, g):
    ms = jnp.mean(x * x, axis=-1, keepdims=True)
    return x * lax.rsqrt(ms + EPS) * g


def _sigmoid(z):
    return 1.0 / (1.0 + jnp.exp(-z))


def _row_tile(m, cap):
    t = min(m, cap)
    assert m % t == 0, (m, t)
    return t


def _norm_matmul_kernel(x_ref, g_ref, w_ref, o_ref, *, scale):
    h = _rms(x_ref[...], g_ref[...]).astype(BF16)
    n = w_ref.shape[1]
    tn = min(n, D_MODEL)
    for n0 in range(0, n, tn):
        y = jnp.dot(h, w_ref[:, n0:n0 + tn], preferred_element_type=F32)
        if scale != 1.0:
            y = y * scale
        o_ref[:, n0:n0 + tn] = y.astype(o_ref.dtype)


def _norm_matmul(x2d, g, w, out_dtype, name, scale=1.0):
    m, d = x2d.shape
    n = w.shape[1]
    tm = _row_tile(m, ROW_TILE)
    return pl.pallas_call(
        functools.partial(_norm_matmul_kernel, scale=scale),
        out_shape=jax.ShapeDtypeStruct((m, n), out_dtype),
        grid=(m // tm,),
        in_specs=[pl.BlockSpec((tm, d), lambda i: (i, 0)),
                  pl.BlockSpec((1, d), lambda i: (0, 0)),
                  pl.BlockSpec((d, n), lambda i: (0, 0))],
        out_specs=pl.BlockSpec((tm, n), lambda i: (i, 0)),
        compiler_params=_cparams("parallel"),
        name=name,
    )(x2d, g.reshape(1, d), w)


def _norm_kv_kernel(x_ref, g_ref, w_ref, k_ref, v_ref):
    d = k_ref.shape[1]
    h = _rms(x_ref[...], g_ref[...]).astype(BF16)
    k_ref[...] = jnp.dot(h, w_ref[:, :d], preferred_element_type=F32)
    v_ref[...] = jnp.dot(h, w_ref[:, d:], preferred_element_type=F32)


def _norm_kv(x2d, g, w, name):
    m, d = x2d.shape
    tm = _row_tile(m, ROW_TILE)
    out = jax.ShapeDtypeStruct((m, d), F32)
    return pl.pallas_call(
        _norm_kv_kernel,
        out_shape=(out, out),
        grid=(m // tm,),
        in_specs=[pl.BlockSpec((tm, d), lambda i: (i, 0)),
                  pl.BlockSpec((1, d), lambda i: (0, 0)),
                  pl.BlockSpec((d, 2 * d), lambda i: (0, 0))],
        out_specs=(pl.BlockSpec((tm, d), lambda i: (i, 0)),
                   pl.BlockSpec((tm, d), lambda i: (i, 0))),
        compiler_params=_cparams("parallel"),
        name=name,
    )(x2d, g.reshape(1, d), w)


def _ffn_kernel(x_ref, a_ref, wm_ref, gm_ref, gin_ref, wg_ref, wu_ref, cw_ref, cb_ref, wo_ref, gout_ref,
                cs_ref, p_ref, wgate_ref, wple_ref,
                o_ref, co_ref, xm_ref, h_ref, acc_ref, gu_ref, act_ref):
    t = pl.program_id(1)
    j = pl.program_id(2)
    nb, tm, d = x_ref.shape
    rows = nb * tm
    rb = min(rows, FFN_EDGE_ROWS)

    def chunk(ref, r0):
        return ref[0, r0:r0 + rb, :] if nb == 1 else ref[...].reshape(rows, ref.shape[2])

    @pl.when(j == 0)
    def _():
        for r0 in range(0, rows, rb):
            m = jnp.dot(chunk(a_ref, r0), wm_ref[...], preferred_element_type=F32)
            xm = chunk(x_ref, r0) + _rms(m, gm_ref[...])
            xm_ref[r0:r0 + rb, :] = xm
            h_ref[r0:r0 + rb, :] = _rms(xm, gin_ref[...]).astype(BF16)
        acc_ref[...] = jnp.zeros_like(acc_ref)

    @pl.when((j == 0) & (t == 0))
    def _():
        co_ref[...] = cs_ref[...]

    cw = cw_ref[...]
    cb = cb_ref[...]
    tf = cw.shape[1]
    gu_ref[:, :tf] = jnp.dot(h_ref[...], wg_ref[...], preferred_element_type=F32)
    gu_ref[:, tf:] = jnp.dot(h_ref[...], wu_ref[...], preferred_element_type=F32)
    r8 = lax.broadcasted_iota(jnp.int32, (8, tf), 0)
    rc = min(tm, FFN_CONV_ROWS)
    for bi in range(nb):
        base = bi * tm
        prev = co_ref[bi, j]
        co_ref[bi, j] = gu_ref[base + tm - (CONV_W - 1):base + tm, :tf]
        head = jnp.where(r8 == 6, prev[0:1, :], prev[1:2, :])
        for r0 in range(base, base + tm, rc):
            if r0 > base:
                head = gu_ref[r0 - 8:r0, :tf]
            gc = gu_ref[r0:r0 + rc, :tf]
            ext = jnp.concatenate([head, gc], axis=0)
            g1 = pltpu.roll(ext, 1, axis=0)[8:, :]
            g2 = pltpu.roll(ext, 2, axis=0)[8:, :]
            c = cb + cw[0:1, :] * g2 + cw[1:2, :] * g1 + cw[2:3, :] * gc
            act_ref[r0:r0 + rc, :] = (c * _sigmoid(c) * gu_ref[r0:r0 + rc, tf:]).astype(BF16)
    acc_ref[...] += jnp.dot(act_ref[...], wo_ref[...], preferred_element_type=F32)

    @pl.when(j == pl.num_programs(2) - 1)
    def _():
        for r0 in range(0, rows, rb):
            y = xm_ref[r0:r0 + rb, :] + _rms(acc_ref[r0:r0 + rb, :], gout_ref[...])
            z = jnp.dot(y.astype(BF16), wgate_ref[...], preferred_element_type=F32)
            e = jnp.dot(chunk(p_ref, r0).astype(BF16), wple_ref[...], preferred_element_type=F32)
            out = y + _sigmoid(z) * e
            if nb == 1:
                o_ref[0, r0:r0 + rb, :] = out
            else:
                o_ref[...] = out.reshape(nb, tm, d)


def _ffn(x, a, w_mix, g_mix, gin, w_in, cw, cb, w_out, gout, conv_state, p, w_gate, w_ple, name):
    b, t, d = x.shape
    f = w_out.shape[0]
    pd = p.shape[2]
    tm = _row_tile(t, FFN_ROW_TILE)
    nb = max(1, min(b, FFN_ROW_TILE // t))
    assert b % nb == 0
    rows = nb * tm
    assert rows % min(rows, FFN_EDGE_ROWS) == 0 and (nb == 1 or rows <= FFN_EDGE_ROWS)
    tf = FFN_COL_TILE
    nf = f // tf
    assert f % tf == 0 and t >= CONV_W - 1 and tm % 8 == 0
    cs = conv_state.reshape(b, CONV_W - 1, nf, tf).transpose(0, 2, 1, 3)
    cs_spec = pl.BlockSpec((nb, nf, CONV_W - 1, tf), lambda bi, ti, j: (bi, 0, 0, 0))
    row_spec = lambda width: pl.BlockSpec((nb, tm, width), lambda bi, ti, j: (bi, ti, 0))
    vec_spec = pl.BlockSpec((1, d), lambda bi, ti, j: (0, 0))
    const_spec = lambda shape: pl.BlockSpec(shape, lambda bi, ti, j: (0, 0),
                                            pipeline_mode=pl.Buffered(1))
    y, co = pl.pallas_call(
        _ffn_kernel,
        out_shape=(jax.ShapeDtypeStruct((b, t, d), F32),
                   jax.ShapeDtypeStruct(cs.shape, F32)),
        grid=(b // nb, t // tm, nf),
        in_specs=[row_spec(d), row_spec(d), const_spec((d, d)), vec_spec,
                  vec_spec,
                  pl.BlockSpec((d, tf), lambda bi, ti, j: (0, j)),
                  pl.BlockSpec((d, tf), lambda bi, ti, j: (0, j + nf)),
                  pl.BlockSpec((CONV_W, tf), lambda bi, ti, j: (0, j)),
                  pl.BlockSpec((1, tf), lambda bi, ti, j: (0, j)),
                  pl.BlockSpec((tf, d), lambda bi, ti, j: (j, 0)),
                  vec_spec,
                  cs_spec, row_spec(pd), const_spec((d, d)), const_spec((pd, d))],
        out_specs=(row_spec(d), cs_spec),
        scratch_shapes=[pltpu.VMEM((rows, d), F32),
                        pltpu.VMEM((rows, d), BF16),
                        pltpu.VMEM((rows, d), F32),
                        pltpu.VMEM((rows, 2 * tf), F32),
                        pltpu.VMEM((rows, tf), BF16)],
        compiler_params=_cparams("parallel", "arbitrary", "arbitrary"),
        name=name,
    )(x, a, w_mix, g_mix.reshape(1, d), gin.reshape(1, d), w_in, w_in, cw, cb.reshape(1, f), w_out,
      gout.reshape(1, d), cs, p, w_gate, w_ple)
    return y, co.transpose(0, 2, 1, 3).reshape(b, CONV_W - 1, f)


def _split3(x):
    hi = x.astype(BF16)
    r1 = x - hi.astype(F32)
    mid = r1.astype(BF16)
    lo = (r1 - mid.astype(F32)).astype(BF16)
    return hi, mid, lo


def _gla_levels(tc):
    return [s for s in (16, 32, 64, 128, 256, 512, 1024) if 2 * s <= tc]


def _gla_level_table(tc):
    r = np.arange(tc)[:, None]
    c = np.arange(tc)[None, :]
    table = np.full((tc, tc), -1, np.int32)
    table[(r // GLA_BLOCK == c // GLA_BLOCK) & (c <= r)] = 0
    for i, s in enumerate(_gla_levels(tc)):
        table[(r // (2 * s) == c // (2 * s)) & ((r // s) % 2 == 1) & ((c // s) % 2 == 0)] = i + 1
    return table


def _gla_kernel(x_ref, gx_ref, w_ref, ll_ref, l1m_ref, om_ref, gn_ref, s0_ref, lvl_ref,
                o_ref, so_ref, st_ref, proj_ref, bg_ref, fac_ref, qi_ref, ks_ref, v_ref, a_ref, oo_ref,
                *, tc):
    t = pl.program_id(1)
    d = x_ref.shape[2]
    n_sub = x_ref.shape[1] // tc

    def project(x_tile, slot):
        hx = _rms(x_tile, gx_ref[...]).astype(BF16)
        for n0 in range(0, 4 * d, d):
            proj_ref[slot, :, n0:n0 + d] = jnp.dot(hx, w_ref[:, n0:n0 + d], preferred_element_type=F32)

    @pl.when(t == 0)
    def _():
        for h in range(N_HEADS):
            st_ref[h] = s0_ref[0, h].T

    project(x_ref[0, 0:tc, :], 0)
    for s in range(n_sub):
        if s + 1 < n_sub:
            project(x_ref[0, (s + 1) * tc:(s + 2) * tc, :], (s + 1) % 2)
        _gla_tile(proj_ref.at[s % 2], ll_ref, l1m_ref, om_ref, gn_ref, lvl_ref, o_ref.at[0, s * tc:(s + 1) * tc, :],
                  st_ref, bg_ref, fac_ref, qi_ref, ks_ref, v_ref, a_ref, oo_ref)

    @pl.when(t == pl.num_programs(1) - 1)
    def _():
        for h in range(N_HEADS):
            so_ref[0, h] = st_ref[h].T


def _gla_tile(p_ref, ll_ref, l1m_ref, om_ref, gn_ref, lvl_ref, o_ref,
              st_ref, bg_ref, fac_ref, qi_ref, ks_ref, v_ref, a_ref, oo_ref):
    tc = p_ref.shape[0]
    d = p_ref.shape[1] // 4
    levels = _gla_levels(tc)

    qh = p_ref[:, 0:d]
    fp = p_ref[:, d:2 * d]
    q = qh * _sigmoid(qh)
    e = jnp.exp(-jnp.abs(fp))
    logsig = jnp.minimum(fp, 0.0) - jnp.log(1.0 + e)
    a = ll_ref[...]
    c = l1m_ref[...] + logsig
    logf = jnp.maximum(a, c) + jnp.log(1.0 + jnp.exp(-jnp.abs(a - c)))
    kk = om_ref[...] * (jnp.where(fp >= 0.0, e, 1.0) / (1.0 + e))

    row = lax.broadcasted_iota(jnp.int32, (tc, tc), 0)
    col = lax.broadcasted_iota(jnp.int32, (tc, tc), 1)
    m_incl = jnp.where(col <= row, 1.0, 0.0).astype(BF16)
    bg = sum(jnp.dot(m_incl, p, preferred_element_type=F32) for p in _split3(logf * LOG2E))
    bg_ref[...] = bg
    b_last = bg[tc - 1:tc, :]

    qi_ref[...] = (q * jnp.exp2(bg)).astype(BF16)
    ks_ref[...] = (kk * jnp.exp2(b_last - bg)).astype(BF16)
    v_ref[...] = p_ref[:, 2 * d:3 * d].astype(BF16)

    def ref_rows(span, offset):
        pieces = []
        for a0 in range(0, tc, span):
            r = a0 + offset
            src = bg_ref[r:r + 1, :] if r >= 0 else jnp.zeros((1, d), F32)
            pieces.append(jnp.broadcast_to(src, (span, d)))
        return pieces[0] if len(pieces) == 1 else jnp.concatenate(pieces, axis=0)

    dl = bg - ref_rows(GLA_BLOCK, -1)
    fac_ref[0] = (q * jnp.exp2(dl)).astype(BF16)
    fac_ref[1] = (kk * jnp.exp2(-dl)).astype(BF16)
    rid = lax.broadcasted_iota(jnp.int32, (tc, 1), 0)
    for li, s in enumerate(levels):
        mid = ref_rows(2 * s, s - 1)
        right = ((rid // s) % 2) == 1
        x = jnp.exp2(jnp.where(right, bg - mid, mid - bg))
        fac_ref[2 + 2 * li] = (q * x).astype(BF16)
        fac_ref[3 + 2 * li] = (kk * x).astype(BF16)

    nt = (((1,), (1,)), ((), ()))
    tn = (((0,), (0,)), ((), ()))
    lvl = lvl_ref[...]
    for h in range(N_HEADS):
        cols = slice(h * HEAD_W, (h + 1) * HEAD_W)
        att = jnp.zeros((tc, tc), F32)
        for ci in range(len(levels), -1, -1):
            prod = lax.dot_general(fac_ref[2 * ci, :, cols], fac_ref[2 * ci + 1, :, cols], nt,
                                   preferred_element_type=F32)
            att = jnp.where(lvl == ci, prod, att)
        a_ref[h] = att.astype(BF16)

    for h in range(N_HEADS):
        cols = slice(h * HEAD_W, (h + 1) * HEAD_W)
        vb = v_ref[:, cols]
        s_t = st_ref[h]
        o_inter = lax.dot_general(qi_ref[:, cols], s_t.astype(BF16), nt, preferred_element_type=F32)
        oo_ref[:, cols] = o_inter + jnp.dot(a_ref[h], vb, preferred_element_type=F32)
        upd = lax.dot_general(vb, ks_ref[:, cols], tn, preferred_element_type=F32)
        st_ref[h] = s_t * jnp.exp2(b_last[:, cols]) + upd

    gh = p_ref[:, 3 * d:4 * d]
    gate = gh * _sigmoid(gh)
    for h in range(N_HEADS):
        cols = slice(h * HEAD_W, (h + 1) * HEAD_W)
        oh = _rms(oo_ref[:, cols], gn_ref[:, cols])
        o_ref[:, cols] = (oh * gate[:, cols]).astype(o_ref.dtype)


def _gla(x, g_in, w_in, log_lower, log1m_lower, one_m_lower, g_out, s0, name):
    b, t, d = x.shape
    tc = _row_tile(t, GLA_TILE)
    assert tc % GLA_BLOCK == 0
    n_sub = GLA_TILES_PER_STEP if t % (GLA_TILES_PER_STEP * tc) == 0 else 1
    rows = n_sub * tc
    vec = pl.BlockSpec((1, d), lambda bi, ti: (0, 0))
    st_spec = pl.BlockSpec((1, N_HEADS, HEAD_W, HEAD_W), lambda bi, ti: (bi, 0, 0, 0))
    n_fac = 2 * (1 + len(_gla_levels(tc)))
    return pl.pallas_call(
        functools.partial(_gla_kernel, tc=tc),
        out_shape=(jax.ShapeDtypeStruct((b, t, d), BF16),
                   jax.ShapeDtypeStruct((b, N_HEADS, HEAD_W, HEAD_W), F32)),
        grid=(b, t // rows),
        in_specs=[pl.BlockSpec((1, rows, d), lambda bi, ti: (bi, ti, 0)),
                  vec,
                  pl.BlockSpec((d, 4 * d), lambda bi, ti: (0, 0), pipeline_mode=pl.Buffered(1)),
                  vec, vec, vec, vec, st_spec,
                  pl.BlockSpec((tc, tc), lambda bi, ti: (0, 0))],
        out_specs=(pl.BlockSpec((1, rows, d), lambda bi, ti: (bi, ti, 0)), st_spec),
        scratch_shapes=[pltpu.VMEM((N_HEADS, HEAD_W, HEAD_W), F32),
                        pltpu.VMEM((2, tc, 4 * d), F32),
                        pltpu.VMEM((tc, d), F32),
                        pltpu.VMEM((n_fac, tc, d), BF16),
                        pltpu.VMEM((tc, d), BF16), pltpu.VMEM((tc, d), BF16),
                        pltpu.VMEM((tc, d), BF16),
                        pltpu.VMEM((N_HEADS, tc, tc), BF16),
                        pltpu.VMEM((tc, d), F32)],
        compiler_params=_cparams("parallel", "arbitrary"),
        name=name,
    )(x, g_in.reshape(1, d), w_in, log_lower.reshape(1, d), log1m_lower.reshape(1, d),
      one_m_lower.reshape(1, d), g_out.reshape(1, d), s0, jnp.asarray(_gla_level_table(tc)))


def _t5_bucket(rel):
    half = NUM_BUCKETS // 2
    ret = jnp.where(rel > 0, half, 0)
    n = jnp.abs(rel)
    max_exact = half // 2
    nf = jnp.maximum(n, 1).astype(F32)
    large = max_exact + (jnp.log(nf / max_exact) / math.log(MAX_DISTANCE / max_exact)
                         * (half - max_exact)).astype(jnp.int32)
    large = jnp.minimum(large, half - 1)
    return ret + jnp.where(n < max_exact, n, large)


def _toeplitz_kernel(w_ref, o_ref, *, tq):
    tk = o_ref.shape[2]
    p = w_ref.shape[3]
    x = jnp.broadcast_to(w_ref[0, 0], (tk, p))
    o_ref[0, 0] = pltpu.roll(x, 0, axis=1, stride=1, stride_axis=0)[:, :tq]


def _bias_tiles_t(rel_bias, d0_list, tq, tk):
    p = pl.cdiv(tq + tk, HEAD_W) * HEAD_W
    m = jnp.arange(p, dtype=jnp.int32)
    rel = jnp.asarray(d0_list, jnp.int32)[:, None] + jnp.where(m <= tq, -m, p - m)[None, :]
    onehot = (_t5_bucket(rel)[:, :, None] == jnp.arange(NUM_BUCKETS, dtype=jnp.int32)).astype(F32)
    w = jnp.einsum("dpn,nh->hdp", onehot, rel_bias.astype(F32) * LOG2E,
                   precision=lax.Precision.HIGHEST)
    h, nd = w.shape[0], w.shape[1]
    return pl.pallas_call(
        functools.partial(_toeplitz_kernel, tq=tq),
        out_shape=jax.ShapeDtypeStruct((h, nd, tk, tq), F32),
        grid=(h, nd),
        in_specs=[pl.BlockSpec((1, 1, 1, p), lambda hi, di: (hi, di, 0, 0))],
        out_specs=pl.BlockSpec((1, 1, tk, tq), lambda hi, di: (hi, di, 0, 0)),
        compiler_params=_cparams("parallel", "parallel"),
        name="bias_tiles",
    )(w.reshape(h, nd, 1, p))


def _attn_kernel(*refs, segs, q_off, tq, hb):
    nseg = len(segs)
    q_ref, lam_ref, gn_ref = refs[0], refs[1], refs[2]
    seg_refs = [refs[3 + 3 * s: 6 + 3 * s] for s in range(nseg)]
    o_ref = refs[3 + 3 * nseg]
    m_ref, l_ref, acc_ref = refs[4 + 3 * nseg: 7 + 3 * nseg]
    kv_scratch = refs[7 + 3 * nseg:]
    seg_scr = [kv_scratch[4 * s: 4 * s + 4] for s in range(nseg)]

    i = pl.program_id(2)
    q0 = q_off + i * tq

    @pl.when(i == 0)
    def _():
        for (k_ref, v_ref, _), (kb_ref, vt_ref, _, _), (_, seg_len, tk, _, _) in zip(seg_refs, seg_scr, segs):
            kb_ref[...] = k_ref[0].astype(BF16)
            for kt in range(seg_len // tk):
                vt_ref[kt] = v_ref[0, kt * tk:(kt + 1) * tk, :].astype(F32).T.astype(BF16)

    q = q_ref[0]
    lane = lax.broadcasted_iota(jnp.int32, (tq, HEAD_W), 1)
    qz = []
    for hh in range(hb):
        qh = q[:, hh * HEAD_W:(hh + 1) * HEAD_W]
        qz.append([jnp.where(lane < DA_HEAD_DIM, qh, 0).astype(BF16),
                   jnp.where(lane >= DA_HEAD_DIM, qh, 0).astype(BF16)])
    m_ref[...] = jnp.full_like(m_ref, NEG_BIG)
    l_ref[...] = jnp.zeros_like(l_ref)
    acc_ref[...] = jnp.zeros_like(acc_ref)
    q_chunk = (q0 + lax.broadcasted_iota(jnp.int32, (1, tq), 1)) // CHUNK
    vis_end = ((q0 + tq - 1) // CHUNK + 1) * CHUNK
    full_end = (q0 // CHUNK + 1) * CHUNK
    nt = (((1,), (1,)), ((), ()))

    for (_, _, b_ref), (kb_ref, vt_ref, sa_ref, sb_ref), (pos0, seg_len, tk, d0_min, d0_step) in zip(
            seg_refs, seg_scr, segs):
        n_tiles = seg_len // tk
        n_vis = jnp.clip((vis_end - pos0 + tk - 1) // tk, 0, n_tiles)
        n_full = jnp.clip((full_end - pos0) // tk, 0, n_vis)

        def scores_into(kt, s_ref, c0=0, kb_ref=kb_ref, tk=tk, n_tiles=n_tiles):
            r0 = pl.multiple_of(jnp.minimum(kt, n_tiles - 1) * tk, tk)
            for hh in range(hb):
                kb = kb_ref[pl.ds(r0, tk), hh * HEAD_W:(hh + 1) * HEAD_W]
                for c in range(2):
                    s_ref[2 * hh + c, :, c0:] = lax.dot_general(kb, qz[hh][c][c0:, :], nt,
                                                                preferred_element_type=F32)

        def consume(kt, s_ref, masked, c0=0, b_ref=b_ref, vt_ref=vt_ref, pos0=pos0, tk=tk,
                    d0_min=d0_min, d0_step=d0_step, n_tiles=n_tiles, n_vis=n_vis):
            ktc = jnp.minimum(kt, n_tiles - 1)
            r0 = ktc * tk
            d_idx = (pos0 + r0 - q0 - d0_min) // d0_step
            if masked:
                k0 = jnp.where(kt < n_vis, pos0 + r0, jnp.int32(1 << 28))
                k_chunk = (k0 + lax.broadcasted_iota(jnp.int32, (tk, 1), 0)) // CHUNK
                visible = k_chunk <= q_chunk[:, c0:]
            for hh in range(hb):
                vt = vt_ref[ktc, hh * HEAD_W:(hh + 1) * HEAD_W, :]
                bias = b_ref[hh, d_idx, :, c0:]
                for c in range(2):
                    j = 2 * hh + c
                    s = s_ref[j, :, c0:] + bias
                    if masked:
                        s = jnp.where(visible, s, NEG_BIG)
                    m_prev = m_ref[j, :, c0:]
                    m_new = jnp.maximum(m_prev, jnp.max(s, axis=0, keepdims=True))
                    p = jnp.exp2(s - m_new)
                    alpha = jnp.exp2(m_prev - m_new)
                    l_ref[j, :, c0:] = alpha * l_ref[j, :, c0:] + jnp.sum(p, axis=0, keepdims=True)
                    acc_ref[j, :, c0:] = alpha * acc_ref[j, :, c0:] + jnp.dot(
                        vt, p.astype(BF16), preferred_element_type=F32)
                    m_ref[j, :, c0:] = m_new

        def pair(u, carry, masked, base, sa_ref=sa_ref, sb_ref=sb_ref):
            t0 = base + 2 * u
            scores_into(t0 + 1, sb_ref)
            consume(t0, sa_ref, masked)
            scores_into(t0 + 2, sa_ref)
            consume(t0 + 1, sb_ref, masked)
            return carry

        n_pairs_full = n_full // 2
        base = 2 * n_pairs_full
        scores_into(0, sa_ref)
        if n_tiles == 1:
            consume(0, sa_ref, True)
            continue
        lax.fori_loop(0, n_pairs_full, functools.partial(pair, masked=False, base=0), 0)
        if pos0 == q_off and tq == 2 * tk and tk % CHUNK == 0 and q_off % CHUNK == 0:
            scores_into(base + 1, sb_ref, c0=tk)
            consume(base, sa_ref, True)
            consume(base + 1, sb_ref, True, c0=tk)
        else:
            lax.fori_loop(0, (n_vis - base + 1) // 2, functools.partial(pair, masked=True, base=base), 0)

    lam = lam_ref[0:1, 0:1]
    for hh in range(hb):
        o_t = (acc_ref[2 * hh] * (1.0 / l_ref[2 * hh])
               - lam * (acc_ref[2 * hh + 1] * (1.0 / l_ref[2 * hh + 1])))
        cols = slice(hh * HEAD_W, (hh + 1) * HEAD_W)
        ms = jnp.mean(o_t * o_t, axis=0, keepdims=True)
        o_n = (o_t * lax.rsqrt(ms + EPS)).T
        o_ref[0, :, cols] = (o_n * gn_ref[:, cols] * lam_ref[0:1, 1:2]).astype(o_ref.dtype)


def _attention(q, segments, rel_bias, lam, out_scale, g_subln, q_off, name):
    b, t_real, d = q.shape
    tq_all = max(t_real, HEAD_W)
    if tq_all != t_real:
        q = jnp.pad(q, ((0, 0), (0, tq_all - t_real), (0, 0)))
    tq = _row_tile(tq_all, ATTN_TILE)
    nq = tq_all // tq
    hb = 2 * ATTN_HEADS_PER_STEP if nq == 1 else ATTN_HEADS_PER_STEP
    hw = hb * HEAD_W
    segs, args, in_specs, scratch = [], [], [], []
    for (k, v, pos0, tk) in segments:
        seg_len = k.shape[1]
        assert seg_len % tk == 0
        d0s = sorted({pos0 + kt * tk - (q_off + i * tq)
                      for i in range(nq) for kt in range(seg_len // tk)
                      if pos0 + kt * tk < ((q_off + (i + 1) * tq - 1) // CHUNK + 1) * CHUNK})
        step = math.gcd(tq, tk)
        d0s = list(range(d0s[0], d0s[-1] + 1, step))
        tiles = _bias_tiles_t(rel_bias, d0s, tq, tk)
        segs.append((pos0, seg_len, tk, d0s[0], step))
        args += [k, v, tiles]
        in_specs += [pl.BlockSpec((1, seg_len, hw), lambda bi, h, i: (bi, 0, h)),
                     pl.BlockSpec((1, seg_len, hw), lambda bi, h, i: (bi, 0, h)),
                     pl.BlockSpec((hb, len(d0s), tk, tq), lambda bi, h, i: (h, 0, 0, 0))]
        scratch += [pltpu.VMEM((seg_len, hw), BF16), pltpu.VMEM((seg_len // tk, hw, tk), BF16),
                    pltpu.VMEM((2 * hb, tk, tq), F32), pltpu.VMEM((2 * hb, tk, tq), F32)]
    scal = jnp.zeros((1, HEAD_W), F32).at[0, 0].set(lam).at[0, 1].set(out_scale)
    kern = functools.partial(_attn_kernel, segs=tuple(segs), q_off=q_off, tq=tq, hb=hb)
    out = pl.pallas_call(
        kern,
        out_shape=jax.ShapeDtypeStruct((b, tq_all, d), BF16),
        grid=(b, N_HEADS // hb, nq),
        in_specs=[pl.BlockSpec((1, tq, hw), lambda bi, h, i: (bi, i, h)),
                  pl.BlockSpec((1, HEAD_W), lambda bi, h, i: (0, 0)),
                  pl.BlockSpec((1, hw), lambda bi, h, i: (0, h))] + in_specs,
        out_specs=pl.BlockSpec((1, tq, hw), lambda bi, h, i: (bi, i, h)),
        scratch_shapes=[pltpu.VMEM((2 * hb, 1, tq), F32), pltpu.VMEM((2 * hb, 1, tq), F32),
                        pltpu.VMEM((2 * hb, HEAD_W, tq), F32)] + scratch,
        compiler_params=_cparams("parallel", "parallel", "arbitrary"),
        name=name,
    )(q, scal, g_subln.reshape(1, d), *args)
    return out[:, :t_real]


def _trunk(x, ple, q_off, past_k, past_v, hg_s0, conv_s0, wts, tag):
    (g_norms, w_in_a, lower, g_hg, w_out_a, g_kv, w_kv, rel_bias, w_q_b, lam_b,
     g_subln, w_out_b, w_ffn_in, conv_w, conv_b, w_ffn_out, w_ple, w_ple_gate) = wts
    b, t, d = x.shape
    depth = g_norms.shape[0]
    n_a = w_in_a.shape[0]
    m = b * t
    hg_out, conv_out = [], []
    k_new = v_new = None
    segments = None
    for i in range(depth):
        nm = f"{tag}{i}"
        x2 = x.reshape(m, d)
        if i < n_a:
            lw = lower[i]
            o, s = _gla(x, g_norms[i, 0], w_in_a[i], jnp.log(lw), jnp.log1p(-lw), 1.0 - lw,
                        g_hg[i], hg_s0[i], nm + "_gla")
            hg_out.append(s)
            w_mix = w_out_a[i]
        else:
            j = i - n_a
            q = _norm_matmul(x2, g_norms[i, 0], w_q_b[j], BF16, nm + "_q", scale=ATTN_Q_SCALE)
            lam_init = 0.8 - 0.6 * math.exp(-0.3 * i)
            lp = lam_b[j].astype(F32)
            lam = jnp.exp(jnp.sum(lp[0] * lp[1])) - jnp.exp(jnp.sum(lp[2] * lp[3])) + lam_init
            o = _attention(q.reshape(b, t, d), segments, rel_bias, lam, 1.0 - lam_init,
                           g_subln[j], q_off, nm + "_attn")
            w_mix = w_out_b[j]
        x, cbuf = _ffn(x, o.reshape(b, t, d), w_mix, g_norms[i, 1], g_norms[i, 2], w_ffn_in[i],
                       conv_w[i], conv_b[i], w_ffn_out[i], g_norms[i, 3], conv_s0[i],
                       ple[i], w_ple_gate[i], w_ple[i], nm + "_ffn")
        conv_out.append(cbuf)
        x2 = x.reshape(m, d)
        if i == n_a - 1:
            k_new, v_new = _norm_kv(x2, g_kv, w_kv, nm + "_kv")
            k_new = k_new.reshape(b, t, d)
            v_new = v_new.reshape(b, t, d)
            segments = []
            if past_k is not None:
                tp = past_k.shape[1]
                segments.append((past_k.reshape(b, tp, d), past_v.reshape(b, tp, d), 0, tp))
            segments.append((k_new, v_new, q_off, min(t, ATTN_KEY_TILE)))
    hd = (b, t, N_HEADS, HEAD_W)
    return x, k_new.reshape(hd), v_new.reshape(hd), jnp.stack(hg_out), jnp.stack(conv_out)


def kernel(x_prompt, x_sample, cache_k, cache_v, state_hgrn, state_conv, p_prompt, p_sample,
           g_norms, w_in_a, lb_raw, g_hg, w_out_a, g_kv, w_kv, rel_bias,
           w_q_b, lam_b, g_subln, w_out_b, w_ffn_in, conv_w, conv_b, w_ffn_out,
           w_ple, w_ple_gate):
    sm = jax.nn.softmax(lb_raw.astype(F32), axis=0)
    cs = jnp.cumsum(sm, axis=0)
    lower = cs - cs[0:1]
    bf = lambda w: w.astype(BF16)
    wts = (g_norms, bf(w_in_a), lower, g_hg, bf(w_out_a), g_kv, bf(w_kv), rel_bias,
           bf(w_q_b), lam_b, g_subln, bf(w_out_b), bf(w_ffn_in), conv_w, conv_b,
           bf(w_ffn_out), bf(w_ple), bf(w_ple_gate))
    bp, tp, _ = x_prompt.shape
    n_a, depth = w_in_a.shape[0], g_norms.shape[0]
    hg0 = jnp.zeros((n_a, bp) + state_hgrn.shape[2:], F32)
    cv0 = jnp.zeros((depth, bp) + state_conv.shape[2:], F32)
    y_p, k_p, v_p, hg_p, cv_p = _trunk(x_prompt, p_prompt, 0, None, None, hg0, cv0, wts, "p")
    y_s, k_s, v_s, hg_s, cv_s = _trunk(x_sample, p_sample, cache_k.shape[1], cache_k, cache_v,
                                       state_hgrn, state_conv, wts, "s")
    return (y_p, y_s, k_p, v_p, k_s, v_s, hg_p, hg_s, cv_p, cv_s)
```

```python
import functools
import math

import numpy as np
import jax
import jax.numpy as jnp
from jax import lax
from jax.experimental import pallas as pl
from jax.experimental.pallas import tpu as pltpu

F32 = jnp.float32
BF16 = jnp.bfloat16

D_MODEL = 1024
N_HEADS = 8
HEAD_W = D_MODEL // N_HEADS
DA_HEAD_DIM = HEAD_W // 2
CHUNK = 64
GLA_BLOCK = 16
NUM_BUCKETS = 32
MAX_DISTANCE = 256
CONV_W = 3
EPS = 1e-6
NEG_BIG = -1e30
LOG2E = math.log2(math.e)
ATTN_Q_SCALE = DA_HEAD_DIM ** -0.5 * LOG2E

V7X_VMEM_LIMIT_BYTES = 56 * 1024 * 1024

ROW_TILE = 1024
FFN_ROW_TILE = 1024
FFN_COL_TILE = 256
FFN_CONV_ROWS = 128
FFN_EDGE_ROWS = 256
GLA_TILE = 256
GLA_TILES_PER_STEP = 2
ATTN_TILE = 512
ATTN_KEY_TILE = 256
ATTN_HEADS_PER_STEP = 2


def _cparams(*sem):
    return pltpu.CompilerParams(dimension_semantics=sem,
                                vmem_limit_bytes=V7X_VMEM_LIMIT_BYTES)


def _rms(x, g):
    ms = jnp.mean(x * x, axis=-1, keepdims=True)
    return x * lax.rsqrt(ms + EPS) * g


def _sigmoid(z):
    return 1.0 / (1.0 + jnp.exp(-z))


def _row_tile(m, cap):
    t = min(m, cap)
    assert m % t == 0, (m, t)
    return t


def _norm_matmul_kernel(x_ref, g_ref, w_ref, o_ref, *, scale):
    h = _rms(x_ref[...], g_ref[...]).astype(BF16)
    n = w_ref.shape[1]
    tn = min(n, D_MODEL)
    for n0 in range(0, n, tn):
        y = jnp.dot(h, w_ref[:, n0:n0 + tn], preferred_element_type=F32)
        if scale != 1.0:
            y = y * scale
        o_ref[:, n0:n0 + tn] = y.astype(o_ref.dtype)


def _norm_matmul(x2d, g, w, out_dtype, name, scale=1.0):
    m, d = x2d.shape
    n = w.shape[1]
    tm = _row_tile(m, ROW_TILE)
    return pl.pallas_call(
        functools.partial(_norm_matmul_kernel, scale=scale),
        out_shape=jax.ShapeDtypeStruct((m, n), out_dtype),
        grid=(m // tm,),
        in_specs=[pl.BlockSpec((tm, d), lambda i: (i, 0)),
                  pl.BlockSpec((1, d), lambda i: (0, 0)),
                  pl.BlockSpec((d, n), lambda i: (0, 0))],
        out_specs=pl.BlockSpec((tm, n), lambda i: (i, 0)),
        compiler_params=_cparams("parallel"),
        name=name,
    )(x2d, g.reshape(1, d), w)


def _norm_kv_kernel(x_ref, g_ref, w_ref, k_ref, v_ref):
    d = k_ref.shape[1]
    h = _rms(x_ref[...], g_ref[...]).astype(BF16)
    k_ref[...] = jnp.dot(h, w_ref[:, :d], preferred_element_type=F32)
    v_ref[...] = jnp.dot(h, w_ref[:, d:], preferred_element_type=F32)


def _norm_kv(x2d, g, w, name):
    m, d = x2d.shape
    tm = _row_tile(m, ROW_TILE)
    out = jax.ShapeDtypeStruct((m, d), F32)
    return pl.pallas_call(
        _norm_kv_kernel,
        out_shape=(out, out),
        grid=(m // tm,),
        in_specs=[pl.BlockSpec((tm, d), lambda i: (i, 0)),
                  pl.BlockSpec((1, d), lambda i: (0, 0)),
                  pl.BlockSpec((d, 2 * d), lambda i: (0, 0))],
        out_specs=(pl.BlockSpec((tm, d), lambda i: (i, 0)),
                   pl.BlockSpec((tm, d), lambda i: (i, 0))),
        compiler_params=_cparams("parallel"),
        name=name,
    )(x2d, g.reshape(1, d), w)


def _ffn_kernel(x_ref, a_ref, wm_ref, gm_ref, gin_ref, wg_ref, wu_ref, cw_ref, cb_ref, wo_ref, gout_ref,
                cs_ref, p_ref, wgate_ref, wple_ref,
                o_ref, co_ref, xm_ref, h_ref, acc_ref, gu_ref, act_ref):
    t = pl.program_id(1)
    j = pl.program_id(2)
    nb, tm, d = x_ref.shape
    rows = nb * tm
    rb = min(rows, FFN_EDGE_ROWS)

    def chunk(ref, r0):
        return ref[0, r0:r0 + rb, :] if nb == 1 else ref[...].reshape(rows, ref.shape[2])

    @pl.when(j == 0)
    def _():
        for r0 in range(0, rows, rb):
            m = jnp.dot(chunk(a_ref, r0), wm_ref[...], preferred_element_type=F32)
            xm = chunk(x_ref, r0) + _rms(m, gm_ref[...])
            xm_ref[r0:r0 + rb, :] = xm
            h_ref[r0:r0 + rb, :] = _rms(xm, gin_ref[...]).astype(BF16)
        acc_ref[...] = jnp.zeros_like(acc_ref)

    @pl.when((j == 0) & (t == 0))
    def _():
        co_ref[...] = cs_ref[...]

    cw = cw_ref[...]
    cb = cb_ref[...]
    tf = cw.shape[1]
    gu_ref[:, :tf] = jnp.dot(h_ref[...], wg_ref[...], preferred_element_type=F32)
    gu_ref[:, tf:] = jnp.dot(h_ref[...], wu_ref[...], preferred_element_type=F32)
    r8 = lax.broadcasted_iota(jnp.int32, (8, tf), 0)
    rc = min(tm, FFN_CONV_ROWS)
    for bi in range(nb):
        base = bi * tm
        prev = co_ref[bi, j]
        co_ref[bi, j] = gu_ref[base + tm - (CONV_W - 1):base + tm, :tf]
        head = jnp.where(r8 == 6, prev[0:1, :], prev[1:2, :])
        for r0 in range(base, base + tm, rc):
            if r0 > base:
                head = gu_ref[r0 - 8:r0, :tf]
            gc = gu_ref[r0:r0 + rc, :tf]
            ext = jnp.concatenate([head, gc], axis=0)
            g1 = pltpu.roll(ext, 1, axis=0)[8:, :]
            g2 = pltpu.roll(ext, 2, axis=0)[8:, :]
            c = cb + cw[0:1, :] * g2 + cw[1:2, :] * g1 + cw[2:3, :] * gc
            act_ref[r0:r0 + rc, :] = (c * _sigmoid(c) * gu_ref[r0:r0 + rc, tf:]).astype(BF16)
    acc_ref[...] += jnp.dot(act_ref[...], wo_ref[...], preferred_element_type=F32)

    @pl.when(j == pl.num_programs(2) - 1)
    def _():
        for r0 in range(0, rows, rb):
            y = xm_ref[r0:r0 + rb, :] + _rms(acc_ref[r0:r0 + rb, :], gout_ref[...])
            z = jnp.dot(y.astype(BF16), wgate_ref[...], preferred_element_type=F32)
            e = jnp.dot(chunk(p_ref, r0).astype(BF16), wple_ref[...], preferred_element_type=F32)
            out = y + _sigmoid(z) * e
            if nb == 1:
                o_ref[0, r0:r0 + rb, :] = out
            else:
                o_ref[...] = out.reshape(nb, tm, d)


def _ffn(x, a, w_mix, g_mix, gin, w_in, cw, cb, w_out, gout, conv_state, p, w_gate, w_ple, name):
    b, t, d = x.shape
    f = w_out.shape[0]
    pd = p.shape[2]
    tm = _row_tile(t, FFN_ROW_TILE)
    nb = max(1, min(b, FFN_ROW_TILE // t))
    assert b % nb == 0
    rows = nb * tm
    assert rows % min(rows, FFN_EDGE_ROWS) == 0 and (nb == 1 or rows <= FFN_EDGE_ROWS)
    tf = FFN_COL_TILE
    nf = f // tf
    assert f % tf == 0 and t >= CONV_W - 1 and tm % 8 == 0
    cs = conv_state.reshape(b, CONV_W - 1, nf, tf).transpose(0, 2, 1, 3)
    cs_spec = pl.BlockSpec((nb, nf, CONV_W - 1, tf), lambda bi, ti, j: (bi, 0, 0, 0))
    row_spec = lambda width: pl.BlockSpec((nb, tm, width), lambda bi, ti, j: (bi, ti, 0))
    vec_spec = pl.BlockSpec((1, d), lambda bi, ti, j: (0, 0))
    const_spec = lambda shape: pl.BlockSpec(shape, lambda bi, ti, j: (0, 0),
                                            pipeline_mode=pl.Buffered(1))
    y, co = pl.pallas_call(
        _ffn_kernel,
        out_shape=(jax.ShapeDtypeStruct((b, t, d), F32),
                   jax.ShapeDtypeStruct(cs.shape, F32)),
        grid=(b // nb, t // tm, nf),
        in_specs=[row_spec(d), row_spec(d), const_spec((d, d)), vec_spec,
                  vec_spec,
                  pl.BlockSpec((d, tf), lambda bi, ti, j: (0, j)),
                  pl.BlockSpec((d, tf), lambda bi, ti, j: (0, j + nf)),
                  pl.BlockSpec((CONV_W, tf), lambda bi, ti, j: (0, j)),
                  pl.BlockSpec((1, tf), lambda bi, ti, j: (0, j)),
                  pl.BlockSpec((tf, d), lambda bi, ti, j: (j, 0)),
                  vec_spec,
                  cs_spec, row_spec(pd), const_spec((d, d)), const_spec((pd, d))],
        out_specs=(row_spec(d), cs_spec),
        scratch_shapes=[pltpu.VMEM((rows, d), F32),
                        pltpu.VMEM((rows, d), BF16),
                        pltpu.VMEM((rows, d), F32),
                        pltpu.VMEM((rows, 2 * tf), F32),
                        pltpu.VMEM((rows, tf), BF16)],
        compiler_params=_cparams("parallel", "arbitrary", "arbitrary"),
        name=name,
    )(x, a, w_mix, g_mix.reshape(1, d), gin.reshape(1, d), w_in, w_in, cw, cb.reshape(1, f), w_out,
      gout.reshape(1, d), cs, p, w_gate, w_ple)
    return y, co.transpose(0, 2, 1, 3).reshape(b, CONV_W - 1, f)


def _split3(x):
    hi = x.astype(BF16)
    r1 = x - hi.astype(F32)
    mid = r1.astype(BF16)
    lo = (r1 - mid.astype(F32)).astype(BF16)
    return hi, mid, lo


def _gla_levels(tc):
    return [s for s in (16, 32, 64, 128, 256, 512, 1024) if 2 * s <= tc]


def _gla_level_table(tc):
    r = np.arange(tc)[:, None]
    c = np.arange(tc)[None, :]
    table = np.full((tc, tc), -1, np.int32)
    table[(r // GLA_BLOCK == c // GLA_BLOCK) & (c <= r)] = 0
    for i, s in enumerate(_gla_levels(tc)):
        table[(r // (2 * s) == c // (2 * s)) & ((r // s) % 2 == 1) & ((c // s) % 2 == 0)] = i + 1
    return table


def _gla_kernel(x_ref, gx_ref, w_ref, ll_ref, l1m_ref, om_ref, gn_ref, s0_ref, lvl_ref,
                o_ref, so_ref, st_ref, proj_ref, bg_ref, fac_ref, qi_ref, ks_ref, v_ref, a_ref, oo_ref,
                *, tc):
    t = pl.program_id(1)
    d = x_ref.shape[2]
    n_sub = x_ref.shape[1] // tc

    def project(x_tile, slot):
        hx = _rms(x_tile, gx_ref[...]).astype(BF16)
        for n0 in range(0, 4 * d, d):
            proj_ref[slot, :, n0:n0 + d] = jnp.dot(hx, w_ref[:, n0:n0 + d], preferred_element_type=F32)

    @pl.when(t == 0)
    def _():
        for h in range(N_HEADS):
            st_ref[h] = s0_ref[0, h].T

    project(x_ref[0, 0:tc, :], 0)
    for s in range(n_sub):
        if s + 1 < n_sub:
            project(x_ref[0, (s + 1) * tc:(s + 2) * tc, :], (s + 1) % 2)
        _gla_tile(proj_ref.at[s % 2], ll_ref, l1m_ref, om_ref, gn_ref, lvl_ref, o_ref.at[0, s * tc:(s + 1) * tc, :],
                  st_ref, bg_ref, fac_ref, qi_ref, ks_ref, v_ref, a_ref, oo_ref)

    @pl.when(t == pl.num_programs(1) - 1)
    def _():
        for h in range(N_HEADS):
            so_ref[0, h] = st_ref[h].T


def _gla_tile(p_ref, ll_ref, l1m_ref, om_ref, gn_ref, lvl_ref, o_ref,
              st_ref, bg_ref, fac_ref, qi_ref, ks_ref, v_ref, a_ref, oo_ref):
    tc = p_ref.shape[0]
    d = p_ref.shape[1] // 4
    levels = _gla_levels(tc)

    qh = p_ref[:, 0:d]
    fp = p_ref[:, d:2 * d]
    q = qh * _sigmoid(qh)
    e = jnp.exp(-jnp.abs(fp))
    logsig = jnp.minimum(fp, 0.0) - jnp.log(1.0 + e)
    a = ll_ref[...]
    c = l1m_ref[...] + logsig
    logf = jnp.maximum(a, c) + jnp.log(1.0 + jnp.exp(-jnp.abs(a - c)))
    kk = om_ref[...] * (jnp.where(fp >= 0.0, e, 1.0) / (1.0 + e))

    row = lax.broadcasted_iota(jnp.int32, (tc, tc), 0)
    col = lax.broadcasted_iota(jnp.int32, (tc, tc), 1)
    m_incl = jnp.where(col <= row, 1.0, 0.0).astype(BF16)
    bg = sum(jnp.dot(m_incl, p, preferred_element_type=F32) for p in _split3(logf * LOG2E))
    bg_ref[...] = bg
    b_last = bg[tc - 1:tc, :]

    qi_ref[...] = (q * jnp.exp2(bg)).astype(BF16)
    ks_ref[...] = (kk * jnp.exp2(b_last - bg)).astype(BF16)
    v_ref[...] = p_ref[:, 2 * d:3 * d].astype(BF16)

    def ref_rows(span, offset):
        pieces = []
        for a0 in range(0, tc, span):
            r = a0 + offset
            src = bg_ref[r:r + 1, :] if r >= 0 else jnp.zeros((1, d), F32)
            pieces.append(jnp.broadcast_to(src, (span, d)))
        return pieces[0] if len(pieces) == 1 else jnp.concatenate(pieces, axis=0)

    dl = bg - ref_rows(GLA_BLOCK, -1)
    fac_ref[0] = (q * jnp.exp2(dl)).astype(BF16)
    fac_ref[1] = (kk * jnp.exp2(-dl)).astype(BF16)
    rid = lax.broadcasted_iota(jnp.int32, (tc, 1), 0)
    for li, s in enumerate(levels):
        mid = ref_rows(2 * s, s - 1)
        right = ((rid // s) % 2) == 1
        x = jnp.exp2(jnp.where(right, bg - mid, mid - bg))
        fac_ref[2 + 2 * li] = (q * x).astype(BF16)
        fac_ref[3 + 2 * li] = (kk * x).astype(BF16)

    nt = (((1,), (1,)), ((), ()))
    tn = (((0,), (0,)), ((), ()))
    lvl = lvl_ref[...]
    for h in range(N_HEADS):
        cols = slice(h * HEAD_W, (h + 1) * HEAD_W)
        att = jnp.zeros((tc, tc), F32)
        for ci in range(len(levels), -1, -1):
            prod = lax.dot_general(fac_ref[2 * ci, :, cols], fac_ref[2 * ci + 1, :, cols], nt,
                                   preferred_element_type=F32)
            att = jnp.where(lvl == ci, prod, att)
        a_ref[h] = att.astype(BF16)

    for h in range(N_HEADS):
        cols = slice(h * HEAD_W, (h + 1) * HEAD_W)
        vb = v_ref[:, cols]
        s_t = st_ref[h]
        o_inter = lax.dot_general(qi_ref[:, cols], s_t.astype(BF16), nt, preferred_element_type=F32)
        oo_ref[:, cols] = o_inter + jnp.dot(a_ref[h], vb, preferred_element_type=F32)
        upd = lax.dot_general(vb, ks_ref[:, cols], tn, preferred_element_type=F32)
        st_ref[h] = s_t * jnp.exp2(b_last[:, cols]) + upd

    gh = p_ref[:, 3 * d:4 * d]
    gate = gh * _sigmoid(gh)
    for h in range(N_HEADS):
        cols = slice(h * HEAD_W, (h + 1) * HEAD_W)
        oh = _rms(oo_ref[:, cols], gn_ref[:, cols])
        o_ref[:, cols] = (oh * gate[:, cols]).astype(o_ref.dtype)


def _gla(x, g_in, w_in, log_lower, log1m_lower, one_m_lower, g_out, s0, name):
    b, t, d = x.shape
    tc = _row_tile(t, GLA_TILE)
    assert tc % GLA_BLOCK == 0
    n_sub = GLA_TILES_PER_STEP if t % (GLA_TILES_PER_STEP * tc) == 0 else 1
    rows = n_sub * tc
    vec = pl.BlockSpec((1, d), lambda bi, ti: (0, 0))
    st_spec = pl.BlockSpec((1, N_HEADS, HEAD_W, HEAD_W), lambda bi, ti: (bi, 0, 0, 0))
    n_fac = 2 * (1 + len(_gla_levels(tc)))
    return pl.pallas_call(
        functools.partial(_gla_kernel, tc=tc),
        out_shape=(jax.ShapeDtypeStruct((b, t, d), BF16),
                   jax.ShapeDtypeStruct((b, N_HEADS, HEAD_W, HEAD_W), F32)),
        grid=(b, t // rows),
        in_specs=[pl.BlockSpec((1, rows, d), lambda bi, ti: (bi, ti, 0)),
                  vec,
                  pl.BlockSpec((d, 4 * d), lambda bi, ti: (0, 0), pipeline_mode=pl.Buffered(1)),
                  vec, vec, vec, vec, st_spec,
                  pl.BlockSpec((tc, tc), lambda bi, ti: (0, 0))],
        out_specs=(pl.BlockSpec((1, rows, d), lambda bi, ti: (bi, ti, 0)), st_spec),
        scratch_shapes=[pltpu.VMEM((N_HEADS, HEAD_W, HEAD_W), F32),
                        pltpu.VMEM((2, tc, 4 * d), F32),
                        pltpu.VMEM((tc, d), F32),
                        pltpu.VMEM((n_fac, tc, d), BF16),
                        pltpu.VMEM((tc, d), BF16), pltpu.VMEM((tc, d), BF16),
                        pltpu.VMEM((tc, d), BF16),
                        pltpu.VMEM((N_HEADS, tc, tc), BF16),
                        pltpu.VMEM((tc, d), F32)],
        compiler_params=_cparams("parallel", "arbitrary"),
        name=name,
    )(x, g_in.reshape(1, d), w_in, log_lower.reshape(1, d), log1m_lower.reshape(1, d),
      one_m_lower.reshape(1, d), g_out.reshape(1, d), s0, jnp.asarray(_gla_level_table(tc)))


def _t5_bucket(rel):
    half = NUM_BUCKETS // 2
    ret = jnp.where(rel > 0, half, 0)
    n = jnp.abs(rel)
    max_exact = half // 2
    nf = jnp.maximum(n, 1).astype(F32)
    large = max_exact + (jnp.log(nf / max_exact) / math.log(MAX_DISTANCE / max_exact)
                         * (half - max_exact)).astype(jnp.int32)
    large = jnp.minimum(large, half - 1)
    return ret + jnp.where(n < max_exact, n, large)


def _toeplitz_kernel(w_ref, o_ref, *, tq):
    tk = o_ref.shape[2]
    p = w_ref.shape[3]
    for di in range(o_ref.shape[1]):
        x = jnp.broadcast_to(w_ref[0, di], (tk, p))
        o_ref[0, di] = pltpu.roll(x, 0, axis=1, stride=1, stride_axis=0)[:, :tq]


def _bias_tiles_t(rel_bias, d0_list, tq, tk):
    p = pl.cdiv(tq + tk, HEAD_W) * HEAD_W
    m = jnp.arange(p, dtype=jnp.int32)
    rel = jnp.asarray(d0_list, jnp.int32)[:, None] + jnp.where(m <= tq, -m, p - m)[None, :]
    onehot = (_t5_bucket(rel)[:, :, None] == jnp.arange(NUM_BUCKETS, dtype=jnp.int32)).astype(F32)
    w = jnp.einsum("dpn,nh->hdp", onehot, rel_bias.astype(F32) * LOG2E,
                   precision=lax.Precision.HIGHEST)
    h, nd = w.shape[0], w.shape[1]
    return pl.pallas_call(
        functools.partial(_toeplitz_kernel, tq=tq),
        out_shape=jax.ShapeDtypeStruct((h, nd, tk, tq), F32),
        grid=(h,),
        in_specs=[pl.BlockSpec((1, nd, 1, p), lambda hi: (hi, 0, 0, 0))],
        out_specs=pl.BlockSpec((1, nd, tk, tq), lambda hi: (hi, 0, 0, 0)),
        compiler_params=_cparams("parallel"),
        name="bias_tiles",
    )(w.reshape(h, nd, 1, p))


def _attn_kernel(*refs, segs, q_off, tq, hb):
    nseg = len(segs)
    q_ref, lam_ref, gn_ref = refs[0], refs[1], refs[2]
    seg_refs = [refs[3 + 3 * s: 6 + 3 * s] for s in range(nseg)]
    o_ref = refs[3 + 3 * nseg]
    m_ref, l_ref, acc_ref = refs[4 + 3 * nseg: 7 + 3 * nseg]
    kv_scratch = refs[7 + 3 * nseg:]
    seg_scr = [kv_scratch[4 * s: 4 * s + 4] for s in range(nseg)]

    i = pl.program_id(2)
    q0 = q_off + i * tq

    @pl.when(i == 0)
    def _():
        for (k_ref, v_ref, _), (kb_ref, vt_ref, _, _), (_, seg_len, tk, _, _) in zip(seg_refs, seg_scr, segs):
            kb_ref[...] = k_ref[0].astype(BF16)
            for kt in range(seg_len // tk):
                vt_ref[kt] = v_ref[0, kt * tk:(kt + 1) * tk, :].astype(F32).T.astype(BF16)

    q = q_ref[0]
    lane = lax.broadcasted_iota(jnp.int32, (tq, HEAD_W), 1)
    qz = []
    for hh in range(hb):
        qh = q[:, hh * HEAD_W:(hh + 1) * HEAD_W]
        qz.append([jnp.where(lane < DA_HEAD_DIM, qh, 0).astype(BF16),
                   jnp.where(lane >= DA_HEAD_DIM, qh, 0).astype(BF16)])
    m_ref[...] = jnp.full_like(m_ref, NEG_BIG)
    l_ref[...] = jnp.zeros_like(l_ref)
    acc_ref[...] = jnp.zeros_like(acc_ref)
    q_chunk = (q0 + lax.broadcasted_iota(jnp.int32, (1, tq), 1)) // CHUNK
    vis_end = ((q0 + tq - 1) // CHUNK + 1) * CHUNK
    full_end = (q0 // CHUNK + 1) * CHUNK
    nt = (((1,), (1,)), ((), ()))

    for (_, _, b_ref), (kb_ref, vt_ref, sa_ref, sb_ref), (pos0, seg_len, tk, d0_min, d0_step) in zip(
            seg_refs, seg_scr, segs):
        n_tiles = seg_len // tk
        n_vis = jnp.clip((vis_end - pos0 + tk - 1) // tk, 0, n_tiles)
        n_full = jnp.clip((full_end - pos0) // tk, 0, n_vis)

        def scores_into(kt, s_ref, c0=0, kb_ref=kb_ref, tk=tk, n_tiles=n_tiles):
            r0 = pl.multiple_of(jnp.minimum(kt, n_tiles - 1) * tk, tk)
            for hh in range(hb):
                kb = kb_ref[pl.ds(r0, tk), hh * HEAD_W:(hh + 1) * HEAD_W]
                for c in range(2):
                    s_ref[2 * hh + c, :, c0:] = lax.dot_general(kb, qz[hh][c][c0:, :], nt,
                                                                preferred_element_type=F32)

        def consume(kt, s_ref, masked, c0=0, b_ref=b_ref, vt_ref=vt_ref, pos0=pos0, tk=tk,
                    d0_min=d0_min, d0_step=d0_step, n_tiles=n_tiles, n_vis=n_vis):
            ktc = jnp.minimum(kt, n_tiles - 1)
            r0 = ktc * tk
            d_idx = (pos0 + r0 - q0 - d0_min) // d0_step
            if masked:
                k0 = jnp.where(kt < n_vis, pos0 + r0, jnp.int32(1 << 28))
                k_chunk = (k0 + lax.broadcasted_iota(jnp.int32, (tk, 1), 0)) // CHUNK
                visible = k_chunk <= q_chunk[:, c0:]
            for hh in range(hb):
                vt = vt_ref[ktc, hh * HEAD_W:(hh + 1) * HEAD_W, :]
                bias = b_ref[hh, d_idx, :, c0:]
                for c in range(2):
                    j = 2 * hh + c
                    s = s_ref[j, :, c0:] + bias
                    if masked:
                        s = jnp.where(visible, s, NEG_BIG)
                    m_prev = m_ref[j, :, c0:]
                    m_new = jnp.maximum(m_prev, jnp.max(s, axis=0, keepdims=True))
                    p = jnp.exp2(s - m_new)
                    alpha = jnp.exp2(m_prev - m_new)
                    l_ref[j, :, c0:] = alpha * l_ref[j, :, c0:] + jnp.sum(p, axis=0, keepdims=True)
                    acc_ref[j, :, c0:] = alpha * acc_ref[j, :, c0:] + jnp.dot(
                        vt, p.astype(BF16), preferred_element_type=F32)
                    m_ref[j, :, c0:] = m_new

        def pair(u, carry, masked, base, sa_ref=sa_ref, sb_ref=sb_ref):
            t0 = base + 2 * u
            scores_into(t0 + 1, sb_ref)
            consume(t0, sa_ref, masked)
            scores_into(t0 + 2, sa_ref)
            consume(t0 + 1, sb_ref, masked)
            return carry

        n_pairs_full = n_full // 2
        base = 2 * n_pairs_full
        scores_into(0, sa_ref)
        if n_tiles == 1:
            consume(0, sa_ref, True)
            continue
        lax.fori_loop(0, n_pairs_full, functools.partial(pair, masked=False, base=0), 0)
        if pos0 == q_off and tq == 2 * tk and tk % CHUNK == 0 and q_off % CHUNK == 0:
            scores_into(base + 1, sb_ref, c0=tk)
            consume(base, sa_ref, True)
            consume(base + 1, sb_ref, True, c0=tk)
        else:
            lax.fori_loop(0, (n_vis - base + 1) // 2, functools.partial(pair, masked=True, base=base), 0)

    lam = lam_ref[0:1, 0:1]
    for hh in range(hb):
        o_t = (acc_ref[2 * hh] * (1.0 / l_ref[2 * hh])
               - lam * (acc_ref[2 * hh + 1] * (1.0 / l_ref[2 * hh + 1])))
        cols = slice(hh * HEAD_W, (hh + 1) * HEAD_W)
        ms = jnp.mean(o_t * o_t, axis=0, keepdims=True)
        o_n = (o_t * lax.rsqrt(ms + EPS)).T
        o_ref[0, :, cols] = (o_n * gn_ref[:, cols] * lam_ref[0:1, 1:2]).astype(o_ref.dtype)


def _attention(q, segments, rel_bias, lam, out_scale, g_subln, q_off, name):
    b, t_real, d = q.shape
    tq_all = max(t_real, HEAD_W)
    if tq_all != t_real:
        q = jnp.pad(q, ((0, 0), (0, tq_all - t_real), (0, 0)))
    tq = _row_tile(tq_all, ATTN_TILE)
    nq = tq_all // tq
    hb = 2 * ATTN_HEADS_PER_STEP if nq == 1 else ATTN_HEADS_PER_STEP
    hw = hb * HEAD_W
    segs, args, in_specs, scratch = [], [], [], []
    for (k, v, pos0, tk) in segments:
        seg_len = k.shape[1]
        assert seg_len % tk == 0
        d0s = sorted({pos0 + kt * tk - (q_off + i * tq)
                      for i in range(nq) for kt in range(seg_len // tk)
                      if pos0 + kt * tk < ((q_off + (i + 1) * tq - 1) // CHUNK + 1) * CHUNK})
        step = math.gcd(tq, tk)
        d0s = list(range(d0s[0], d0s[-1] + 1, step))
        tiles = _bias_tiles_t(rel_bias, d0s, tq, tk)
        segs.append((pos0, seg_len, tk, d0s[0], step))
        args += [k, v, tiles]
        in_specs += [pl.BlockSpec((1, seg_len, hw), lambda bi, h, i: (bi, 0, h)),
                     pl.BlockSpec((1, seg_len, hw), lambda bi, h, i: (bi, 0, h)),
                     pl.BlockSpec((hb, len(d0s), tk, tq), lambda bi, h, i: (h, 0, 0, 0))]
        scratch += [pltpu.VMEM((seg_len, hw), BF16), pltpu.VMEM((seg_len // tk, hw, tk), BF16),
                    pltpu.VMEM((2 * hb, tk, tq), F32), pltpu.VMEM((2 * hb, tk, tq), F32)]
    scal = jnp.zeros((1, HEAD_W), F32).at[0, 0].set(lam).at[0, 1].set(out_scale)
    kern = functools.partial(_attn_kernel, segs=tuple(segs), q_off=q_off, tq=tq, hb=hb)
    out = pl.pallas_call(
        kern,
        out_shape=jax.ShapeDtypeStruct((b, tq_all, d), BF16),
        grid=(b, N_HEADS // hb, nq),
        in_specs=[pl.BlockSpec((1, tq, hw), lambda bi, h, i: (bi, i, h)),
                  pl.BlockSpec((1, HEAD_W), lambda bi, h, i: (0, 0)),
                  pl.BlockSpec((1, hw), lambda bi, h, i: (0, h))] + in_specs,
        out_specs=pl.BlockSpec((1, tq, hw), lambda bi, h, i: (bi, i, h)),
        scratch_shapes=[pltpu.VMEM((2 * hb, 1, tq), F32), pltpu.VMEM((2 * hb, 1, tq), F32),
                        pltpu.VMEM((2 * hb, HEAD_W, tq), F32)] + scratch,
        compiler_params=_cparams("parallel", "parallel", "arbitrary"),
        name=name,
    )(q, scal, g_subln.reshape(1, d), *args)
    return out[:, :t_real]


def _trunk(x, ple, q_off, past_k, past_v, hg_s0, conv_s0, wts, tag):
    (g_norms, w_in_a, lower, g_hg, w_out_a, g_kv, w_kv, rel_bias, w_q_b, lam_b,
     g_subln, w_out_b, w_ffn_in, conv_w, conv_b, w_ffn_out, w_ple, w_ple_gate) = wts
    b, t, d = x.shape
    depth = g_norms.shape[0]
    n_a = w_in_a.shape[0]
    m = b * t
    hg_out, conv_out = [], []
    k_new = v_new = None
    segments = None
    for i in range(depth):
        nm = f"{tag}{i}"
        x2 = x.reshape(m, d)
        if i < n_a:
            lw = lower[i]
            o, s = _gla(x, g_norms[i, 0], w_in_a[i], jnp.log(lw), jnp.log1p(-lw), 1.0 - lw,
                        g_hg[i], hg_s0[i], nm + "_gla")
            hg_out.append(s)
            w_mix = w_out_a[i]
        else:
            j = i - n_a
            q = _norm_matmul(x2, g_norms[i, 0], w_q_b[j], BF16, nm + "_q", scale=ATTN_Q_SCALE)
            lam_init = 0.8 - 0.6 * math.exp(-0.3 * i)
            lp = lam_b[j].astype(F32)
            lam = jnp.exp(jnp.sum(lp[0] * lp[1])) - jnp.exp(jnp.sum(lp[2] * lp[3])) + lam_init
            o = _attention(q.reshape(b, t, d), segments, rel_bias, lam, 1.0 - lam_init,
                           g_subln[j], q_off, nm + "_attn")
            w_mix = w_out_b[j]
        x, cbuf = _ffn(x, o.reshape(b, t, d), w_mix, g_norms[i, 1], g_norms[i, 2], w_ffn_in[i],
                       conv_w[i], conv_b[i], w_ffn_out[i], g_norms[i, 3], conv_s0[i],
                       ple[i], w_ple_gate[i], w_ple[i], nm + "_ffn")
        conv_out.append(cbuf)
        x2 = x.reshape(m, d)
        if i == n_a - 1:
            k_new, v_new = _norm_kv(x2, g_kv, w_kv, nm + "_kv")
            k_new = k_new.reshape(b, t, d)
            v_new = v_new.reshape(b, t, d)
            segments = []
            if past_k is not None:
                tp = past_k.shape[1]
                segments.append((past_k.reshape(b, tp, d), past_v.reshape(b, tp, d), 0, tp))
            segments.append((k_new, v_new, q_off, min(t, ATTN_KEY_TILE)))
    hd = (b, t, N_HEADS, HEAD_W)
    return x, k_new.reshape(hd), v_new.reshape(hd), jnp.stack(hg_out), jnp.stack(conv_out)


def kernel(x_prompt, x_sample, cache_k, cache_v, state_hgrn, state_conv, p_prompt, p_sample,
           g_norms, w_in_a, lb_raw, g_hg, w_out_a, g_kv, w_kv, rel_bias,
           w_q_b, lam_b, g_subln, w_out_b, w_ffn_in, conv_w, conv_b, w_ffn_out,
           w_ple, w_ple_gate):
    sm = jax.nn.softmax(lb_raw.astype(F32), axis=0)
    cs = jnp.cumsum(sm, axis=0)
    lower = cs - cs[0:1]
    bf = lambda w: w.astype(BF16)
    wts = (g_norms, bf(w_in_a), lower, g_hg, bf(w_out_a), g_kv, bf(w_kv), rel_bias,
           bf(w_q_b), lam_b, g_subln, bf(w_out_b), bf(w_ffn_in), conv_w, conv_b,
           bf(w_ffn_out), bf(w_ple), bf(w_ple_gate))
    bp, tp, _ = x_prompt.shape
    n_a, depth = w_in_a.shape[0], g_norms.shape[0]
    hg0 = jnp.zeros((n_a, bp) + state_hgrn.shape[2:], F32)
    cv0 = jnp.zeros((depth, bp) + state_conv.shape[2:], F32)
    y_p, k_p, v_p, hg_p, cv_p = _trunk(x_prompt, p_prompt, 0, None, None, hg0, cv0, wts, "p")
    y_s, k_s, v_s, hg_s, cv_s = _trunk(x_sample, p_sample, cache_k.shape[1], cache_k, cache_v,
                                       state_hgrn, state_conv, wts, "s")
    return (y_p, y_s, k_p, v_p, k_s, v_s, hg_p, hg_s, cv_p, cv_s)
```
